```python
import jax, jax.numpy as jnp
from jax import lax
import numpy as np

D_MODEL = 1024
BATCH = 16
SEQ = 4096
DEPTH = 2
DEC_BATCH = 8
DEC_SEQ = 16
PAST_LEN = 4096

CHUNK = 64
Q_BLOCK = 128
EPS = 1e-6
N_EVEN = (DEPTH + 1) // 2
N_ODD = DEPTH // 2
FOX_HEADS = 8
FOX_DH = D_MODEL // 16
FOX_W = FOX_HEADS * FOX_DH
FOX_SCALE = FOX_DH ** -0.5
MLSTM_HEADS = 4
MLSTM_DV = D_MODEL // 8
MLSTM_DK = MLSTM_DV // 2
MLSTM_W = MLSTM_HEADS * MLSTM_DV
MLSTM_QK_W = MLSTM_HEADS * MLSTM_DK
MLSTM_SCALE = MLSTM_DK ** -0.5
MIX_W_EVEN = FOX_W + MLSTM_W
CONV_W = 3
MEM_TOKENS = 256
MEM_HEADS = 4
MEM_DH = D_MODEL // 8
MEM_W = MEM_HEADS * MEM_DH
MEM_SCALE = MEM_DH ** -0.5
D_FF = 7 * D_MODEL // 2
N_EXPERTS = 8
TOP_K = 2
EVEN_SIZES = (FOX_W, FOX_W, FOX_W, FOX_HEADS, MLSTM_QK_W, MLSTM_QK_W, MLSTM_W, MLSTM_W, MLSTM_HEADS, MLSTM_HEADS)
EVEN_SPLITS = tuple(sum(EVEN_SIZES[:i + 1]) for i in range(len(EVEN_SIZES) - 1))
D_IN_EVEN = sum(EVEN_SIZES)

kernel_name = 'streaming_fox_mlstm_shortconv_moe_step'


def rmsnorm(x, g):
    xf = x.astype(jnp.float32)
    xf = xf * lax.rsqrt(jnp.mean(xf * xf, axis=-1, keepdims=True) + EPS)
    return (xf * g.astype(jnp.float32)).astype(x.dtype)


def fox_block(q_blk, q_pos, f_q, k, v, f_k):
    s = jnp.einsum('bqhd,bshd->bhqs', q_blk, k, preferred_element_type=jnp.float32) * FOX_SCALE
    bias = f_q.transpose(0, 2, 1)[:, :, :, None] - f_k[:, :, None, :]
    mask = jnp.arange(k.shape[1])[None, :] <= q_pos[:, None]
    p = jax.nn.softmax(jnp.where(mask, s + bias, -jnp.inf), axis=-1)
    return jnp.einsum('bhqs,bshd->bqhd', p.astype(v.dtype), v)


def fox_prompt(q, k, v, logf):
    B, S = q.shape[:2]
    F = jnp.cumsum(logf, axis=1)
    nb = S // Q_BLOCK
    qb = q.reshape(B, nb, Q_BLOCK, FOX_HEADS, FOX_DH).swapaxes(0, 1)
    fb = F.reshape(B, nb, Q_BLOCK, FOX_HEADS).swapaxes(0, 1)
    pos = jnp.arange(S, dtype=jnp.int32).reshape(nb, Q_BLOCK)
    f_k = F.transpose(0, 2, 1)
    out = lax.map(lambda a: fox_block(a[0], a[1], a[2], k, v, f_k), (qb, pos, fb))
    return out.swapaxes(0, 1).reshape(B, S, FOX_HEADS, FOX_DH)


def fox_sample(q, k, v, logf, cache_k, cache_v, cache_logf):
    P, T = cache_k.shape[1], q.shape[1]
    k_all = jnp.concatenate([cache_k.astype(k.dtype), k], axis=1)
    v_all = jnp.concatenate([cache_v.astype(v.dtype), v], axis=1)
    F = jnp.cumsum(jnp.concatenate([cache_logf.astype(jnp.float32), logf], axis=1), axis=1)
    q_pos = P + jnp.arange(T, dtype=jnp.int32)
    return fox_block(q, q_pos, F[:, P:], k_all, v_all, F.transpose(0, 2, 1))


def mlstm_chunk(carry, inp):
    c, n, m = carry
    q, k, v, ig, lf = inp
    L = q.shape[1]
    b = jnp.cumsum(lf, axis=1).transpose(0, 2, 1)
    ig_h = ig.transpose(0, 2, 1)
    causal = jnp.tril(jnp.ones((L, L), dtype=bool))
    d = jnp.where(causal, b[..., :, None] - b[..., None, :] + ig_h[..., None, :], -jnp.inf)
    inter = b + m[..., None]
    m_t = jnp.maximum(inter, jnp.max(d, axis=-1))
    w_intra = jnp.exp(d - m_t[..., None])
    w_inter = jnp.exp(inter - m_t)
    a = jnp.einsum('blhk,bshk->bhls', q, k) * w_intra
    num = w_inter[..., None] * jnp.einsum('bhvk,blhk->bhlv', c, q) + jnp.einsum('bhls,bshv->bhlv', a, v)
    den = w_inter * jnp.einsum('bhk,blhk->bhl', n, q) + jnp.sum(a, axis=-1)
    h = num / jnp.maximum(jnp.abs(den), jnp.exp(-m_t))[..., None]
    b_last = b[..., -1]
    g = b_last[..., None] - b + ig_h
    m_new = jnp.maximum(b_last + m, jnp.max(g, axis=-1))
    w_state = jnp.exp(g - m_new[..., None])
    decay = jnp.exp(b_last + m - m_new)
    c_new = decay[..., None, None] * c + jnp.einsum('bhs,bshv,bshk->bhvk', w_state, v, k)
    n_new = decay[..., None] * n + jnp.einsum('bhs,bshk->bhk', w_state, k)
    return (c_new, n_new, m_new), h.transpose(0, 2, 1, 3)


def mlstm_prompt(q, k, v, ig, lf):
    B, S = q.shape[:2]
    nc = S // CHUNK
    to_chunks = lambda a: a.reshape((B, nc, CHUNK) + a.shape[2:]).swapaxes(0, 1)
    init = (jnp.zeros((B, MLSTM_HEADS, MLSTM_DV, MLSTM_DK), jnp.float32),
            jnp.zeros((B, MLSTM_HEADS, MLSTM_DK), jnp.float32),
            jnp.zeros((B, MLSTM_HEADS), jnp.float32))
    state, h = lax.scan(mlstm_chunk, init, (to_chunks(q), to_chunks(k), to_chunks(v), to_chunks(ig), to_chunks(lf)))
    return state, h.swapaxes(0, 1).reshape(B, S, MLSTM_HEADS, MLSTM_DV)


def even_mixer(xn, w_in, b_f_fox, b_i, b_f_m, norm_g, w_out, cache):
    B, T, _ = xn.shape
    fq, fk, fv, ff, mq, mk, mv, mo, mi, mf = jnp.split(xn @ w_in, list(EVEN_SPLITS), axis=-1)
    fq = fq.reshape(B, T, FOX_HEADS, FOX_DH)
    fk = fk.reshape(B, T, FOX_HEADS, FOX_DH)
    fv = fv.reshape(B, T, FOX_HEADS, FOX_DH)
    logf = jax.nn.log_sigmoid(ff.astype(jnp.float32) + b_f_fox.astype(jnp.float32))
    mq = mq.reshape(B, T, MLSTM_HEADS, MLSTM_DK).astype(jnp.float32) * MLSTM_SCALE
    mk = mk.reshape(B, T, MLSTM_HEADS, MLSTM_DK).astype(jnp.float32)
    mv = mv.reshape(B, T, MLSTM_HEADS, MLSTM_DV).astype(jnp.float32)
    ig = mi.astype(jnp.float32) + b_i.astype(jnp.float32)
    lf_m = jax.nn.log_sigmoid(mf.astype(jnp.float32) + b_f_m.astype(jnp.float32))
    if cache is None:
        fox_out = fox_prompt(fq, fk, fv, logf)
        (c, n, m), h_cell = mlstm_prompt(mq, mk, mv, ig, lf_m)
    else:
        ck, cv, clf, c0, n0, m0 = cache
        fox_out = fox_sample(fq, fk, fv, logf, ck, cv, clf)
        (c, n, m), h_cell = mlstm_chunk((c0.astype(jnp.float32), n0.astype(jnp.float32), m0.astype(jnp.float32)),
                                        (mq, mk, mv, ig, lf_m))
    h_cell = h_cell * lax.rsqrt(jnp.mean(h_cell * h_cell, axis=-1, keepdims=True) + EPS)
    h_m = h_cell.reshape(B, T, MLSTM_W) * norm_g.astype(jnp.float32) * jax.nn.sigmoid(mo.astype(jnp.float32))
    y = jnp.concatenate([fox_out.reshape(B, T, FOX_W).astype(xn.dtype), h_m.astype(xn.dtype)], axis=-1) @ w_out
    return y, (fk, fv, logf, c, n, m)


def odd_mixer(xn, w_in, conv_w, w_out, conv_state):
    B, T, _ = xn.shape
    gate_b, gate_c, hx = jnp.split(xn @ w_in, 3, axis=-1)
    u = gate_c * hx
    if conv_state is None:
        past = jnp.zeros((B, CONV_W - 1, D_MODEL), u.dtype)
    else:
        past = conv_state.astype(u.dtype)
    u_pad = jnp.concatenate([past, u], axis=1)
    conv = conv_w[0] * u_pad[:, 0:T]
    for j in range(1, CONV_W):
        conv = conv + conv_w[j] * u_pad[:, j:j + T]
    y = (gate_b * conv) @ w_out
    return y, u_pad[:, -(CONV_W - 1):]


def memory_kv(mem, norm_g, wk, wv):
    B, M, _ = mem.shape
    ks, vs = [], []
    for layer in range(DEPTH):
        mn = rmsnorm(mem, norm_g[layer])
        ks.append((mn @ wk[layer]).reshape(B, M, MEM_HEADS, MEM_DH))
        vs.append((mn @ wv[layer]).reshape(B, M, MEM_HEADS, MEM_DH))
    return jnp.stack(ks, axis=0), jnp.stack(vs, axis=0)


def cross_attention(xn, wq, wo, mk, mv):
    B, T, _ = xn.shape
    q = (xn @ wq).reshape(B, T, MEM_HEADS, MEM_DH)
    s = jnp.einsum('bthd,bmhd->bhtm', q, mk.astype(q.dtype), preferred_element_type=jnp.float32) * MEM_SCALE
    p = jax.nn.softmax(s, axis=-1).astype(q.dtype)
    o = jnp.einsum('bhtm,bmhd->bthd', p, mv.astype(q.dtype))
    return o.reshape(B, T, MEM_W) @ wo


def swiglu(xn, wg, wu, wd):
    return (jax.nn.silu(xn @ wg) * (xn @ wu)) @ wd


def moe_swiglu(xn, router_w, wg, wu, wd):
    logits = (xn @ router_w).astype(jnp.float32)
    top_v, top_i = lax.top_k(logits, TOP_K)
    gates = jax.nn.softmax(top_v, axis=-1)
    combine = jnp.sum(jax.nn.one_hot(top_i, N_EXPERTS, dtype=jnp.float32) * gates[..., None], axis=-2)
    out = jnp.zeros_like(xn)
    for e in range(N_EXPERTS):
        out = out + swiglu(xn, wg[e], wu[e], wd[e]) * combine[..., e:e + 1].astype(xn.dtype)
    return out


def run_trunk(h, mem_k, mem_v, caches, p):
    fox_k, fox_v, fox_lf, m_c, m_n, m_m, conv_st = [], [], [], [], [], [], []
    for layer in range(DEPTH):
        idx = layer // 2
        xn = rmsnorm(h, p['norm_mix_g'][layer])
        if layer % 2 == 0:
            cache = None if caches is None else tuple(c[idx] for c in caches[:6])
            y, (fk, fv, lf, c, n, m) = even_mixer(xn, p['even_w_in'][idx], p['fox_b_f'][idx], p['mlstm_b_i'][idx],
                                                  p['mlstm_b_f'][idx], p['mlstm_norm_g'][idx], p['even_w_out'][idx], cache)
            fox_k.append(fk); fox_v.append(fv); fox_lf.append(lf); m_c.append(c); m_n.append(n); m_m.append(m)
        else:
            st = None if caches is None else caches[6][idx]
            y, new_st = odd_mixer(xn, p['odd_w_in'][idx], p['conv_w'][idx], p['odd_w_out'][idx], st)
            conv_st.append(new_st)
        h = h + y
        h = h + cross_attention(rmsnorm(h, p['norm_cross_g'][layer]), p['mem_wq'][layer], p['mem_wo'][layer],
                                mem_k[layer], mem_v[layer])
        xn = rmsnorm(h, p['norm_ffn_g'][layer])
        if layer % 2 == 0:
            h = h + swiglu(xn, p['ffn_w_gate'][idx], p['ffn_w_up'][idx], p['ffn_w_down'][idx])
        else:
            h = h + moe_swiglu(xn, p['router_w'][idx], p['moe_w_gate'][idx], p['moe_w_up'][idx], p['moe_w_down'][idx])
    y = rmsnorm(h, p['final_norm_g'])
    st = (jnp.stack(fox_k), jnp.stack(fox_v), jnp.stack(fox_lf), jnp.stack(m_c), jnp.stack(m_n), jnp.stack(m_m),
          jnp.stack(conv_st))
    return y, st


def setup_inputs(seed: int = 0) -> dict:
    key = jax.random.key(seed)
    ks = iter(jax.random.split(key, 48))
    f32 = jnp.float32

    def nrm(shape, scale=1.0):
        return jax.random.normal(next(ks), shape, f32) * scale

    def w(shape, fan_in):
        return nrm(shape, fan_in ** -0.5)

    def gain(shape):
        return 1.0 + nrm(shape, 0.01)

    def unif(shape, lo, hi):
        return jax.random.uniform(next(ks), shape, f32, minval=lo, maxval=hi)

    return {
        'x_prompt': nrm((BATCH, SEQ, D_MODEL)),
        'x_sample': nrm((DEC_BATCH, DEC_SEQ, D_MODEL)),
        'mem_prompt': nrm((BATCH, MEM_TOKENS, D_MODEL)),
        'cache_fox_k': nrm((N_EVEN, DEC_BATCH, PAST_LEN, FOX_HEADS, FOX_DH)),
        'cache_fox_v': nrm((N_EVEN, DEC_BATCH, PAST_LEN, FOX_HEADS, FOX_DH)),
        'cache_fox_logf': jax.nn.log_sigmoid(3.0 + nrm((N_EVEN, DEC_BATCH, PAST_LEN, FOX_HEADS))),
        'state_mlstm_c': nrm((N_EVEN, DEC_BATCH, MLSTM_HEADS, MLSTM_DV, MLSTM_DK), 0.1),
        'state_mlstm_n': nrm((N_EVEN, DEC_BATCH, MLSTM_HEADS, MLSTM_DK), 0.1),
        'state_mlstm_m': unif((N_EVEN, DEC_BATCH, MLSTM_HEADS), 0.0, 2.0),
        'state_conv': nrm((N_ODD, DEC_BATCH, CONV_W - 1, D_MODEL)),
        'cache_mem_k': nrm((DEPTH, DEC_BATCH, MEM_TOKENS, MEM_HEADS, MEM_DH)),
        'cache_mem_v': nrm((DEPTH, DEC_BATCH, MEM_TOKENS, MEM_HEADS, MEM_DH)),
        'norm_mix_g': gain((DEPTH, D_MODEL)),
        'norm_mem_g': gain((DEPTH, D_MODEL)),
        'norm_cross_g': gain((DEPTH, D_MODEL)),
        'norm_ffn_g': gain((DEPTH, D_MODEL)),
        'final_norm_g': gain((D_MODEL,)),
        'even_w_in': w((N_EVEN, D_MODEL, D_IN_EVEN), D_MODEL),
        'fox_b_f': unif((N_EVEN, FOX_HEADS), 1.0, 5.0),
        'mlstm_b_i': nrm((N_EVEN, MLSTM_HEADS), 0.1),
        'mlstm_b_f': unif((N_EVEN, MLSTM_HEADS), 3.0, 6.0),
        'mlstm_norm_g': gain((N_EVEN, MLSTM_W)),
        'even_w_out': w((N_EVEN, MIX_W_EVEN, D_MODEL), MIX_W_EVEN),
        'odd_w_in': w((N_ODD, D_MODEL, 3 * D_MODEL), D_MODEL),
        'conv_w': w((N_ODD, CONV_W, D_MODEL), CONV_W),
        'odd_w_out': w((N_ODD, D_MODEL, D_MODEL), D_MODEL),
        'mem_wq': w((DEPTH, D_MODEL, MEM_W), D_MODEL),
        'mem_wk': w((DEPTH, D_MODEL, MEM_W), D_MODEL),
        'mem_wv': w((DEPTH, D_MODEL, MEM_W), D_MODEL),
        'mem_wo': w((DEPTH, MEM_W, D_MODEL), MEM_W),
        'ffn_w_gate': w((N_EVEN, D_MODEL, D_FF), D_MODEL),
        'ffn_w_up': w((N_EVEN, D_MODEL, D_FF), D_MODEL),
        'ffn_w_down': w((N_EVEN, D_FF, D_MODEL), D_FF),
        'router_w': w((N_ODD, D_MODEL, N_EXPERTS), D_MODEL),
        'moe_w_gate': w((N_ODD, N_EXPERTS, D_MODEL, D_FF), D_MODEL),
        'moe_w_up': w((N_ODD, N_EXPERTS, D_MODEL, D_FF), D_MODEL),
        'moe_w_down': w((N_ODD, N_EXPERTS, D_FF, D_MODEL), D_FF),
    }


def reference(x_prompt, x_sample, mem_prompt, cache_fox_k, cache_fox_v, cache_fox_logf, state_mlstm_c, state_mlstm_n,
              state_mlstm_m, state_conv, cache_mem_k, cache_mem_v, norm_mix_g, norm_mem_g, norm_cross_g, norm_ffn_g,
              final_norm_g, even_w_in, fox_b_f, mlstm_b_i, mlstm_b_f, mlstm_norm_g, even_w_out, odd_w_in, conv_w,
              odd_w_out, mem_wq, mem_wk, mem_wv, mem_wo, ffn_w_gate, ffn_w_up, ffn_w_down, router_w, moe_w_gate,
              moe_w_up, moe_w_down):
    params = dict(norm_mix_g=norm_mix_g, norm_cross_g=norm_cross_g, norm_ffn_g=norm_ffn_g, final_norm_g=final_norm_g,
                  even_w_in=even_w_in, fox_b_f=fox_b_f, mlstm_b_i=mlstm_b_i, mlstm_b_f=mlstm_b_f,
                  mlstm_norm_g=mlstm_norm_g, even_w_out=even_w_out, odd_w_in=odd_w_in, conv_w=conv_w,
                  odd_w_out=odd_w_out, mem_wq=mem_wq, mem_wo=mem_wo, ffn_w_gate=ffn_w_gate, ffn_w_up=ffn_w_up,
                  ffn_w_down=ffn_w_down, router_w=router_w, moe_w_gate=moe_w_gate, moe_w_up=moe_w_up,
                  moe_w_down=moe_w_down)
    mem_k_p, mem_v_p = memory_kv(mem_prompt, norm_mem_g, mem_wk, mem_wv)
    y_prompt, (p_fk, p_fv, p_flf, p_c, p_n, p_m, p_conv) = run_trunk(x_prompt, mem_k_p, mem_v_p, None, params)
    caches = (cache_fox_k, cache_fox_v, cache_fox_logf, state_mlstm_c, state_mlstm_n, state_mlstm_m, state_conv)
    y_sample, (s_fk, s_fv, s_flf, s_c, s_n, s_m, s_conv) = run_trunk(x_sample, cache_mem_k, cache_mem_v, caches, params)
    return (y_prompt, y_sample, p_fk, p_fv, p_flf, p_c, p_n, p_m, p_conv, mem_k_p, mem_v_p,
            s_fk, s_fv, s_flf, s_c, s_n, s_m, s_conv)
```

```python
import functools

import jax
import jax.numpy as jnp
from jax import lax
from jax.experimental import pallas as pl
from jax.experimental.pallas import tpu as pltpu

F32 = jnp.float32
BF16 = jnp.bfloat16

D_MODEL = 1024
EPS = 1e-6
FOX_HEADS = 8
FOX_DH = 64
FOX_W = FOX_HEADS * FOX_DH
FOX_SCALE = FOX_DH ** -0.5
MLSTM_HEADS = 4
MLSTM_DV = 128
MLSTM_DK = 64
MLSTM_W = MLSTM_HEADS * MLSTM_DV
MLSTM_QK_W = MLSTM_HEADS * MLSTM_DK
MLSTM_SCALE = MLSTM_DK ** -0.5
MLSTM_CHUNK = 64
MLSTM_EXT = 256
MEM_TOKENS = 256
MEM_HEADS = 4
MEM_DH = 128
MEM_W = MEM_HEADS * MEM_DH
MEM_SCALE = MEM_DH ** -0.5
D_FF = 3584
N_EXPERTS = 8
CONV_W = 3
EVEN_SIZES = (FOX_W, FOX_W, FOX_W, FOX_HEADS, MLSTM_QK_W, MLSTM_QK_W, MLSTM_W, MLSTM_W, MLSTM_HEADS, MLSTM_HEADS)
EVEN_SPLITS = tuple(sum(EVEN_SIZES[:i + 1]) for i in range(len(EVEN_SIZES) - 1))

LANES = 128
SUBLANES = 8
FOX_TILE = 256
FF_TILE = 512
_C_Q, _C_K, _C_V, _C_MQ, _C_MK, _C_MV, _C_MO, _C_G, _C_END = 0, 512, 1024, 1536, 1792, 2048, 2560, 3072, 3200
_G_FOX, _G_IG, _G_LF, _G_END = 0, 8, 12, 16
NEG = -1e30


def _cp(sem, vmem_mb):
    return pltpu.CompilerParams(dimension_semantics=sem, vmem_limit_bytes=vmem_mb * 1024 * 1024)


def _rms(x, g):
    return x * lax.rsqrt(jnp.mean(x * x, axis=-1, keepdims=True) + EPS) * g


def _log_sigmoid(x):
    return jnp.minimum(x, 0.0) - jnp.log1p(jnp.exp(-jnp.abs(x)))


def _sigmoid(x):
    return 1.0 / (1.0 + jnp.exp(-x))


def _cumsum_rows(x):
    n = x.shape[0]
    row = lax.broadcasted_iota(jnp.int32, x.shape, 0)
    s = 1
    while s < n:
        x = x + jnp.where(row >= s, pltpu.roll(x, s, axis=0), 0.0)
        s *= 2
    return x


def _split3(f):
    hi = f.astype(BF16).astype(F32)
    r = f - hi
    mid = r.astype(BF16).astype(F32)
    return hi, mid, r - mid


def _head_block(src, h, lane):
    p, odd = divmod(h, 2)
    blk = src[:, LANES * p:LANES * (p + 1)]
    return pltpu.roll(blk, FOX_DH, axis=1) if odd else blk


def _fox_aug(src, cum_f, h, lane, is_query):
    blk = _head_block(src, h, lane)
    hi, mid, lo = _split3(jnp.broadcast_to(cum_f[:, h:h + 1], blk.shape))
    if is_query:
        aug = jnp.where(lane == 64, hi, jnp.where(lane == 65, mid, jnp.where(lane == 66, lo,
                        jnp.where(lane < 70, 1.0, 0.0))))
    else:
        aug = jnp.where(lane < 67, 1.0, jnp.where(lane == 67, -hi, jnp.where(lane == 68, -mid,
                        jnp.where(lane == 69, -lo, 0.0))))
    return jnp.where(lane < FOX_DH, blk, aug).astype(BF16)


def _even_in_kernel(x_ref, g_ref, w_ref, b_ref, f0_ref,
                    qa_ref, ka_ref, fk_ref, fv_ref, vb_ref, mq_ref, mk_ref, mv_ref, mo_ref, gt_ref,
                    carry_ref, *, tiles_per_batch):
    @pl.when(pl.program_id(0) % tiles_per_batch == 0)
    def _():
        carry_ref[...] = f0_ref[...]

    xn = _rms(x_ref[...], g_ref[...]).astype(BF16)
    acc = jnp.dot(xn, w_ref[...], preferred_element_type=F32)
    tm = acc.shape[0]
    lane = lax.broadcasted_iota(jnp.int32, (tm, LANES), 1)

    gates = acc[:, _C_G:_C_END] + b_ref[...]
    ls = _log_sigmoid(gates)
    is_ig = (lane >= _G_IG) & (lane < _G_LF)
    gt_ref[...] = jnp.where(is_ig, gates, jnp.where(lane < _G_END, ls, 0.0))
    cum_f = _cumsum_rows(jnp.where(lane < _G_IG, ls, 0.0)) + carry_ref[...]
    carry_ref[...] = cum_f[tm - 1:tm, :]

    q_all = acc[:, _C_Q:_C_K] * FOX_SCALE
    k_all = acc[:, _C_K:_C_V]
    v_all = acc[:, _C_V:_C_MQ]
    for h in range(FOX_HEADS):
        qa_ref[:, LANES * h:LANES * (h + 1)] = _fox_aug(q_all, cum_f, h, lane, True)
        ka_ref[:, LANES * h:LANES * (h + 1)] = _fox_aug(k_all, cum_f, h, lane, False)
    fk_ref[...] = k_all
    fv_ref[...] = v_all
    vb_ref[...] = v_all.astype(BF16)

    mq_all = acc[:, _C_MQ:_C_MK] * MLSTM_SCALE
    mk_all = acc[:, _C_MK:_C_MV]
    for h in range(MLSTM_HEADS):
        mq_ref[:, LANES * h:LANES * (h + 1)] = jnp.where(lane < MLSTM_DK, _head_block(mq_all, h, lane), 0.0).astype(BF16)
        mk_ref[:, LANES * h:LANES * (h + 1)] = jnp.where(lane < MLSTM_DK, _head_block(mk_all, h, lane), 0.0).astype(BF16)
    mv_ref[...] = acc[:, _C_MV:_C_MO].astype(BF16)
    mo_ref[...] = acc[:, _C_MO:_C_G].astype(BF16)


def _even_in(x, g, w, bias, f0, *, tm, tiles_per_batch):
    n = x.shape[0]
    row = lambda i: (i, 0)
    fixed = lambda i: (0, 0)
    widths = (2 * D_MODEL, 2 * D_MODEL, FOX_W, FOX_W, FOX_W, MLSTM_W, MLSTM_W, MLSTM_W, MLSTM_W, LANES)
    dtypes = (BF16, BF16, F32, F32, BF16, BF16, BF16, BF16, BF16, F32)
    return pl.pallas_call(
        functools.partial(_even_in_kernel, tiles_per_batch=tiles_per_batch),
        grid=(n // tm,),
        in_specs=[pl.BlockSpec((tm, D_MODEL), row), pl.BlockSpec((1, D_MODEL), fixed),
                  pl.BlockSpec((D_MODEL, _C_END), fixed), pl.BlockSpec((1, LANES), fixed),
                  pl.BlockSpec((None, 1, LANES), lambda i: (i // tiles_per_batch, 0, 0))],
        out_specs=[pl.BlockSpec((tm, wd), row) for wd in widths],
        out_shape=[jax.ShapeDtypeStruct((n, wd), dt) for wd, dt in zip(widths, dtypes)],
        scratch_shapes=[pltpu.VMEM((1, LANES), F32)],
        compiler_params=_cp(("arbitrary",), 48),
        name="even_in",
    )(x, g, w, bias, f0)


def _cache_prep_kernel(k_ref, lf_ref, ka_ref, fend_ref, carry_ref):
    @pl.when(pl.program_id(1) == 0)
    def _():
        carry_ref[...] = jnp.zeros_like(carry_ref)

    k_all = k_ref[...]
    tm = k_all.shape[0]
    lane = lax.broadcasted_iota(jnp.int32, (tm, LANES), 1)
    cum_f = _cumsum_rows(lf_ref[...]) + carry_ref[...]
    carry_ref[...] = cum_f[tm - 1:tm, :]
    fend_ref[...] = cum_f[tm - 1:tm, :]
    for h in range(FOX_HEADS):
        ka_ref[:, LANES * h:LANES * (h + 1)] = _fox_aug(k_all, cum_f, h, lane, False)


def _cache_prep(cache_k, cache_lf, *, tm):
    b, p, _ = cache_k.shape
    return pl.pallas_call(
        _cache_prep_kernel,
        grid=(b, p // tm),
        in_specs=[pl.BlockSpec((None, tm, FOX_W), lambda i, j: (i, j, 0)),
                  pl.BlockSpec((None, tm, LANES), lambda i, j: (i, j, 0))],
        out_specs=[pl.BlockSpec((None, tm, 2 * D_MODEL), lambda i, j: (i, j, 0)),
                   pl.BlockSpec((None, 1, LANES), lambda i, j: (i, 0, 0))],
        out_shape=[jax.ShapeDtypeStruct((b, p, 2 * D_MODEL), BF16), jax.ShapeDtypeStruct((b, 1, LANES), F32)],
        scratch_shapes=[pltpu.VMEM((1, LANES), F32)],
        compiler_params=_cp(("arbitrary", "arbitrary"), 32),
        name="fox_cache_prep",
    )(cache_k, cache_lf)


def _fox_kernel(q_ref, k_ref, v_ref, o_ref, *, q0):
    t = FOX_TILE
    n_full = q0 + pl.program_id(2)
    q = q_ref[...]
    qs = (q[:, 0:LANES], q[:, LANES:2 * LANES])
    visible = (lax.broadcasted_iota(jnp.int32, (t, t), 0) <= lax.broadcasted_iota(jnp.int32, (t, t), 1))

    def step(j, carry, diagonal):
        off = pl.multiple_of(j * t, t)
        k = k_ref[pl.ds(off, t), :]
        v = v_ref[pl.ds(off, t), :]
        out = []
        for hh in range(2):
            m, l, acc = carry[3 * hh:3 * hh + 3]
            s = lax.dot_general(k[:, LANES * hh:LANES * (hh + 1)], qs[hh], (((1,), (1,)), ((), ())),
                                preferred_element_type=F32)
            if diagonal:
                s = jnp.where(visible, s, NEG)
            m_new = jnp.maximum(m, jnp.max(s, axis=0, keepdims=True))
            alpha = jnp.exp(m - m_new)
            p = jnp.exp(s - m_new)
            l = alpha * l + jnp.sum(p, axis=0, keepdims=True)
            pv = lax.dot_general(v, p.astype(BF16), (((0,), (0,)), ((), ())),
                                 preferred_element_type=F32)
            acc = alpha * acc + pv[FOX_DH * hh:FOX_DH * (hh + 1), :]
            out += [m_new, l, acc]
        return tuple(out)

    init = (jnp.full((1, t), NEG, F32), jnp.zeros((1, t), F32), jnp.zeros((FOX_DH, t), F32)) * 2
    carry = lax.fori_loop(0, n_full, lambda j, c: step(j, c, False), init)
    carry = step(n_full, carry, True)
    for hh in range(2):
        _, l, acc = carry[3 * hh:3 * hh + 3]
        o_ref[FOX_DH * hh:FOX_DH * (hh + 1), :] = (acc / l).astype(BF16)


def _fox_attention(q_aug, k_aug, v, *, q0):
    b, tq_total, _ = q_aug.shape
    t_kv = k_aug.shape[1]
    nq = tq_total // FOX_TILE
    return pl.pallas_call(
        functools.partial(_fox_kernel, q0=q0),
        grid=(b, FOX_HEADS // 2, nq),
        in_specs=[pl.BlockSpec((None, FOX_TILE, 2 * LANES), lambda bi, p, i: (bi, i, p)),
                  pl.BlockSpec((None, t_kv, 2 * LANES), lambda bi, p, i: (bi, 0, p)),
                  pl.BlockSpec((None, t_kv, LANES), lambda bi, p, i: (bi, 0, p))],
        out_specs=pl.BlockSpec((None, LANES, FOX_TILE), lambda bi, p, i: (bi, p, i)),
        out_shape=jax.ShapeDtypeStruct((b, FOX_W, tq_total), BF16),
        compiler_params=_cp(("arbitrary", "arbitrary", "arbitrary"), 40),
        name="fox_attention",
    )(q_aug, k_aug, v)


def _mlstm_kernel(mq_ref, mk_ref, mv_ref, mo_ref, gt_ref, c0_ref, m0_ref, ng_ref,
                  hm_ref, co_ref, mout_ref, c_sc, m_sc, gt_t, cs_t, *, chunk, chunks_per_step):
    step = pl.program_id(1)

    @pl.when(step == 0)
    def _():
        c_sc[...] = c0_ref[...]
        m_sc[...] = m0_ref[...]

    ln = chunk
    causal = lax.broadcasted_iota(jnp.int32, (ln, ln), 0) >= lax.broadcasted_iota(jnp.int32, (ln, ln), 1)
    lane = lax.broadcasted_iota(jnp.int32, (ln, LANES), 1)
    pad = jnp.zeros((LANES - ln, LANES), F32) if ln < LANES else None

    for c in range(chunks_per_step):
        rows = slice(c * ln, (c + 1) * ln)
        g = gt_ref[rows, :]
        cs = _cumsum_rows(g)
        gt_t[...] = (jnp.concatenate([g, pad], axis=0) if pad is not None else g).T
        cs_t[...] = (jnp.concatenate([cs, pad], axis=0) if pad is not None else cs).T
        for h in range(MLSTM_HEADS):
            hl = slice(LANES * h, LANES * (h + 1))
            ig_col = g[:, _G_IG + h:_G_IG + h + 1]
            b_col = cs[:, _G_LF + h:_G_LF + h + 1]
            ig_row = gt_t[_G_IG + h:_G_IG + h + 1, 0:ln]
            b_row = cs_t[_G_LF + h:_G_LF + h + 1, 0:ln]
            m_prev = m_sc[h:h + 1, 0:1]
            q = mq_ref[rows, hl]
            k = mk_ref[rows, hl]
            v = mv_ref[rows, hl]

            d = jnp.where(causal, b_col - b_row + ig_row, -jnp.inf)
            inter = b_col + m_prev
            m_t = jnp.maximum(inter, jnp.max(d, axis=1, keepdims=True))
            w_intra = jnp.exp(d - m_t)
            w_inter = jnp.exp(inter - m_t)
            a = lax.dot_general(q, k, (((1,), (1,)), ((), ())), preferred_element_type=F32) * w_intra
            cq = lax.dot_general(q, c_sc[h].astype(BF16), (((1,), (1,)), ((), ())),
                                 preferred_element_type=F32)
            num = w_inter * cq[:, 0:MLSTM_DV] + jnp.dot(a.astype(BF16), v, preferred_element_type=F32)
            den = w_inter * cq[:, MLSTM_DV:MLSTM_DV + 1] + jnp.sum(a, axis=1, keepdims=True)
            h_cell = num / jnp.maximum(jnp.abs(den), jnp.exp(-m_t))

            b_last = b_col[ln - 1:ln, :]
            g_col = b_last - b_col + ig_col
            m_new = jnp.maximum(b_last + m_prev, jnp.max(g_col, axis=0, keepdims=True))
            w_state = jnp.exp(g_col - m_new)
            decay = jnp.exp(b_last + m_prev - m_new)
            vw = jnp.concatenate([v.astype(F32) * w_state, jnp.where(lane == 0, w_state, 0.0)], axis=1)
            upd = lax.dot_general(vw.astype(BF16), k, (((0,), (0,)), ((), ())),
                                  preferred_element_type=F32)
            c_sc[h] = decay * c_sc[h] + upd
            m_sc[h:h + 1, :] = jnp.broadcast_to(m_new, (1, LANES))

            hn = h_cell * lax.rsqrt(jnp.mean(h_cell * h_cell, axis=1, keepdims=True) + EPS)
            hm_ref[rows, hl] = (hn * ng_ref[:, hl] * _sigmoid(mo_ref[rows, hl].astype(F32))).astype(BF16)

    @pl.when(step == pl.num_programs(1) - 1)
    def _():
        co_ref[...] = c_sc[...]
        mout_ref[...] = m_sc[...]


def _mlstm(mq, mk, mv, mo, gt, c0, m0, ng, *, chunk, chunks_per_step):
    b, t, _ = mq.shape
    rows = chunk * chunks_per_step
    tok = lambda wd: pl.BlockSpec((None, rows, wd), lambda i, j: (i, j, 0))
    c_spec = pl.BlockSpec((None, MLSTM_HEADS, MLSTM_EXT, LANES), lambda i, j: (i, 0, 0, 0))
    m_spec = pl.BlockSpec((None, SUBLANES, LANES), lambda i, j: (i, 0, 0))
    return pl.pallas_call(
        functools.partial(_mlstm_kernel, chunk=chunk, chunks_per_step=chunks_per_step),
        grid=(b, t // rows),
        in_specs=[tok(MLSTM_W), tok(MLSTM_W), tok(MLSTM_W), tok(MLSTM_W), tok(LANES), c_spec, m_spec,
                  pl.BlockSpec((1, MLSTM_W), lambda i, j: (0, 0))],
        out_specs=[tok(MLSTM_W), c_spec, m_spec],
        out_shape=[jax.ShapeDtypeStruct((b, t, MLSTM_W), BF16),
                   jax.ShapeDtypeStruct((b, MLSTM_HEADS, MLSTM_EXT, LANES), F32),
                   jax.ShapeDtypeStruct((b, SUBLANES, LANES), F32)],
        scratch_shapes=[pltpu.VMEM((MLSTM_HEADS, MLSTM_EXT, LANES), F32), pltpu.VMEM((SUBLANES, LANES), F32),
                        pltpu.VMEM((LANES, LANES), F32), pltpu.VMEM((LANES, LANES), F32)],
        compiler_params=_cp(("arbitrary", "arbitrary"), 32),
        name="mlstm",
    )(mq, mk, mv, mo, gt, c0, m0, ng)


def _even_out_kernel(x_ref, fox_ref, hm_ref, w_ref, o_ref, *, fox_transposed):
    dims = (((0,), (0,)), ((), ())) if fox_transposed else (((1,), (0,)), ((), ()))
    y = lax.dot_general(fox_ref[...], w_ref[0:FOX_W, :], dims, preferred_element_type=F32)
    y = y + jnp.dot(hm_ref[...], w_ref[FOX_W:, :], preferred_element_type=F32)
    o_ref[...] = x_ref[...] + y


def _even_out(x, fox, hm, w, *, tm, fox_transposed):
    b, t, _ = x.shape
    fox_spec = (pl.BlockSpec((None, FOX_W, tm), lambda i, j: (i, 0, j)) if fox_transposed
                else pl.BlockSpec((None, tm, FOX_W), lambda i, j: (i, j, 0)))
    return pl.pallas_call(
        functools.partial(_even_out_kernel, fox_transposed=fox_transposed),
        grid=(b, t // tm),
        in_specs=[pl.BlockSpec((None, tm, D_MODEL), lambda i, j: (i, j, 0)), fox_spec,
                  pl.BlockSpec((None, tm, MLSTM_W), lambda i, j: (i, j, 0)),
                  pl.BlockSpec((D_MODEL, D_MODEL), lambda i, j: (0, 0))],
        out_specs=pl.BlockSpec((None, tm, D_MODEL), lambda i, j: (i, j, 0)),
        out_shape=jax.ShapeDtypeStruct((b, t, D_MODEL), F32),
        compiler_params=_cp(("arbitrary", "arbitrary"), 32),
        name="even_out",
    )(x, fox, hm, w)


def _cross_kernel(h_ref, g_ref, wq_ref, wo_ref, mk_ref, mv_ref, o_ref):
    x = h_ref[...]
    xn = _rms(x, g_ref[...]).astype(BF16)
    q = jnp.dot(xn, wq_ref[...], preferred_element_type=F32).astype(BF16)
    outs = []
    for h in range(MEM_HEADS):
        hl = slice(MEM_DH * h, MEM_DH * (h + 1))
        s = lax.dot_general(q[:, hl], mk_ref[:, hl], (((1,), (1,)), ((), ())),
                            preferred_element_type=F32) * MEM_SCALE
        e = jnp.exp(s - jnp.max(s, axis=1, keepdims=True))
        p = e * (1.0 / jnp.sum(e, axis=1, keepdims=True))
        outs.append(jnp.dot(p.astype(BF16), mv_ref[:, hl], preferred_element_type=F32).astype(BF16))
    o = jnp.concatenate(outs, axis=1)
    o_ref[...] = x + jnp.dot(o, wo_ref[...], preferred_element_type=F32)


def _cross(h, g, wq, wo, mem_k, mem_v, *, tm):
    b, t, _ = h.shape
    tok = pl.BlockSpec((None, tm, D_MODEL), lambda i, j: (i, j, 0))
    mem = pl.BlockSpec((None, MEM_TOKENS, MEM_W), lambda i, j: (i, 0, 0))
    return pl.pallas_call(
        _cross_kernel,
        grid=(b, t // tm),
        in_specs=[tok, pl.BlockSpec((1, D_MODEL), lambda i, j: (0, 0)),
                  pl.BlockSpec((D_MODEL, MEM_W), lambda i, j: (0, 0)),
                  pl.BlockSpec((MEM_W, D_MODEL), lambda i, j: (0, 0)), mem, mem],
        out_specs=tok,
        out_shape=jax.ShapeDtypeStruct((b, t, D_MODEL), F32),
        compiler_params=_cp(("arbitrary", "arbitrary"), 32),
        name="cross_attention",
    )(h, g, wq, wo, mem_k, mem_v)


def _memkv_kernel(mem_ref, g_ref, wk_ref, wv_ref, k_ref, v_ref, kb_ref, vb_ref):
    mn = _rms(mem_ref[...], g_ref[...]).astype(BF16)
    k = jnp.dot(mn, wk_ref[...], preferred_element_type=F32)
    v = jnp.dot(mn, wv_ref[...], preferred_element_type=F32)
    k_ref[...] = k
    v_ref[...] = v
    kb_ref[...] = k.astype(BF16)
    vb_ref[...] = v.astype(BF16)


def _memkv(mem, g, wk, wv, *, tm):
    n = mem.shape[0]
    depth = g.shape[0]
    w_spec = pl.BlockSpec((None, D_MODEL, MEM_W), lambda l, i: (l, 0, 0))
    o_spec = pl.BlockSpec((None, tm, MEM_W), lambda l, i: (l, i, 0))
    return pl.pallas_call(
        _memkv_kernel,
        grid=(depth, n // tm),
        in_specs=[pl.BlockSpec((tm, D_MODEL), lambda l, i: (i, 0)),
                  pl.BlockSpec((None, 1, D_MODEL), lambda l, i: (l, 0, 0)), w_spec, w_spec],
        out_specs=[o_spec] * 4,
        out_shape=[jax.ShapeDtypeStruct((depth, n, MEM_W), dt) for dt in (F32, F32, BF16, BF16)],
        compiler_params=_cp(("arbitrary", "arbitrary"), 32),
        name="memory_kv",
    )(mem, g, wk, wv)


def _silu(x):
    return x * _sigmoid(x)


def _ffn_kernel(h_ref, g_ref, wg_ref, wu_ref, wd_ref, o_ref, xn_sc):
    @pl.when(pl.program_id(1) == 0)
    def _():
        x = h_ref[...]
        xn_sc[...] = _rms(x, g_ref[...]).astype(BF16)
        o_ref[...] = x

    xn = xn_sc[...]
    a = _silu(jnp.dot(xn, wg_ref[...], preferred_element_type=F32)) * jnp.dot(xn, wu_ref[...], preferred_element_type=F32)
    o_ref[...] += jnp.dot(a.astype(BF16), wd_ref[...], preferred_element_type=F32)


def _ffn(h, g, wg, wu, wd, *, tm):
    n = h.shape[0]
    tok = pl.BlockSpec((tm, D_MODEL), lambda i, j: (i, 0))
    return pl.pallas_call(
        _ffn_kernel,
        grid=(n // tm, D_FF // FF_TILE),
        in_specs=[tok, pl.BlockSpec((1, D_MODEL), lambda i, j: (0, 0)),
                  pl.BlockSpec((D_MODEL, FF_TILE), lambda i, j: (0, j)),
                  pl.BlockSpec((D_MODEL, FF_TILE), lambda i, j: (0, j)),
                  pl.BlockSpec((FF_TILE, D_MODEL), lambda i, j: (j, 0))],
        out_specs=tok,
        out_shape=jax.ShapeDtypeStruct((n, D_MODEL), F32),
        scratch_shapes=[pltpu.VMEM((tm, D_MODEL), BF16)],
        compiler_params=_cp(("arbitrary", "arbitrary"), 48),
        name="dense_swiglu",
    )(h, g, wg, wu, wd)


def _odd_kernel(h_ref, g_ref, win_ref, cw_ref, wout_ref, past_ref, o_ref, st_ref, u_sc):
    j = pl.program_id(1)
    tm = h_ref.shape[0]

    @pl.when(j == 0)
    def _():
        u_sc[0:SUBLANES, :] = past_ref[...]

    @pl.when(j > 0)
    def _():
        u_sc[0:SUBLANES, :] = u_sc[tm:tm + SUBLANES, :]

    x = h_ref[...]
    xn = _rms(x, g_ref[...]).astype(BF16)
    z = jnp.dot(xn, win_ref[...], preferred_element_type=F32)
    gate_b = z[:, 0:D_MODEL]
    u = z[:, D_MODEL:2 * D_MODEL] * z[:, 2 * D_MODEL:]
    u_sc[SUBLANES:, :] = u
    conv = (cw_ref[0:1, :] * u_sc[SUBLANES - 2:SUBLANES - 2 + tm, :]
            + cw_ref[1:2, :] * u_sc[SUBLANES - 1:SUBLANES - 1 + tm, :]
            + cw_ref[2:3, :] * u)
    o_ref[...] = x + jnp.dot((gate_b * conv).astype(BF16), wout_ref[...], preferred_element_type=F32)

    @pl.when(j == pl.num_programs(1) - 1)
    def _():
        st_ref[...] = u_sc[tm:tm + SUBLANES, :]


def _odd(h, g, w_in, cw, w_out, past, *, tm):
    b, t, _ = h.shape
    tok = pl.BlockSpec((None, tm, D_MODEL), lambda i, j: (i, j, 0))
    st = pl.BlockSpec((None, SUBLANES, D_MODEL), lambda i, j: (i, 0, 0))
    return pl.pallas_call(
        _odd_kernel,
        grid=(b, t // tm),
        in_specs=[tok, pl.BlockSpec((1, D_MODEL), lambda i, j: (0, 0)),
                  pl.BlockSpec((D_MODEL, 3 * D_MODEL), lambda i, j: (0, 0)),
                  pl.BlockSpec((SUBLANES, D_MODEL), lambda i, j: (0, 0)),
                  pl.BlockSpec((D_MODEL, D_MODEL), lambda i, j: (0, 0)), st],
        out_specs=[tok, st],
        out_shape=[jax.ShapeDtypeStruct((b, t, D_MODEL), F32), jax.ShapeDtypeStruct((b, SUBLANES, D_MODEL), F32)],
        scratch_shapes=[pltpu.VMEM((tm + SUBLANES, D_MODEL), F32)],
        compiler_params=_cp(("arbitrary", "arbitrary"), 48),
        name="short_conv_mixer",
    )(h, g, w_in, cw, w_out, past)


def _route(logits, lane_f):
    lg = jnp.where(lane_f < N_EXPERTS, logits, -jnp.inf)
    m1 = jnp.max(lg, axis=1, keepdims=True)
    i1 = jnp.min(jnp.where(lg == m1, lane_f, float(LANES)), axis=1, keepdims=True)
    lg2 = jnp.where(lane_f == i1, -jnp.inf, lg)
    m2 = jnp.max(lg2, axis=1, keepdims=True)
    i2 = jnp.min(jnp.where(lg2 == m2, lane_f, float(LANES)), axis=1, keepdims=True)
    e2 = jnp.exp(m2 - m1)
    inv = 1.0 / (1.0 + e2)
    return jnp.where(lane_f == i1, inv, 0.0) + jnp.where(lane_f == i2, e2 * inv, 0.0)


def _moe_kernel(h_ref, g_ref, rw_ref, wg_ref, wu_ref, wd_ref, fg_ref, o_ref, xn_sc, comb_sc):
    e = pl.program_id(1)
    j = pl.program_id(2)
    tm = h_ref.shape[0]
    lane = lax.broadcasted_iota(jnp.int32, (tm, LANES), 1)

    @pl.when((e == 0) & (j == 0))
    def _():
        x = h_ref[...]
        xn = _rms(x, g_ref[...]).astype(BF16)
        xn_sc[...] = xn
        logits = jnp.dot(xn, rw_ref[...], preferred_element_type=F32)
        comb_sc[...] = _route(logits, lane.astype(F32))
        o_ref[...] = x

    xn = xn_sc[...]
    a = _silu(jnp.dot(xn, wg_ref[...], preferred_element_type=F32)) * jnp.dot(xn, wu_ref[...], preferred_element_type=F32)
    y = jnp.dot(a.astype(BF16), wd_ref[...], preferred_element_type=F32)
    w_e = jnp.sum(jnp.where(lane == e, comb_sc[...], 0.0), axis=1, keepdims=True)
    o_ref[...] += y * w_e

    @pl.when((e == pl.num_programs(1) - 1) & (j == pl.num_programs(2) - 1))
    def _():
        o_ref[...] = _rms(o_ref[...], fg_ref[...])


def _moe(h, g, rw, wg, wu, wd, fg, *, tm):
    n = h.shape[0]
    tok = pl.BlockSpec((tm, D_MODEL), lambda i, e, j: (i, 0))
    vec = pl.BlockSpec((1, D_MODEL), lambda i, e, j: (0, 0))
    return pl.pallas_call(
        _moe_kernel,
        grid=(n // tm, N_EXPERTS, D_FF // FF_TILE),
        in_specs=[tok, vec, pl.BlockSpec((D_MODEL, LANES), lambda i, e, j: (0, 0)),
                  pl.BlockSpec((None, D_MODEL, FF_TILE), lambda i, e, j: (e, 0, j)),
                  pl.BlockSpec((None, D_MODEL, FF_TILE), lambda i, e, j: (e, 0, j)),
                  pl.BlockSpec((None, FF_TILE, D_MODEL), lambda i, e, j: (e, j, 0)), vec],
        out_specs=tok,
        out_shape=jax.ShapeDtypeStruct((n, D_MODEL), F32),
        scratch_shapes=[pltpu.VMEM((tm, D_MODEL), BF16), pltpu.VMEM((tm, LANES), F32)],
        compiler_params=_cp(("arbitrary", "arbitrary", "arbitrary"), 48),
        name="moe_swiglu",
    )(h, g, rw, wg, wu, wd, fg)


def _pack_params(p):
    fq, fk, fv, ff, mq, mk, mv, mo, mi, mf = jnp.split(p['even_w_in'][0], list(EVEN_SPLITS), axis=1)
    gate_w = jnp.concatenate([ff, mi, mf, jnp.zeros((D_MODEL, LANES - _G_END), F32)], axis=1)
    gate_b = jnp.concatenate([p['fox_b_f'][0], p['mlstm_b_i'][0], p['mlstm_b_f'][0], jnp.zeros((LANES - _G_END,), F32)])
    row = lambda a: a.reshape(1, -1).astype(F32)
    return dict(
        even_w=jnp.concatenate([fq, fk, fv, mq, mk, mv, mo, gate_w], axis=1).astype(BF16),
        even_b=gate_b.reshape(1, LANES),
        even_w_out=p['even_w_out'][0].astype(BF16),
        mlstm_norm_g=row(p['mlstm_norm_g'][0]),
        norm_mix_g=[row(p['norm_mix_g'][l]) for l in range(2)],
        norm_cross_g=[row(p['norm_cross_g'][l]) for l in range(2)],
        norm_ffn_g=[row(p['norm_ffn_g'][l]) for l in range(2)],
        final_norm_g=row(p['final_norm_g']),
        mem_wq=p['mem_wq'].astype(BF16), mem_wo=p['mem_wo'].astype(BF16),
        ffn_wg=p['ffn_w_gate'][0].astype(BF16), ffn_wu=p['ffn_w_up'][0].astype(BF16), ffn_wd=p['ffn_w_down'][0].astype(BF16),
        odd_w_in=p['odd_w_in'][0].astype(BF16), odd_w_out=p['odd_w_out'][0].astype(BF16),
        conv_w=jnp.concatenate([p['conv_w'][0], jnp.zeros((SUBLANES - CONV_W, D_MODEL), F32)], axis=0),
        router_w=jnp.concatenate([p['router_w'][0], jnp.zeros((D_MODEL, LANES - N_EXPERTS), F32)], axis=1).astype(BF16),
        moe_wg=p['moe_w_gate'][0].astype(BF16), moe_wu=p['moe_w_up'][0].astype(BF16), moe_wd=p['moe_w_down'][0].astype(BF16),
    )


def _trunk(x, mem_k, mem_v, caches, w, *, tm, tm_wide, chunk, chunks_per_step):
    b, t, _ = x.shape
    n = b * t
    flat = lambda a: a.reshape(n, a.shape[-1])
    per_b = lambda a: a.reshape(b, t, a.shape[-1])

    if caches is None:
        f0 = jnp.zeros((b, 1, LANES), F32)
    else:
        cache_k, cache_v, cache_lf, c0, n0, m0, conv_st = caches
        past_len = cache_k.shape[1]
        lf_pad = jnp.pad(cache_lf.astype(F32), ((0, 0), (0, 0), (0, LANES - FOX_HEADS)))
        ka_cache, f0 = _cache_prep(cache_k.reshape(b, past_len, FOX_W).astype(F32), lf_pad, tm=FOX_TILE)
    qa, ka, fk, fv, vb, mq, mk, mv, mo, gt = _even_in(flat(x), w['norm_mix_g'][0], w['even_w'], w['even_b'], f0,
                                                      tm=tm, tiles_per_batch=t // tm)
    if caches is None:
        fox_t = _fox_attention(per_b(qa), per_b(ka), per_b(vb), q0=0)
        c_ext0 = jnp.zeros((b, MLSTM_HEADS, MLSTM_EXT, LANES), F32)
        m_ext0 = jnp.zeros((b, SUBLANES, LANES), F32)
        past = jnp.zeros((b, SUBLANES, D_MODEL), F32)
    else:
        pad_t = lambda a: jnp.pad(per_b(a), ((0, 0), (0, FOX_TILE - t), (0, 0)))
        k_all = jnp.concatenate([ka_cache, pad_t(ka)], axis=1)
        v_all = jnp.concatenate([cache_v.reshape(b, past_len, FOX_W).astype(BF16), pad_t(vb)], axis=1)
        fox_full = _fox_attention(pad_t(qa), k_all, v_all, q0=past_len // FOX_TILE)
        fox_t = jnp.swapaxes(fox_full[:, :, :t], 1, 2)
        c_ext0 = jnp.concatenate([
            jnp.pad(c0.astype(F32), ((0, 0), (0, 0), (0, 0), (0, LANES - MLSTM_DK))),
            jnp.pad(n0.astype(F32)[:, :, None, :], ((0, 0), (0, 0), (0, MLSTM_EXT - MLSTM_DV - 1), (0, LANES - MLSTM_DK)))],
            axis=2)
        m_ext0 = jnp.broadcast_to(jnp.pad(m0.astype(F32), ((0, 0), (0, SUBLANES - MLSTM_HEADS)))[:, :, None],
                                  (b, SUBLANES, LANES))
        past = jnp.pad(conv_st.astype(F32), ((0, 0), (SUBLANES - (CONV_W - 1), 0), (0, 0)))
    hm, c_ext, m_ext = _mlstm(per_b(mq), per_b(mk), per_b(mv), per_b(mo), per_b(gt), c_ext0, m_ext0,
                              w['mlstm_norm_g'], chunk=chunk, chunks_per_step=chunks_per_step)
    h = _even_out(x, fox_t, hm, w['even_w_out'], tm=tm, fox_transposed=caches is None)
    h = _cross(h, w['norm_cross_g'][0], w['mem_wq'][0], w['mem_wo'][0], mem_k[0], mem_v[0], tm=tm)
    h = _ffn(flat(h), w['norm_ffn_g'][0], w['ffn_wg'], w['ffn_wu'], w['ffn_wd'], tm=tm_wide)

    h, conv_new = _odd(per_b(h), w['norm_mix_g'][1], w['odd_w_in'], w['conv_w'], w['odd_w_out'], past, tm=tm)
    h = _cross(h, w['norm_cross_g'][1], w['mem_wq'][1], w['mem_wo'][1], mem_k[1], mem_v[1], tm=tm)
    y = _moe(flat(h), w['norm_ffn_g'][1], w['router_w'], w['moe_wg'], w['moe_wu'], w['moe_wd'], w['final_norm_g'],
             tm=tm_wide)

    states = (
        fk.reshape(1, b, t, FOX_HEADS, FOX_DH), fv.reshape(1, b, t, FOX_HEADS, FOX_DH),
        per_b(gt)[None, :, :, _G_FOX:_G_IG],
        c_ext[None, :, :, 0:MLSTM_DV, 0:MLSTM_DK], c_ext[None, :, :, MLSTM_DV, 0:MLSTM_DK], m_ext[None, :, 0:MLSTM_HEADS, 0],
        conv_new[None, :, SUBLANES - (CONV_W - 1):, :],
    )
    return per_b(y), states


def kernel(x_prompt, x_sample, mem_prompt, cache_fox_k, cache_fox_v, cache_fox_logf, state_mlstm_c, state_mlstm_n, state_mlstm_m, state_conv, cache_mem_k, cache_mem_v, norm_mix_g, norm_mem_g, norm_cross_g, norm_ffn_g, final_norm_g, even_w_in, fox_b_f, mlstm_b_i, mlstm_b_f, mlstm_norm_g, even_w_out, odd_w_in, conv_w, odd_w_out, mem_wq, mem_wk, mem_wv, mem_wo, ffn_w_gate, ffn_w_up, ffn_w_down, router_w, moe_w_gate, moe_w_up, moe_w_down):
    w = _pack_params(dict(
        norm_mix_g=norm_mix_g, norm_cross_g=norm_cross_g, norm_ffn_g=norm_ffn_g, final_norm_g=final_norm_g,
        even_w_in=even_w_in, fox_b_f=fox_b_f, mlstm_b_i=mlstm_b_i, mlstm_b_f=mlstm_b_f, mlstm_norm_g=mlstm_norm_g,
        even_w_out=even_w_out, odd_w_in=odd_w_in, conv_w=conv_w, odd_w_out=odd_w_out, mem_wq=mem_wq, mem_wo=mem_wo,
        ffn_w_gate=ffn_w_gate, ffn_w_up=ffn_w_up, ffn_w_down=ffn_w_down, router_w=router_w,
        moe_w_gate=moe_w_gate, moe_w_up=moe_w_up, moe_w_down=moe_w_down))

    bp, tp, _ = x_prompt.shape
    bs, ts, _ = x_sample.shape
    depth = norm_mem_g.shape[0]

    mem_k_p, mem_v_p, mem_kb, mem_vb = _memkv(mem_prompt.reshape(bp * MEM_TOKENS, D_MODEL),
                                              norm_mem_g.reshape(depth, 1, D_MODEL).astype(F32),
                                              mem_wk.astype(BF16), mem_wv.astype(BF16), tm=512)
    per_layer = lambda a, nb: a.reshape(depth, nb, MEM_TOKENS, MEM_W)
    y_prompt, st_p = _trunk(x_prompt, per_layer(mem_kb, bp), per_layer(mem_vb, bp), None, w,
                            tm=256, tm_wide=1024, chunk=MLSTM_CHUNK, chunks_per_step=4)

    caches = (cache_fox_k[0], cache_fox_v[0], cache_fox_logf[0], state_mlstm_c[0], state_mlstm_n[0],
              state_mlstm_m[0], state_conv[0])
    y_sample, st_s = _trunk(x_sample, per_layer(cache_mem_k.astype(BF16), bs), per_layer(cache_mem_v.astype(BF16), bs),
                            caches, w, tm=ts, tm_wide=bs * ts, chunk=ts, chunks_per_step=1)

    mem_shape = (depth, bp, MEM_TOKENS, MEM_HEADS, MEM_DH)
    return (y_prompt, y_sample) + st_p + (mem_k_p.reshape(mem_shape), mem_v_p.reshape(mem_shape)) + st_s
```

```python
import functools

import jax
import jax.numpy as jnp
from jax import lax
from jax.experimental import pallas as pl
from jax.experimental.pallas import tpu as pltpu

F32 = jnp.float32
BF16 = jnp.bfloat16

D_MODEL = 1024
EPS = 1e-6
FOX_HEADS = 8
FOX_DH = 64
FOX_W = FOX_HEADS * FOX_DH
FOX_SCALE = FOX_DH ** -0.5
MLSTM_HEADS = 4
MLSTM_DV = 128
MLSTM_DK = 64
MLSTM_W = MLSTM_HEADS * MLSTM_DV
MLSTM_QK_W = MLSTM_HEADS * MLSTM_DK
MLSTM_SCALE = MLSTM_DK ** -0.5
MLSTM_CHUNK = 256
MLSTM_EXT = 256
MEM_TOKENS = 256
MEM_HEADS = 4
MEM_DH = 128
MEM_W = MEM_HEADS * MEM_DH
MEM_SCALE = MEM_DH ** -0.5
D_FF = 3584
N_EXPERTS = 8
CONV_W = 3
EVEN_SIZES = (FOX_W, FOX_W, FOX_W, FOX_HEADS, MLSTM_QK_W, MLSTM_QK_W, MLSTM_W, MLSTM_W, MLSTM_HEADS, MLSTM_HEADS)
EVEN_SPLITS = tuple(sum(EVEN_SIZES[:i + 1]) for i in range(len(EVEN_SIZES) - 1))

LANES = 128
SUBLANES = 8
FOX_AUG_W = FOX_HEADS * LANES
FOX_TILE = 256
FOX_LOOKAHEAD = 3
FOX_PV_DELAY = 1
FF_TILE = 512
_C_Q, _C_K, _C_V, _C_MQ, _C_MK, _C_MV, _C_MO, _C_G, _C_END = 0, 512, 1024, 1536, 1792, 2048, 2560, 3072, 3200
_G_FOX, _G_IG, _G_LF, _G_END = 0, 8, 12, 16
NEG = -1e30
LOG2E = 1.4426950408889634


def _cp(sem, vmem_mb):
    return pltpu.CompilerParams(dimension_semantics=sem, vmem_limit_bytes=vmem_mb * 1024 * 1024)


def _rms(x, g):
    return x * lax.rsqrt(jnp.mean(x * x, axis=-1, keepdims=True) + EPS) * g


def _log_sigmoid(x):
    return jnp.minimum(x, 0.0) - jnp.log1p(jnp.exp(-jnp.abs(x)))


def _sigmoid(x):
    return 1.0 / (1.0 + jnp.exp(-x))


def _cumsum_rows(x):
    n = x.shape[0]
    row = lax.broadcasted_iota(jnp.int32, x.shape, 0)
    s = 1
    while s < n:
        x = x + jnp.where(row >= s, pltpu.roll(x, s, axis=0), 0.0)
        s *= 2
    return x


def _split3(f):
    hi = f.astype(BF16).astype(F32)
    r = f - hi
    mid = r.astype(BF16).astype(F32)
    return hi, mid, r - mid


def _head_block(src, h, lane):
    p, odd = divmod(h, 2)
    blk = src[:, LANES * p:LANES * (p + 1)]
    return pltpu.roll(blk, FOX_DH, axis=1) if odd else blk


def _fox_aug(src, cum_f, h, lane, is_query):
    blk = _head_block(src, h, lane)
    hi, mid, lo = _split3(jnp.broadcast_to(cum_f[:, h:h + 1] * LOG2E, blk.shape))
    if is_query:
        aug = jnp.where(lane == 64, hi, jnp.where(lane == 65, mid, jnp.where(lane == 66, lo,
                        jnp.where(lane < 70, 1.0, 0.0))))
    else:
        aug = jnp.where(lane < 67, 1.0, jnp.where(lane == 67, -hi, jnp.where(lane == 68, -mid,
                        jnp.where(lane == 69, -lo, 0.0))))
    return jnp.where(lane < FOX_DH, blk, aug).astype(BF16)


def _even_in_kernel(x_ref, g_ref, w_ref, b_ref, f0_ref,
                    qa_ref, ka_ref, fk_ref, fv_ref, vb_ref, mq_ref, mk_ref, mv_ref, mo_ref, gt_ref,
                    carry_ref, *, tiles_per_batch):
    @pl.when(pl.program_id(0) % tiles_per_batch == 0)
    def _():
        carry_ref[...] = f0_ref[...]

    xn = _rms(x_ref[...], g_ref[...]).astype(BF16)
    acc = jnp.dot(xn, w_ref[...], preferred_element_type=F32)
    tm = acc.shape[0]
    lane = lax.broadcasted_iota(jnp.int32, (tm, LANES), 1)

    gates = acc[:, _C_G:_C_END] + b_ref[...]
    ls = _log_sigmoid(gates)
    is_ig = (lane >= _G_IG) & (lane < _G_LF)
    gt_ref[...] = jnp.where(is_ig, gates, jnp.where(lane < _G_END, ls, 0.0))
    cum_f = _cumsum_rows(jnp.where(lane < _G_IG, ls, 0.0)) + carry_ref[...]
    carry_ref[...] = cum_f[tm - 1:tm, :]

    q_all = acc[:, _C_Q:_C_K] * (FOX_SCALE * LOG2E)
    k_all = acc[:, _C_K:_C_V]
    v_all = acc[:, _C_V:_C_MQ]
    for h in range(FOX_HEADS):
        qa_ref[:, LANES * h:LANES * (h + 1)] = _fox_aug(q_all, cum_f, h, lane, True)
        ka_ref[:, LANES * h:LANES * (h + 1)] = _fox_aug(k_all, cum_f, h, lane, False)
    fk_ref[...] = k_all
    fv_ref[...] = v_all
    vb_ref[...] = v_all.astype(BF16)

    mq_all = acc[:, _C_MQ:_C_MK] * MLSTM_SCALE
    mk_all = acc[:, _C_MK:_C_MV]
    for h in range(MLSTM_HEADS):
        mq_ref[:, LANES * h:LANES * (h + 1)] = jnp.where(lane < MLSTM_DK, _head_block(mq_all, h, lane), 0.0).astype(BF16)
        mk_ref[:, LANES * h:LANES * (h + 1)] = jnp.where(lane < MLSTM_DK, _head_block(mk_all, h, lane), 0.0).astype(BF16)
    mv_ref[...] = acc[:, _C_MV:_C_MO].astype(BF16)
    mo_ref[...] = acc[:, _C_MO:_C_G].astype(BF16)


def _even_in(x, g, w, bias, f0, *, tm, tiles_per_batch):
    n = x.shape[0]
    row = lambda i: (i, 0)
    fixed = lambda i: (0, 0)
    widths = (FOX_AUG_W, FOX_AUG_W, FOX_W, FOX_W, FOX_W, MLSTM_W, MLSTM_W, MLSTM_W, MLSTM_W, LANES)
    dtypes = (BF16, BF16, F32, F32, BF16, BF16, BF16, BF16, BF16, F32)
    return pl.pallas_call(
        functools.partial(_even_in_kernel, tiles_per_batch=tiles_per_batch),
        grid=(n // tm,),
        in_specs=[pl.BlockSpec((tm, D_MODEL), row), pl.BlockSpec((1, D_MODEL), fixed),
                  pl.BlockSpec((D_MODEL, _C_END), fixed), pl.BlockSpec((1, LANES), fixed),
                  pl.BlockSpec((None, 1, LANES), lambda i: (i // tiles_per_batch, 0, 0))],
        out_specs=[pl.BlockSpec((tm, wd), row) for wd in widths],
        out_shape=[jax.ShapeDtypeStruct((n, wd), dt) for wd, dt in zip(widths, dtypes)],
        scratch_shapes=[pltpu.VMEM((1, LANES), F32)],
        compiler_params=_cp(("arbitrary",), 48),
        name="even_in",
    )(x, g, w, bias, f0)


def _cache_prep_kernel(k_ref, lf_ref, ka_ref, fend_ref, carry_ref):
    @pl.when(pl.program_id(1) == 0)
    def _():
        carry_ref[...] = jnp.zeros_like(carry_ref)

    k_all = k_ref[...]
    tm = k_all.shape[0]
    lane = lax.broadcasted_iota(jnp.int32, (tm, LANES), 1)
    cum_f = _cumsum_rows(lf_ref[...]) + carry_ref[...]
    carry_ref[...] = cum_f[tm - 1:tm, :]
    fend_ref[...] = cum_f[tm - 1:tm, :]
    for h in range(FOX_HEADS):
        ka_ref[:, LANES * h:LANES * (h + 1)] = _fox_aug(k_all, cum_f, h, lane, False)


def _cache_prep(cache_k, cache_lf, *, tm):
    b, p, _ = cache_k.shape
    return pl.pallas_call(
        _cache_prep_kernel,
        grid=(b, p // tm),
        in_specs=[pl.BlockSpec((None, tm, FOX_W), lambda i, j: (i, j, 0)),
                  pl.BlockSpec((None, tm, LANES), lambda i, j: (i, j, 0))],
        out_specs=[pl.BlockSpec((None, tm, FOX_AUG_W), lambda i, j: (i, j, 0)),
                   pl.BlockSpec((None, 1, LANES), lambda i, j: (i, 0, 0))],
        out_shape=[jax.ShapeDtypeStruct((b, p, FOX_AUG_W), BF16), jax.ShapeDtypeStruct((b, 1, LANES), F32)],
        scratch_shapes=[pltpu.VMEM((1, LANES), F32)],
        compiler_params=_cp(("arbitrary", "arbitrary"), 32),
        name="fox_cache_prep",
    )(cache_k, cache_lf)


def _fox_kernel(q_ref, k_ref, v_ref, o_ref, m_sc, l_sc, acc_sc, *, q0):
    t = FOX_TILE
    n_full = q0 + pl.program_id(1)
    visible = (lax.broadcasted_iota(jnp.int32, (t, t), 0) <= lax.broadcasted_iota(jnp.int32, (t, t), 1))
    m_sc[...] = jnp.full_like(m_sc, NEG)
    l_sc[...] = jnp.zeros_like(l_sc)
    acc_sc[...] = jnp.zeros_like(acc_sc)

    def step(j, diagonal):
        off = pl.multiple_of(j * t, t)

        def scores(h):
            hl = slice(LANES * h, LANES * (h + 1))
            return lax.dot_general(k_ref[pl.ds(off, t), hl], q_ref[:, hl], (((1,), (1,)), ((), ())),
                                   preferred_element_type=F32)

        def weighted_values(h, p, alpha):
            pair, hh = divmod(h, 2)
            v_t = v_ref[pl.ds(off, t), LANES * pair:LANES * (pair + 1)].T
            pv = jnp.dot(v_t[FOX_DH * hh:FOX_DH * (hh + 1), :], p, preferred_element_type=F32)
            rows = slice(FOX_DH * h, FOX_DH * (h + 1))
            acc_sc[rows, :] = alpha * acc_sc[rows, :] + pv

        pending = {h: scores(h) for h in range(FOX_LOOKAHEAD)}
        ready = {}
        for h in range(FOX_HEADS):
            s = pending.pop(h)
            if diagonal:
                s = jnp.where(visible, s, NEG)
            m_old = m_sc[h:h + 1, :]
            m_new = jnp.maximum(m_old, jnp.max(s, axis=0, keepdims=True))
            alpha = jnp.exp2(m_old - m_new)
            p = jnp.exp2(s - m_new)
            m_sc[h:h + 1, :] = m_new
            l_sc[h:h + 1, :] = alpha * l_sc[h:h + 1, :] + jnp.sum(p, axis=0, keepdims=True)
            ready[h] = (p.astype(BF16), alpha)
            if h + FOX_LOOKAHEAD < FOX_HEADS:
                pending[h + FOX_LOOKAHEAD] = scores(h + FOX_LOOKAHEAD)
            if h - FOX_PV_DELAY in ready:
                weighted_values(h - FOX_PV_DELAY, *ready.pop(h - FOX_PV_DELAY))
        for h in sorted(ready):
            weighted_values(h, *ready[h])

    def body(j, c):
        step(j, False)
        return c

    lax.fori_loop(0, n_full, body, 0)
    step(n_full, True)
    for h in range(FOX_HEADS):
        rows = slice(FOX_DH * h, FOX_DH * (h + 1))
        o_ref[rows, :] = (acc_sc[rows, :] / l_sc[h:h + 1, :]).astype(BF16)


def _fox_attention(q_aug, k_aug, v, *, q0):
    b, tq_total, _ = q_aug.shape
    t_kv = k_aug.shape[1]
    nq = tq_total // FOX_TILE
    return pl.pallas_call(
        functools.partial(_fox_kernel, q0=q0),
        grid=(b, nq),
        in_specs=[pl.BlockSpec((None, FOX_TILE, FOX_AUG_W), lambda bi, i: (bi, i, 0)),
                  pl.BlockSpec((None, t_kv, FOX_AUG_W), lambda bi, i: (bi, 0, 0)),
                  pl.BlockSpec((None, t_kv, FOX_W), lambda bi, i: (bi, 0, 0))],
        out_specs=pl.BlockSpec((None, FOX_W, FOX_TILE), lambda bi, i: (bi, 0, i)),
        out_shape=jax.ShapeDtypeStruct((b, FOX_W, tq_total), BF16),
        scratch_shapes=[pltpu.VMEM((FOX_HEADS, FOX_TILE), F32), pltpu.VMEM((FOX_HEADS, FOX_TILE), F32),
                        pltpu.VMEM((FOX_W, FOX_TILE), F32)],
        compiler_params=_cp(("arbitrary", "arbitrary"), 48),
        name="fox_attention",
    )(q_aug, k_aug, v)


def _mlstm_kernel(mq_ref, mk_ref, mv_ref, mo_ref, gt_ref, c0_ref, m0_ref, ng_ref,
                  hm_ref, co_ref, mout_ref, c_sc, m_sc, gt_t, cs_t, *, chunk):
    step = pl.program_id(1)

    @pl.when(step == 0)
    def _():
        c_sc[...] = c0_ref[...]
        m_sc[...] = m0_ref[...]

    ln = chunk
    nt = (((1,), (1,)), ((), ()))
    causal = lax.broadcasted_iota(jnp.int32, (ln, ln), 0) >= lax.broadcasted_iota(jnp.int32, (ln, ln), 1)
    lane = lax.broadcasted_iota(jnp.int32, (ln, LANES), 1)
    heads = range(MLSTM_HEADS)
    hl = [slice(LANES * h, LANES * (h + 1)) for h in heads]

    g = gt_ref[...]
    cs = _cumsum_rows(g)
    if ln < LANES:
        pad = jnp.zeros((LANES - ln, LANES), F32)
        gt_t[...] = jnp.concatenate([g, pad], axis=0).T
        cs_t[...] = jnp.concatenate([cs, pad], axis=0).T
    else:
        gt_t[...] = g.T
        cs_t[...] = cs.T
    ig_col = [g[:, _G_IG + h:_G_IG + h + 1] for h in heads]
    b_col = [cs[:, _G_LF + h:_G_LF + h + 1] for h in heads]
    b_last = [b[ln - 1:ln, :] for b in b_col]

    qk = [lax.dot_general(mq_ref[:, hl[h]], mk_ref[:, hl[h]], nt, preferred_element_type=F32) for h in heads]
    m_loc, a_sum, av, g_max, upd = [], [], [], [], []
    for h in heads:
        d = jnp.where(causal, b_col[h] - cs_t[_G_LF + h:_G_LF + h + 1, 0:ln] + gt_t[_G_IG + h:_G_IG + h + 1, 0:ln],
                      -jnp.inf)
        m_loc.append(jnp.max(d, axis=1, keepdims=True))
        a = qk[h] * jnp.exp(d - m_loc[h])
        a_sum.append(jnp.sum(a, axis=1, keepdims=True))
        av.append(jnp.dot(a.astype(BF16), mv_ref[:, hl[h]], preferred_element_type=F32))
    for h in heads:
        g_col = b_last[h] - b_col[h] + ig_col[h]
        g_max.append(jnp.max(g_col, axis=0, keepdims=True))
        w_loc = jnp.exp(g_col - g_max[h])
        vw = jnp.concatenate([mv_ref[:, hl[h]].astype(F32) * w_loc, jnp.where(lane == 0, w_loc, 0.0)], axis=1)
        upd.append(lax.dot_general(vw.astype(BF16), mk_ref[:, hl[h]], (((0,), (0,)), ((), ())),
                                   preferred_element_type=F32))

    for h in heads:
        m_prev = m_sc[h:h + 1, 0:1]
        c_prev = c_sc[h]
        cq = lax.dot_general(mq_ref[:, hl[h]], c_prev.astype(BF16), nt, preferred_element_type=F32)
        inter = b_col[h] + m_prev
        m_t = jnp.maximum(inter, m_loc[h])
        w_inter = jnp.exp(inter - m_t)
        w_intra = jnp.exp(m_loc[h] - m_t)
        num = w_inter * cq[:, 0:MLSTM_DV] + w_intra * av[h]
        den = w_inter * cq[:, MLSTM_DV:MLSTM_DV + 1] + w_intra * a_sum[h]
        h_cell = num / jnp.maximum(jnp.abs(den), jnp.exp(-m_t))

        m_new = jnp.maximum(b_last[h] + m_prev, g_max[h])
        c_sc[h] = jnp.exp(b_last[h] + m_prev - m_new) * c_prev + jnp.exp(g_max[h] - m_new) * upd[h]
        m_sc[h:h + 1, :] = jnp.broadcast_to(m_new, (1, LANES))

        hn = h_cell * lax.rsqrt(jnp.mean(h_cell * h_cell, axis=1, keepdims=True) + EPS)
        hm_ref[:, hl[h]] = (hn * ng_ref[:, hl[h]] * _sigmoid(mo_ref[:, hl[h]].astype(F32))).astype(BF16)

    @pl.when(step == pl.num_programs(1) - 1)
    def _():
        co_ref[...] = c_sc[...]
        mout_ref[...] = m_sc[...]


def _mlstm(mq, mk, mv, mo, gt, c0, m0, ng, *, chunk):
    b, t, _ = mq.shape
    tok = lambda wd: pl.BlockSpec((None, chunk, wd), lambda i, j: (i, j, 0))
    c_spec = pl.BlockSpec((None, MLSTM_HEADS, MLSTM_EXT, LANES), lambda i, j: (i, 0, 0, 0))
    m_spec = pl.BlockSpec((None, SUBLANES, LANES), lambda i, j: (i, 0, 0))
    t_cols = max(chunk, LANES)
    return pl.pallas_call(
        functools.partial(_mlstm_kernel, chunk=chunk),
        grid=(b, t // chunk),
        in_specs=[tok(MLSTM_W), tok(MLSTM_W), tok(MLSTM_W), tok(MLSTM_W), tok(LANES), c_spec, m_spec,
                  pl.BlockSpec((1, MLSTM_W), lambda i, j: (0, 0))],
        out_specs=[tok(MLSTM_W), c_spec, m_spec],
        out_shape=[jax.ShapeDtypeStruct((b, t, MLSTM_W), BF16),
                   jax.ShapeDtypeStruct((b, MLSTM_HEADS, MLSTM_EXT, LANES), F32),
                   jax.ShapeDtypeStruct((b, SUBLANES, LANES), F32)],
        scratch_shapes=[pltpu.VMEM((MLSTM_HEADS, MLSTM_EXT, LANES), F32), pltpu.VMEM((SUBLANES, LANES), F32),
                        pltpu.VMEM((LANES, t_cols), F32), pltpu.VMEM((LANES, t_cols), F32)],
        compiler_params=_cp(("arbitrary", "arbitrary"), 32),
        name="mlstm",
    )(mq, mk, mv, mo, gt, c0, m0, ng)


def _even_out_kernel(x_ref, fox_ref, hm_ref, w_ref, o_ref, *, fox_transposed):
    dims = (((0,), (0,)), ((), ())) if fox_transposed else (((1,), (0,)), ((), ()))
    y = lax.dot_general(fox_ref[...], w_ref[0:FOX_W, :], dims, preferred_element_type=F32)
    y = y + jnp.dot(hm_ref[...], w_ref[FOX_W:, :], preferred_element_type=F32)
    o_ref[...] = x_ref[...] + y


def _even_out(x, fox, hm, w, *, tm, fox_transposed):
    b, t, _ = x.shape
    fox_spec = (pl.BlockSpec((None, FOX_W, tm), lambda i, j: (i, 0, j)) if fox_transposed
                else pl.BlockSpec((None, tm, FOX_W), lambda i, j: (i, j, 0)))
    return pl.pallas_call(
        functools.partial(_even_out_kernel, fox_transposed=fox_transposed),
        grid=(b, t // tm),
        in_specs=[pl.BlockSpec((None, tm, D_MODEL), lambda i, j: (i, j, 0)), fox_spec,
                  pl.BlockSpec((None, tm, MLSTM_W), lambda i, j: (i, j, 0)),
                  pl.BlockSpec((D_MODEL, D_MODEL), lambda i, j: (0, 0))],
        out_specs=pl.BlockSpec((None, tm, D_MODEL), lambda i, j: (i, j, 0)),
        out_shape=jax.ShapeDtypeStruct((b, t, D_MODEL), F32),
        compiler_params=_cp(("arbitrary", "arbitrary"), 32),
        name="even_out",
    )(x, fox, hm, w)


def _cross_kernel(h_ref, g_ref, wq_ref, wo_ref, mk_ref, mv_ref, o_ref):
    x = h_ref[...]
    xn = _rms(x, g_ref[...]).astype(BF16)
    q = jnp.dot(xn, wq_ref[...], preferred_element_type=F32).astype(BF16)
    outs = []
    for h in range(MEM_HEADS):
        hl = slice(MEM_DH * h, MEM_DH * (h + 1))
        s = lax.dot_general(q[:, hl], mk_ref[:, hl], (((1,), (1,)), ((), ())),
                            preferred_element_type=F32) * MEM_SCALE
        e = jnp.exp(s - jnp.max(s, axis=1, keepdims=True))
        p = e * (1.0 / jnp.sum(e, axis=1, keepdims=True))
        outs.append(jnp.dot(p.astype(BF16), mv_ref[:, hl], preferred_element_type=F32).astype(BF16))
    o = jnp.concatenate(outs, axis=1)
    o_ref[...] = x + jnp.dot(o, wo_ref[...], preferred_element_type=F32)


def _cross(h, g, wq, wo, mem_k, mem_v, *, tm):
    b, t, _ = h.shape
    tok = pl.BlockSpec((None, tm, D_MODEL), lambda i, j: (i, j, 0))
    mem = pl.BlockSpec((None, MEM_TOKENS, MEM_W), lambda i, j: (i, 0, 0))
    return pl.pallas_call(
        _cross_kernel,
        grid=(b, t // tm),
        in_specs=[tok, pl.BlockSpec((1, D_MODEL), lambda i, j: (0, 0)),
                  pl.BlockSpec((D_MODEL, MEM_W), lambda i, j: (0, 0)),
                  pl.BlockSpec((MEM_W, D_MODEL), lambda i, j: (0, 0)), mem, mem],
        out_specs=tok,
        out_shape=jax.ShapeDtypeStruct((b, t, D_MODEL), F32),
        compiler_params=_cp(("arbitrary", "arbitrary"), 32),
        name="cross_attention",
    )(h, g, wq, wo, mem_k, mem_v)


def _memkv_kernel(mem_ref, g_ref, wk_ref, wv_ref, k_ref, v_ref, kb_ref, vb_ref):
    mn = _rms(mem_ref[...], g_ref[...]).astype(BF16)
    k = jnp.dot(mn, wk_ref[...], preferred_element_type=F32)
    v = jnp.dot(mn, wv_ref[...], preferred_element_type=F32)
    k_ref[...] = k
    v_ref[...] = v
    kb_ref[...] = k.astype(BF16)
    vb_ref[...] = v.astype(BF16)


def _memkv(mem, g, wk, wv, *, tm):
    n = mem.shape[0]
    depth = g.shape[0]
    w_spec = pl.BlockSpec((None, D_MODEL, MEM_W), lambda l, i: (l, 0, 0))
    o_spec = pl.BlockSpec((None, tm, MEM_W), lambda l, i: (l, i, 0))
    return pl.pallas_call(
        _memkv_kernel,
        grid=(depth, n // tm),
        in_specs=[pl.BlockSpec((tm, D_MODEL), lambda l, i: (i, 0)),
                  pl.BlockSpec((None, 1, D_MODEL), lambda l, i: (l, 0, 0)), w_spec, w_spec],
        out_specs=[o_spec] * 4,
        out_shape=[jax.ShapeDtypeStruct((depth, n, MEM_W), dt) for dt in (F32, F32, BF16, BF16)],
        compiler_params=_cp(("arbitrary", "arbitrary"), 32),
        name="memory_kv",
    )(mem, g, wk, wv)


def _silu(x):
    return x * _sigmoid(x)


def _ffn_kernel(h_ref, g_ref, wg_ref, wu_ref, wd_ref, o_ref, xn_sc):
    @pl.when(pl.program_id(1) == 0)
    def _():
        x = h_ref[...]
        xn_sc[...] = _rms(x, g_ref[...]).astype(BF16)
        o_ref[...] = x

    xn = xn_sc[...]
    a = _silu(jnp.dot(xn, wg_ref[...], preferred_element_type=F32)) * jnp.dot(xn, wu_ref[...], preferred_element_type=F32)
    o_ref[...] += jnp.dot(a.astype(BF16), wd_ref[...], preferred_element_type=F32)


def _ffn(h, g, wg, wu, wd, *, tm):
    n = h.shape[0]
    tok = pl.BlockSpec((tm, D_MODEL), lambda i, j: (i, 0))
    return pl.pallas_call(
        _ffn_kernel,
        grid=(n // tm, D_FF // FF_TILE),
        in_specs=[tok, pl.BlockSpec((1, D_MODEL), lambda i, j: (0, 0)),
                  pl.BlockSpec((D_MODEL, FF_TILE), lambda i, j: (0, j)),
                  pl.BlockSpec((D_MODEL, FF_TILE), lambda i, j: (0, j)),
                  pl.BlockSpec((FF_TILE, D_MODEL), lambda i, j: (j, 0))],
        out_specs=tok,
        out_shape=jax.ShapeDtypeStruct((n, D_MODEL), F32),
        scratch_shapes=[pltpu.VMEM((tm, D_MODEL), BF16)],
        compiler_params=_cp(("arbitrary", "arbitrary"), 48),
        name="dense_swiglu",
    )(h, g, wg, wu, wd)


def _odd_kernel(h_ref, g_ref, win_ref, cw_ref, wout_ref, past_ref, o_ref, st_ref, u_sc):
    j = pl.program_id(1)
    tm = h_ref.shape[0]

    @pl.when(j == 0)
    def _():
        u_sc[0:SUBLANES, :] = past_ref[...]

    @pl.when(j > 0)
    def _():
        u_sc[0:SUBLANES, :] = u_sc[tm:tm + SUBLANES, :]

    x = h_ref[...]
    xn = _rms(x, g_ref[...]).astype(BF16)
    z = jnp.dot(xn, win_ref[...], preferred_element_type=F32)
    gate_b = z[:, 0:D_MODEL]
    u = z[:, D_MODEL:2 * D_MODEL] * z[:, 2 * D_MODEL:]
    u_sc[SUBLANES:, :] = u
    conv = (cw_ref[0:1, :] * u_sc[SUBLANES - 2:SUBLANES - 2 + tm, :]
            + cw_ref[1:2, :] * u_sc[SUBLANES - 1:SUBLANES - 1 + tm, :]
            + cw_ref[2:3, :] * u)
    o_ref[...] = x + jnp.dot((gate_b * conv).astype(BF16), wout_ref[...], preferred_element_type=F32)

    @pl.when(j == pl.num_programs(1) - 1)
    def _():
        st_ref[...] = u_sc[tm:tm + SUBLANES, :]


def _odd(h, g, w_in, cw, w_out, past, *, tm):
    b, t, _ = h.shape
    tok = pl.BlockSpec((None, tm, D_MODEL), lambda i, j: (i, j, 0))
    st = pl.BlockSpec((None, SUBLANES, D_MODEL), lambda i, j: (i, 0, 0))
    return pl.pallas_call(
        _odd_kernel,
        grid=(b, t // tm),
        in_specs=[tok, pl.BlockSpec((1, D_MODEL), lambda i, j: (0, 0)),
                  pl.BlockSpec((D_MODEL, 3 * D_MODEL), lambda i, j: (0, 0)),
                  pl.BlockSpec((SUBLANES, D_MODEL), lambda i, j: (0, 0)),
                  pl.BlockSpec((D_MODEL, D_MODEL), lambda i, j: (0, 0)), st],
        out_specs=[tok, st],
        out_shape=[jax.ShapeDtypeStruct((b, t, D_MODEL), F32), jax.ShapeDtypeStruct((b, SUBLANES, D_MODEL), F32)],
        scratch_shapes=[pltpu.VMEM((tm + SUBLANES, D_MODEL), F32)],
        compiler_params=_cp(("arbitrary", "arbitrary"), 48),
        name="short_conv_mixer",
    )(h, g, w_in, cw, w_out, past)


def _route(logits, lane_f):
    lg = jnp.where(lane_f < N_EXPERTS, logits, -jnp.inf)
    m1 = jnp.max(lg, axis=1, keepdims=True)
    i1 = jnp.min(jnp.where(lg == m1, lane_f, float(LANES)), axis=1, keepdims=True)
    lg2 = jnp.where(lane_f == i1, -jnp.inf, lg)
    m2 = jnp.max(lg2, axis=1, keepdims=True)
    i2 = jnp.min(jnp.where(lg2 == m2, lane_f, float(LANES)), axis=1, keepdims=True)
    e2 = jnp.exp(m2 - m1)
    inv = 1.0 / (1.0 + e2)
    return jnp.where(lane_f == i1, inv, 0.0) + jnp.where(lane_f == i2, e2 * inv, 0.0)


def _moe_kernel(h_ref, g_ref, rw_ref, wg_ref, wu_ref, wd_ref, fg_ref, o_ref, xn_sc, comb_sc):
    e = pl.program_id(1)
    j = pl.program_id(2)
    tm = h_ref.shape[0]
    lane = lax.broadcasted_iota(jnp.int32, (tm, LANES), 1)

    @pl.when((e == 0) & (j == 0))
    def _():
        x = h_ref[...]
        xn = _rms(x, g_ref[...]).astype(BF16)
        xn_sc[...] = xn
        logits = jnp.dot(xn, rw_ref[...], preferred_element_type=F32)
        comb_sc[...] = _route(logits, lane.astype(F32))
        o_ref[...] = x

    xn = xn_sc[...]
    a = _silu(jnp.dot(xn, wg_ref[...], preferred_element_type=F32)) * jnp.dot(xn, wu_ref[...], preferred_element_type=F32)
    y = jnp.dot(a.astype(BF16), wd_ref[...], preferred_element_type=F32)
    w_e = jnp.sum(jnp.where(lane == e, comb_sc[...], 0.0), axis=1, keepdims=True)
    o_ref[...] += y * w_e

    @pl.when((e == pl.num_programs(1) - 1) & (j == pl.num_programs(2) - 1))
    def _():
        o_ref[...] = _rms(o_ref[...], fg_ref[...])


def _moe(h, g, rw, wg, wu, wd, fg, *, tm):
    n = h.shape[0]
    tok = pl.BlockSpec((tm, D_MODEL), lambda i, e, j: (i, 0))
    vec = pl.BlockSpec((1, D_MODEL), lambda i, e, j: (0, 0))
    return pl.pallas_call(
        _moe_kernel,
        grid=(n // tm, N_EXPERTS, D_FF // FF_TILE),
        in_specs=[tok, vec, pl.BlockSpec((D_MODEL, LANES), lambda i, e, j: (0, 0)),
                  pl.BlockSpec((None, D_MODEL, FF_TILE), lambda i, e, j: (e, 0, j)),
                  pl.BlockSpec((None, D_MODEL, FF_TILE), lambda i, e, j: (e, 0, j)),
                  pl.BlockSpec((None, FF_TILE, D_MODEL), lambda i, e, j: (e, j, 0)), vec],
        out_specs=tok,
        out_shape=jax.ShapeDtypeStruct((n, D_MODEL), F32),
        scratch_shapes=[pltpu.VMEM((tm, D_MODEL), BF16), pltpu.VMEM((tm, LANES), F32)],
        compiler_params=_cp(("arbitrary", "arbitrary", "arbitrary"), 48),
        name="moe_swiglu",
    )(h, g, rw, wg, wu, wd, fg)


def _pack_params(p):
    fq, fk, fv, ff, mq, mk, mv, mo, mi, mf = jnp.split(p['even_w_in'][0], list(EVEN_SPLITS), axis=1)
    gate_w = jnp.concatenate([ff, mi, mf, jnp.zeros((D_MODEL, LANES - _G_END), F32)], axis=1)
    gate_b = jnp.concatenate([p['fox_b_f'][0], p['mlstm_b_i'][0], p['mlstm_b_f'][0], jnp.zeros((LANES - _G_END,), F32)])
    row = lambda a: a.reshape(1, -1).astype(F32)
    return dict(
        even_w=jnp.concatenate([fq, fk, fv, mq, mk, mv, mo, gate_w], axis=1).astype(BF16),
        even_b=gate_b.reshape(1, LANES),
        even_w_out=p['even_w_out'][0].astype(BF16),
        mlstm_norm_g=row(p['mlstm_norm_g'][0]),
        norm_mix_g=[row(p['norm_mix_g'][l]) for l in range(2)],
        norm_cross_g=[row(p['norm_cross_g'][l]) for l in range(2)],
        norm_ffn_g=[row(p['norm_ffn_g'][l]) for l in range(2)],
        final_norm_g=row(p['final_norm_g']),
        mem_wq=p['mem_wq'].astype(BF16), mem_wo=p['mem_wo'].astype(BF16),
        ffn_wg=p['ffn_w_gate'][0].astype(BF16), ffn_wu=p['ffn_w_up'][0].astype(BF16), ffn_wd=p['ffn_w_down'][0].astype(BF16),
        odd_w_in=p['odd_w_in'][0].astype(BF16), odd_w_out=p['odd_w_out'][0].astype(BF16),
        conv_w=jnp.concatenate([p['conv_w'][0], jnp.zeros((SUBLANES - CONV_W, D_MODEL), F32)], axis=0),
        router_w=jnp.concatenate([p['router_w'][0], jnp.zeros((D_MODEL, LANES - N_EXPERTS), F32)], axis=1).astype(BF16),
        moe_wg=p['moe_w_gate'][0].astype(BF16), moe_wu=p['moe_w_up'][0].astype(BF16), moe_wd=p['moe_w_down'][0].astype(BF16),
    )


def _trunk(x, mem_k, mem_v, caches, w, *, tm, tm_wide, chunk):
    b, t, _ = x.shape
    n = b * t
    flat = lambda a: a.reshape(n, a.shape[-1])
    per_b = lambda a: a.reshape(b, t, a.shape[-1])

    if caches is None:
        f0 = jnp.zeros((b, 1, LANES), F32)
    else:
        cache_k, cache_v, cache_lf, c0, n0, m0, conv_st = caches
        past_len = cache_k.shape[1]
        lf_pad = jnp.pad(cache_lf.astype(F32), ((0, 0), (0, 0), (0, LANES - FOX_HEADS)))
        ka_cache, f0 = _cache_prep(cache_k.reshape(b, past_len, FOX_W).astype(F32), lf_pad, tm=FOX_TILE)
    qa, ka, fk, fv, vb, mq, mk, mv, mo, gt = _even_in(flat(x), w['norm_mix_g'][0], w['even_w'], w['even_b'], f0,
                                                      tm=tm, tiles_per_batch=t // tm)
    if caches is None:
        fox_t = _fox_attention(per_b(qa), per_b(ka), per_b(vb), q0=0)
        c_ext0 = jnp.zeros((b, MLSTM_HEADS, MLSTM_EXT, LANES), F32)
        m_ext0 = jnp.zeros((b, SUBLANES, LANES), F32)
        past = jnp.zeros((b, SUBLANES, D_MODEL), F32)
    else:
        pad_t = lambda a: jnp.pad(per_b(a), ((0, 0), (0, FOX_TILE - t), (0, 0)))
        k_all = jnp.concatenate([ka_cache, pad_t(ka)], axis=1)
        v_all = jnp.concatenate([cache_v.reshape(b, past_len, FOX_W).astype(BF16), pad_t(vb)], axis=1)
        fox_full = _fox_attention(pad_t(qa), k_all, v_all, q0=past_len // FOX_TILE)
        fox_t = jnp.swapaxes(fox_full[:, :, :t], 1, 2)
        c_ext0 = jnp.concatenate([
            jnp.pad(c0.astype(F32), ((0, 0), (0, 0), (0, 0), (0, LANES - MLSTM_DK))),
            jnp.pad(n0.astype(F32)[:, :, None, :], ((0, 0), (0, 0), (0, MLSTM_EXT - MLSTM_DV - 1), (0, LANES - MLSTM_DK)))],
            axis=2)
        m_ext0 = jnp.broadcast_to(jnp.pad(m0.astype(F32), ((0, 0), (0, SUBLANES - MLSTM_HEADS)))[:, :, None],
                                  (b, SUBLANES, LANES))
        past = jnp.pad(conv_st.astype(F32), ((0, 0), (SUBLANES - (CONV_W - 1), 0), (0, 0)))
    hm, c_ext, m_ext = _mlstm(per_b(mq), per_b(mk), per_b(mv), per_b(mo), per_b(gt), c_ext0, m_ext0,
                              w['mlstm_norm_g'], chunk=chunk)
    h = _even_out(x, fox_t, hm, w['even_w_out'], tm=tm, fox_transposed=caches is None)
    h = _cross(h, w['norm_cross_g'][0], w['mem_wq'][0], w['mem_wo'][0], mem_k[0], mem_v[0], tm=tm)
    h = _ffn(flat(h), w['norm_ffn_g'][0], w['ffn_wg'], w['ffn_wu'], w['ffn_wd'], tm=tm_wide)

    h, conv_new = _odd(per_b(h), w['norm_mix_g'][1], w['odd_w_in'], w['conv_w'], w['odd_w_out'], past, tm=tm)
    h = _cross(h, w['norm_cross_g'][1], w['mem_wq'][1], w['mem_wo'][1], mem_k[1], mem_v[1], tm=tm)
    y = _moe(flat(h), w['norm_ffn_g'][1], w['router_w'], w['moe_wg'], w['moe_wu'], w['moe_wd'], w['final_norm_g'],
             tm=tm_wide)

    states = (
        fk.reshape(1, b, t, FOX_HEADS, FOX_DH), fv.reshape(1, b, t, FOX_HEADS, FOX_DH),
        per_b(gt)[None, :, :, _G_FOX:_G_IG],
        c_ext[None, :, :, 0:MLSTM_DV, 0:MLSTM_DK], c_ext[None, :, :, MLSTM_DV, 0:MLSTM_DK], m_ext[None, :, 0:MLSTM_HEADS, 0],
        conv_new[None, :, SUBLANES - (CONV_W - 1):, :],
    )
    return per_b(y), states


def kernel(x_prompt, x_sample, mem_prompt, cache_fox_k, cache_fox_v, cache_fox_logf, state_mlstm_c, state_mlstm_n, state_mlstm_m, state_conv, cache_mem_k, cache_mem_v, norm_mix_g, norm_mem_g, norm_cross_g, norm_ffn_g, final_norm_g, even_w_in, fox_b_f, mlstm_b_i, mlstm_b_f, mlstm_norm_g, even_w_out, odd_w_in, conv_w, odd_w_out, mem_wq, mem_wk, mem_wv, mem_wo, ffn_w_gate, ffn_w_up, ffn_w_down, router_w, moe_w_gate, moe_w_up, moe_w_down):
    w = _pack_params(dict(
        norm_mix_g=norm_mix_g, norm_cross_g=norm_cross_g, norm_ffn_g=norm_ffn_g, final_norm_g=final_norm_g,
        even_w_in=even_w_in, fox_b_f=fox_b_f, mlstm_b_i=mlstm_b_i, mlstm_b_f=mlstm_b_f, mlstm_norm_g=mlstm_norm_g,
        even_w_out=even_w_out, odd_w_in=odd_w_in, conv_w=conv_w, odd_w_out=odd_w_out, mem_wq=mem_wq, mem_wo=mem_wo,
        ffn_w_gate=ffn_w_gate, ffn_w_up=ffn_w_up, ffn_w_down=ffn_w_down, router_w=router_w,
        moe_w_gate=moe_w_gate, moe_w_up=moe_w_up, moe_w_down=moe_w_down))

    bp, tp, _ = x_prompt.shape
    bs, ts, _ = x_sample.shape
    depth = norm_mem_g.shape[0]

    mem_k_p, mem_v_p, mem_kb, mem_vb = _memkv(mem_prompt.reshape(bp * MEM_TOKENS, D_MODEL),
                                              norm_mem_g.reshape(depth, 1, D_MODEL).astype(F32),
                                              mem_wk.astype(BF16), mem_wv.astype(BF16), tm=512)
    per_layer = lambda a, nb: a.reshape(depth, nb, MEM_TOKENS, MEM_W)
    y_prompt, st_p = _trunk(x_prompt, per_layer(mem_kb, bp), per_layer(mem_vb, bp), None, w,
                            tm=256, tm_wide=1024, chunk=MLSTM_CHUNK)

    caches = (cache_fox_k[0], cache_fox_v[0], cache_fox_logf[0], state_mlstm_c[0], state_mlstm_n[0],
              state_mlstm_m[0], state_conv[0])
    y_sample, st_s = _trunk(x_sample, per_layer(cache_mem_k.astype(BF16), bs), per_layer(cache_mem_v.astype(BF16), bs),
                            caches, w, tm=ts, tm_wide=bs * ts, chunk=ts)

    mem_shape = (depth, bp, MEM_TOKENS, MEM_HEADS, MEM_DH)
    return (y_prompt, y_sample) + st_p + (mem_k_p.reshape(mem_shape), mem_v_p.reshape(mem_shape)) + st_s
```

```python
import functools

import jax
import jax.numpy as jnp
from jax import lax
from jax.experimental import pallas as pl
from jax.experimental.pallas import tpu as pltpu

F32 = jnp.float32
BF16 = jnp.bfloat16

D_MODEL = 1024
EPS = 1e-6
FOX_HEADS = 8
FOX_DH = 64
FOX_W = FOX_HEADS * FOX_DH
FOX_SCALE = FOX_DH ** -0.5
MLSTM_HEADS = 4
MLSTM_DV = 128
MLSTM_DK = 64
MLSTM_W = MLSTM_HEADS * MLSTM_DV
MLSTM_QK_W = MLSTM_HEADS * MLSTM_DK
MLSTM_SCALE = MLSTM_DK ** -0.5
MLSTM_CHUNK = 256
MLSTM_EXT = 256
MEM_TOKENS = 256
MEM_HEADS = 4
MEM_DH = 128
MEM_W = MEM_HEADS * MEM_DH
MEM_SCALE = MEM_DH ** -0.5
D_FF = 3584
N_EXPERTS = 8
TOP_K = 2
MOE_EXPERT_TILE = 1024
MOE_TOKEN_TILE = 512
MOE_DMA_UNROLL = 8
CONV_W = 3
EVEN_SIZES = (FOX_W, FOX_W, FOX_W, FOX_HEADS, MLSTM_QK_W, MLSTM_QK_W, MLSTM_W, MLSTM_W, MLSTM_HEADS, MLSTM_HEADS)
EVEN_SPLITS = tuple(sum(EVEN_SIZES[:i + 1]) for i in range(len(EVEN_SIZES) - 1))

LANES = 128
SUBLANES = 8
FOX_AUG_W = FOX_HEADS * LANES
FOX_TILE = 256
FOX_LOOKAHEAD = 3
FOX_PV_DELAY = 1
FF_TILE = 512
_C_Q, _C_K, _C_V, _C_MQ, _C_MK, _C_MV, _C_MO, _C_G, _C_END = 0, 512, 1024, 1536, 1792, 2048, 2560, 3072, 3200
_G_FOX, _G_IG, _G_LF, _G_END = 0, 8, 12, 16
NEG = -1e30
LOG2E = 1.4426950408889634


def _cp(sem, vmem_mb):
    return pltpu.CompilerParams(dimension_semantics=sem, vmem_limit_bytes=vmem_mb * 1024 * 1024)


def _rms(x, g):
    return x * lax.rsqrt(jnp.mean(x * x, axis=-1, keepdims=True) + EPS) * g


def _log_sigmoid(x):
    return jnp.minimum(x, 0.0) - jnp.log1p(jnp.exp(-jnp.abs(x)))


def _sigmoid(x):
    return 1.0 / (1.0 + jnp.exp(-x))


def _cumsum_rows(x):
    n = x.shape[0]
    row = lax.broadcasted_iota(jnp.int32, x.shape, 0)
    s = 1
    while s < n:
        x = x + jnp.where(row >= s, pltpu.roll(x, s, axis=0), 0.0)
        s *= 2
    return x


def _split3(f):
    hi = f.astype(BF16).astype(F32)
    r = f - hi
    mid = r.astype(BF16).astype(F32)
    return hi, mid, r - mid


def _head_block(src, h, lane):
    p, odd = divmod(h, 2)
    blk = src[:, LANES * p:LANES * (p + 1)]
    return pltpu.roll(blk, FOX_DH, axis=1) if odd else blk


def _fox_aug(src, cum_f, h, lane, is_query):
    blk = _head_block(src, h, lane)
    hi, mid, lo = _split3(jnp.broadcast_to(cum_f[:, h:h + 1] * LOG2E, blk.shape))
    if is_query:
        aug = jnp.where(lane == 64, hi, jnp.where(lane == 65, mid, jnp.where(lane == 66, lo,
                        jnp.where(lane < 70, 1.0, 0.0))))
    else:
        aug = jnp.where(lane < 67, 1.0, jnp.where(lane == 67, -hi, jnp.where(lane == 68, -mid,
                        jnp.where(lane == 69, -lo, 0.0))))
    return jnp.where(lane < FOX_DH, blk, aug).astype(BF16)


def _even_in_kernel(x_ref, g_ref, w_ref, b_ref, f0_ref,
                    qa_ref, ka_ref, fk_ref, fv_ref, vb_ref, mq_ref, mk_ref, mv_ref, mo_ref, gt_ref,
                    carry_ref, *, tiles_per_batch):
    @pl.when(pl.program_id(0) % tiles_per_batch == 0)
    def _():
        carry_ref[...] = f0_ref[...]

    xn = _rms(x_ref[...], g_ref[...]).astype(BF16)
    acc = jnp.dot(xn, w_ref[...], preferred_element_type=F32)
    tm = acc.shape[0]
    lane = lax.broadcasted_iota(jnp.int32, (tm, LANES), 1)

    gates = acc[:, _C_G:_C_END] + b_ref[...]
    ls = _log_sigmoid(gates)
    is_ig = (lane >= _G_IG) & (lane < _G_LF)
    gt_ref[...] = jnp.where(is_ig, gates, jnp.where(lane < _G_END, ls, 0.0))
    cum_f = _cumsum_rows(jnp.where(lane < _G_IG, ls, 0.0)) + carry_ref[...]
    carry_ref[...] = cum_f[tm - 1:tm, :]

    q_all = acc[:, _C_Q:_C_K] * (FOX_SCALE * LOG2E)
    k_all = acc[:, _C_K:_C_V]
    v_all = acc[:, _C_V:_C_MQ]
    for h in range(FOX_HEADS):
        qa_ref[:, LANES * h:LANES * (h + 1)] = _fox_aug(q_all, cum_f, h, lane, True)
        ka_ref[:, LANES * h:LANES * (h + 1)] = _fox_aug(k_all, cum_f, h, lane, False)
    fk_ref[...] = k_all
    fv_ref[...] = v_all
    vb_ref[...] = v_all.astype(BF16)

    mq_all = acc[:, _C_MQ:_C_MK] * MLSTM_SCALE
    mk_all = acc[:, _C_MK:_C_MV]
    for h in range(MLSTM_HEADS):
        mq_ref[:, LANES * h:LANES * (h + 1)] = jnp.where(lane < MLSTM_DK, _head_block(mq_all, h, lane), 0.0).astype(BF16)
        mk_ref[:, LANES * h:LANES * (h + 1)] = jnp.where(lane < MLSTM_DK, _head_block(mk_all, h, lane), 0.0).astype(BF16)
    mv_ref[...] = acc[:, _C_MV:_C_MO].astype(BF16)
    mo_ref[...] = acc[:, _C_MO:_C_G].astype(BF16)


def _even_in(x, g, w, bias, f0, *, tm, tiles_per_batch):
    n = x.shape[0]
    row = lambda i: (i, 0)
    fixed = lambda i: (0, 0)
    widths = (FOX_AUG_W, FOX_AUG_W, FOX_W, FOX_W, FOX_W, MLSTM_W, MLSTM_W, MLSTM_W, MLSTM_W, LANES)
    dtypes = (BF16, BF16, F32, F32, BF16, BF16, BF16, BF16, BF16, F32)
    return pl.pallas_call(
        functools.partial(_even_in_kernel, tiles_per_batch=tiles_per_batch),
        grid=(n // tm,),
        in_specs=[pl.BlockSpec((tm, D_MODEL), row), pl.BlockSpec((1, D_MODEL), fixed),
                  pl.BlockSpec((D_MODEL, _C_END), fixed), pl.BlockSpec((1, LANES), fixed),
                  pl.BlockSpec((None, 1, LANES), lambda i: (i // tiles_per_batch, 0, 0))],
        out_specs=[pl.BlockSpec((tm, wd), row) for wd in widths],
        out_shape=[jax.ShapeDtypeStruct((n, wd), dt) for wd, dt in zip(widths, dtypes)],
        scratch_shapes=[pltpu.VMEM((1, LANES), F32)],
        compiler_params=_cp(("arbitrary",), 48),
        name="even_in",
    )(x, g, w, bias, f0)


def _cache_prep_kernel(k_ref, lf_ref, ka_ref, fend_ref, carry_ref):
    @pl.when(pl.program_id(1) == 0)
    def _():
        carry_ref[...] = jnp.zeros_like(carry_ref)

    k_all = k_ref[...]
    tm = k_all.shape[0]
    lane = lax.broadcasted_iota(jnp.int32, (tm, LANES), 1)
    cum_f = _cumsum_rows(lf_ref[...]) + carry_ref[...]
    carry_ref[...] = cum_f[tm - 1:tm, :]
    fend_ref[...] = cum_f[tm - 1:tm, :]
    for h in range(FOX_HEADS):
        ka_ref[:, LANES * h:LANES * (h + 1)] = _fox_aug(k_all, cum_f, h, lane, False)


def _cache_prep(cache_k, cache_lf, *, tm):
    b, p, _ = cache_k.shape
    return pl.pallas_call(
        _cache_prep_kernel,
        grid=(b, p // tm),
        in_specs=[pl.BlockSpec((None, tm, FOX_W), lambda i, j: (i, j, 0)),
                  pl.BlockSpec((None, tm, LANES), lambda i, j: (i, j, 0))],
        out_specs=[pl.BlockSpec((None, tm, FOX_AUG_W), lambda i, j: (i, j, 0)),
                   pl.BlockSpec((None, 1, LANES), lambda i, j: (i, 0, 0))],
        out_shape=[jax.ShapeDtypeStruct((b, p, FOX_AUG_W), BF16), jax.ShapeDtypeStruct((b, 1, LANES), F32)],
        scratch_shapes=[pltpu.VMEM((1, LANES), F32)],
        compiler_params=_cp(("arbitrary", "arbitrary"), 32),
        name="fox_cache_prep",
    )(cache_k, cache_lf)


def _fox_kernel(q_ref, k_ref, v_ref, o_ref, m_sc, l_sc, acc_sc, *, q0):
    t = FOX_TILE
    n_full = q0 + pl.program_id(1)
    visible = (lax.broadcasted_iota(jnp.int32, (t, t), 0) <= lax.broadcasted_iota(jnp.int32, (t, t), 1))
    m_sc[...] = jnp.full_like(m_sc, NEG)
    l_sc[...] = jnp.zeros_like(l_sc)
    acc_sc[...] = jnp.zeros_like(acc_sc)

    def step(j, diagonal):
        off = pl.multiple_of(j * t, t)

        def scores(h):
            hl = slice(LANES * h, LANES * (h + 1))
            return lax.dot_general(k_ref[pl.ds(off, t), hl], q_ref[:, hl], (((1,), (1,)), ((), ())),
                                   preferred_element_type=F32)

        def weighted_values(h, p, alpha):
            pair, hh = divmod(h, 2)
            v_t = v_ref[pl.ds(off, t), LANES * pair:LANES * (pair + 1)].T
            pv = jnp.dot(v_t[FOX_DH * hh:FOX_DH * (hh + 1), :], p, preferred_element_type=F32)
            rows = slice(FOX_DH * h, FOX_DH * (h + 1))
            acc_sc[rows, :] = alpha * acc_sc[rows, :] + pv

        pending = {h: scores(h) for h in range(FOX_LOOKAHEAD)}
        ready = {}
        for h in range(FOX_HEADS):
            s = pending.pop(h)
            if diagonal:
                s = jnp.where(visible, s, NEG)
            m_old = m_sc[h:h + 1, :]
            m_new = jnp.maximum(m_old, jnp.max(s, axis=0, keepdims=True))
            alpha = jnp.exp2(m_old - m_new)
            p = jnp.exp2(s - m_new)
            m_sc[h:h + 1, :] = m_new
            l_sc[h:h + 1, :] = alpha * l_sc[h:h + 1, :] + jnp.sum(p, axis=0, keepdims=True)
            ready[h] = (p.astype(BF16), alpha)
            if h + FOX_LOOKAHEAD < FOX_HEADS:
                pending[h + FOX_LOOKAHEAD] = scores(h + FOX_LOOKAHEAD)
            if h - FOX_PV_DELAY in ready:
                weighted_values(h - FOX_PV_DELAY, *ready.pop(h - FOX_PV_DELAY))
        for h in sorted(ready):
            weighted_values(h, *ready[h])

    def body(j, c):
        step(j, False)
        return c

    lax.fori_loop(0, n_full, body, 0)
    step(n_full, True)
    for h in range(FOX_HEADS):
        rows = slice(FOX_DH * h, FOX_DH * (h + 1))
        o_ref[rows, :] = (acc_sc[rows, :] / l_sc[h:h + 1, :]).astype(BF16)


def _fox_attention(q_aug, k_aug, v, *, q0):
    b, tq_total, _ = q_aug.shape
    t_kv = k_aug.shape[1]
    nq = tq_total // FOX_TILE
    return pl.pallas_call(
        functools.partial(_fox_kernel, q0=q0),
        grid=(b, nq),
        in_specs=[pl.BlockSpec((None, FOX_TILE, FOX_AUG_W), lambda bi, i: (bi, i, 0)),
                  pl.BlockSpec((None, t_kv, FOX_AUG_W), lambda bi, i: (bi, 0, 0)),
                  pl.BlockSpec((None, t_kv, FOX_W), lambda bi, i: (bi, 0, 0))],
        out_specs=pl.BlockSpec((None, FOX_W, FOX_TILE), lambda bi, i: (bi, 0, i)),
        out_shape=jax.ShapeDtypeStruct((b, FOX_W, tq_total), BF16),
        scratch_shapes=[pltpu.VMEM((FOX_HEADS, FOX_TILE), F32), pltpu.VMEM((FOX_HEADS, FOX_TILE), F32),
                        pltpu.VMEM((FOX_W, FOX_TILE), F32)],
        compiler_params=_cp(("arbitrary", "arbitrary"), 48),
        name="fox_attention",
    )(q_aug, k_aug, v)


def _mlstm_kernel(mq_ref, mk_ref, mv_ref, mo_ref, gt_ref, c0_ref, m0_ref, ng_ref,
                  hm_ref, co_ref, mout_ref, c_sc, m_sc, gt_t, cs_t, *, chunk):
    step = pl.program_id(1)

    @pl.when(step == 0)
    def _():
        c_sc[...] = c0_ref[...]
        m_sc[...] = m0_ref[...]

    ln = chunk
    nt = (((1,), (1,)), ((), ()))
    causal = lax.broadcasted_iota(jnp.int32, (ln, ln), 0) >= lax.broadcasted_iota(jnp.int32, (ln, ln), 1)
    lane = lax.broadcasted_iota(jnp.int32, (ln, LANES), 1)
    heads = range(MLSTM_HEADS)
    hl = [slice(LANES * h, LANES * (h + 1)) for h in heads]

    g = gt_ref[...]
    cs = _cumsum_rows(g)
    if ln < LANES:
        pad = jnp.zeros((LANES - ln, LANES), F32)
        gt_t[...] = jnp.concatenate([g, pad], axis=0).T
        cs_t[...] = jnp.concatenate([cs, pad], axis=0).T
    else:
        gt_t[...] = g.T
        cs_t[...] = cs.T
    ig_col = [g[:, _G_IG + h:_G_IG + h + 1] for h in heads]
    b_col = [cs[:, _G_LF + h:_G_LF + h + 1] for h in heads]
    b_last = [b[ln - 1:ln, :] for b in b_col]

    qk = [lax.dot_general(mq_ref[:, hl[h]], mk_ref[:, hl[h]], nt, preferred_element_type=F32) for h in heads]
    m_loc, a_sum, av, g_max, upd = [], [], [], [], []
    for h in heads:
        d = jnp.where(causal, b_col[h] - cs_t[_G_LF + h:_G_LF + h + 1, 0:ln] + gt_t[_G_IG + h:_G_IG + h + 1, 0:ln],
                      -jnp.inf)
        m_loc.append(jnp.max(d, axis=1, keepdims=True))
        a = qk[h] * jnp.exp(d - m_loc[h])
        a_sum.append(jnp.sum(a, axis=1, keepdims=True))
        av.append(jnp.dot(a.astype(BF16), mv_ref[:, hl[h]], preferred_element_type=F32))
    for h in heads:
        g_col = b_last[h] - b_col[h] + ig_col[h]
        g_max.append(jnp.max(g_col, axis=0, keepdims=True))
        w_loc = jnp.exp(g_col - g_max[h])
        vw = jnp.concatenate([mv_ref[:, hl[h]].astype(F32) * w_loc, jnp.where(lane == 0, w_loc, 0.0)], axis=1)
        upd.append(lax.dot_general(vw.astype(BF16), mk_ref[:, hl[h]], (((0,), (0,)), ((), ())),
                                   preferred_element_type=F32))

    for h in heads:
        m_prev = m_sc[h:h + 1, 0:1]
        c_prev = c_sc[h]
        cq = lax.dot_general(mq_ref[:, hl[h]], c_prev.astype(BF16), nt, preferred_element_type=F32)
        inter = b_col[h] + m_prev
        m_t = jnp.maximum(inter, m_loc[h])
        w_inter = jnp.exp(inter - m_t)
        w_intra = jnp.exp(m_loc[h] - m_t)
        num = w_inter * cq[:, 0:MLSTM_DV] + w_intra * av[h]
        den = w_inter * cq[:, MLSTM_DV:MLSTM_DV + 1] + w_intra * a_sum[h]
        h_cell = num / jnp.maximum(jnp.abs(den), jnp.exp(-m_t))

        m_new = jnp.maximum(b_last[h] + m_prev, g_max[h])
        c_sc[h] = jnp.exp(b_last[h] + m_prev - m_new) * c_prev + jnp.exp(g_max[h] - m_new) * upd[h]
        m_sc[h:h + 1, :] = jnp.broadcast_to(m_new, (1, LANES))

        hn = h_cell * lax.rsqrt(jnp.mean(h_cell * h_cell, axis=1, keepdims=True) + EPS)
        hm_ref[:, hl[h]] = (hn * ng_ref[:, hl[h]] * _sigmoid(mo_ref[:, hl[h]].astype(F32))).astype(BF16)

    @pl.when(step == pl.num_programs(1) - 1)
    def _():
        co_ref[...] = c_sc[...]
        mout_ref[...] = m_sc[...]


def _mlstm(mq, mk, mv, mo, gt, c0, m0, ng, *, chunk):
    b, t, _ = mq.shape
    tok = lambda wd: pl.BlockSpec((None, chunk, wd), lambda i, j: (i, j, 0))
    c_spec = pl.BlockSpec((None, MLSTM_HEADS, MLSTM_EXT, LANES), lambda i, j: (i, 0, 0, 0))
    m_spec = pl.BlockSpec((None, SUBLANES, LANES), lambda i, j: (i, 0, 0))
    t_cols = max(chunk, LANES)
    return pl.pallas_call(
        functools.partial(_mlstm_kernel, chunk=chunk),
        grid=(b, t // chunk),
        in_specs=[tok(MLSTM_W), tok(MLSTM_W), tok(MLSTM_W), tok(MLSTM_W), tok(LANES), c_spec, m_spec,
                  pl.BlockSpec((1, MLSTM_W), lambda i, j: (0, 0))],
        out_specs=[tok(MLSTM_W), c_spec, m_spec],
        out_shape=[jax.ShapeDtypeStruct((b, t, MLSTM_W), BF16),
                   jax.ShapeDtypeStruct((b, MLSTM_HEADS, MLSTM_EXT, LANES), F32),
                   jax.ShapeDtypeStruct((b, SUBLANES, LANES), F32)],
        scratch_shapes=[pltpu.VMEM((MLSTM_HEADS, MLSTM_EXT, LANES), F32), pltpu.VMEM((SUBLANES, LANES), F32),
                        pltpu.VMEM((LANES, t_cols), F32), pltpu.VMEM((LANES, t_cols), F32)],
        compiler_params=_cp(("arbitrary", "arbitrary"), 32),
        name="mlstm",
    )(mq, mk, mv, mo, gt, c0, m0, ng)


def _even_out_kernel(x_ref, fox_ref, hm_ref, w_ref, o_ref, *, fox_transposed):
    dims = (((0,), (0,)), ((), ())) if fox_transposed else (((1,), (0,)), ((), ()))
    y = lax.dot_general(fox_ref[...], w_ref[0:FOX_W, :], dims, preferred_element_type=F32)
    y = y + jnp.dot(hm_ref[...], w_ref[FOX_W:, :], preferred_element_type=F32)
    o_ref[...] = x_ref[...] + y


def _even_out(x, fox, hm, w, *, tm, fox_transposed):
    b, t, _ = x.shape
    fox_spec = (pl.BlockSpec((None, FOX_W, tm), lambda i, j: (i, 0, j)) if fox_transposed
                else pl.BlockSpec((None, tm, FOX_W), lambda i, j: (i, j, 0)))
    return pl.pallas_call(
        functools.partial(_even_out_kernel, fox_transposed=fox_transposed),
        grid=(b, t // tm),
        in_specs=[pl.BlockSpec((None, tm, D_MODEL), lambda i, j: (i, j, 0)), fox_spec,
                  pl.BlockSpec((None, tm, MLSTM_W), lambda i, j: (i, j, 0)),
                  pl.BlockSpec((D_MODEL, D_MODEL), lambda i, j: (0, 0))],
        out_specs=pl.BlockSpec((None, tm, D_MODEL), lambda i, j: (i, j, 0)),
        out_shape=jax.ShapeDtypeStruct((b, t, D_MODEL), F32),
        compiler_params=_cp(("arbitrary", "arbitrary"), 32),
        name="even_out",
    )(x, fox, hm, w)


def _cross_kernel(h_ref, g_ref, wq_ref, wo_ref, mk_ref, mv_ref, o_ref):
    x = h_ref[...]
    xn = _rms(x, g_ref[...]).astype(BF16)
    q = jnp.dot(xn, wq_ref[...], preferred_element_type=F32).astype(BF16)
    outs = []
    for h in range(MEM_HEADS):
        hl = slice(MEM_DH * h, MEM_DH * (h + 1))
        s = lax.dot_general(q[:, hl], mk_ref[:, hl], (((1,), (1,)), ((), ())),
                            preferred_element_type=F32) * MEM_SCALE
        e = jnp.exp(s - jnp.max(s, axis=1, keepdims=True))
        p = e * (1.0 / jnp.sum(e, axis=1, keepdims=True))
        outs.append(jnp.dot(p.astype(BF16), mv_ref[:, hl], preferred_element_type=F32).astype(BF16))
    o = jnp.concatenate(outs, axis=1)
    o_ref[...] = x + jnp.dot(o, wo_ref[...], preferred_element_type=F32)


def _cross(h, g, wq, wo, mem_k, mem_v, *, tm):
    b, t, _ = h.shape
    tok = pl.BlockSpec((None, tm, D_MODEL), lambda i, j: (i, j, 0))
    mem = pl.BlockSpec((None, MEM_TOKENS, MEM_W), lambda i, j: (i, 0, 0))
    return pl.pallas_call(
        _cross_kernel,
        grid=(b, t // tm),
        in_specs=[tok, pl.BlockSpec((1, D_MODEL), lambda i, j: (0, 0)),
                  pl.BlockSpec((D_MODEL, MEM_W), lambda i, j: (0, 0)),
                  pl.BlockSpec((MEM_W, D_MODEL), lambda i, j: (0, 0)), mem, mem],
        out_specs=tok,
        out_shape=jax.ShapeDtypeStruct((b, t, D_MODEL), F32),
        compiler_params=_cp(("arbitrary", "arbitrary"), 32),
        name="cross_attention",
    )(h, g, wq, wo, mem_k, mem_v)


def _memkv_kernel(mem_ref, g_ref, wk_ref, wv_ref, k_ref, v_ref, kb_ref, vb_ref):
    mn = _rms(mem_ref[...], g_ref[...]).astype(BF16)
    k = jnp.dot(mn, wk_ref[...], preferred_element_type=F32)
    v = jnp.dot(mn, wv_ref[...], preferred_element_type=F32)
    k_ref[...] = k
    v_ref[...] = v
    kb_ref[...] = k.astype(BF16)
    vb_ref[...] = v.astype(BF16)


def _memkv(mem, g, wk, wv, *, tm):
    n = mem.shape[0]
    depth = g.shape[0]
    w_spec = pl.BlockSpec((None, D_MODEL, MEM_W), lambda l, i: (l, 0, 0))
    o_spec = pl.BlockSpec((None, tm, MEM_W), lambda l, i: (l, i, 0))
    return pl.pallas_call(
        _memkv_kernel,
        grid=(depth, n // tm),
        in_specs=[pl.BlockSpec((tm, D_MODEL), lambda l, i: (i, 0)),
                  pl.BlockSpec((None, 1, D_MODEL), lambda l, i: (l, 0, 0)), w_spec, w_spec],
        out_specs=[o_spec] * 4,
        out_shape=[jax.ShapeDtypeStruct((depth, n, MEM_W), dt) for dt in (F32, F32, BF16, BF16)],
        compiler_params=_cp(("arbitrary", "arbitrary"), 32),
        name="memory_kv",
    )(mem, g, wk, wv)


def _silu(x):
    return x * _sigmoid(x)


def _ffn_kernel(h_ref, g_ref, wg_ref, wu_ref, wd_ref, o_ref, xn_sc):
    @pl.when(pl.program_id(1) == 0)
    def _():
        x = h_ref[...]
        xn_sc[...] = _rms(x, g_ref[...]).astype(BF16)
        o_ref[...] = x

    xn = xn_sc[...]
    a = _silu(jnp.dot(xn, wg_ref[...], preferred_element_type=F32)) * jnp.dot(xn, wu_ref[...], preferred_element_type=F32)
    o_ref[...] += jnp.dot(a.astype(BF16), wd_ref[...], preferred_element_type=F32)


def _ffn(h, g, wg, wu, wd, *, tm):
    n = h.shape[0]
    tok = pl.BlockSpec((tm, D_MODEL), lambda i, j: (i, 0))
    return pl.pallas_call(
        _ffn_kernel,
        grid=(n // tm, D_FF // FF_TILE),
        in_specs=[tok, pl.BlockSpec((1, D_MODEL), lambda i, j: (0, 0)),
                  pl.BlockSpec((D_MODEL, FF_TILE), lambda i, j: (0, j)),
                  pl.BlockSpec((D_MODEL, FF_TILE), lambda i, j: (0, j)),
                  pl.BlockSpec((FF_TILE, D_MODEL), lambda i, j: (j, 0))],
        out_specs=tok,
        out_shape=jax.ShapeDtypeStruct((n, D_MODEL), F32),
        scratch_shapes=[pltpu.VMEM((tm, D_MODEL), BF16)],
        compiler_params=_cp(("arbitrary", "arbitrary"), 48),
        name="dense_swiglu",
    )(h, g, wg, wu, wd)


def _odd_kernel(h_ref, g_ref, win_ref, cw_ref, wout_ref, past_ref, o_ref, st_ref, u_sc):
    j = pl.program_id(1)
    tm = h_ref.shape[0]

    @pl.when(j == 0)
    def _():
        u_sc[0:SUBLANES, :] = past_ref[...]

    @pl.when(j > 0)
    def _():
        u_sc[0:SUBLANES, :] = u_sc[tm:tm + SUBLANES, :]

    x = h_ref[...]
    xn = _rms(x, g_ref[...]).astype(BF16)
    z = jnp.dot(xn, win_ref[...], preferred_element_type=F32)
    gate_b = z[:, 0:D_MODEL]
    u = z[:, D_MODEL:2 * D_MODEL] * z[:, 2 * D_MODEL:]
    u_sc[SUBLANES:, :] = u
    conv = (cw_ref[0:1, :] * u_sc[SUBLANES - 2:SUBLANES - 2 + tm, :]
            + cw_ref[1:2, :] * u_sc[SUBLANES - 1:SUBLANES - 1 + tm, :]
            + cw_ref[2:3, :] * u)
    o_ref[...] = x + jnp.dot((gate_b * conv).astype(BF16), wout_ref[...], preferred_element_type=F32)

    @pl.when(j == pl.num_programs(1) - 1)
    def _():
        st_ref[...] = u_sc[tm:tm + SUBLANES, :]


def _odd(h, g, w_in, cw, w_out, past, *, tm):
    b, t, _ = h.shape
    tok = pl.BlockSpec((None, tm, D_MODEL), lambda i, j: (i, j, 0))
    st = pl.BlockSpec((None, SUBLANES, D_MODEL), lambda i, j: (i, 0, 0))
    return pl.pallas_call(
        _odd_kernel,
        grid=(b, t // tm),
        in_specs=[tok, pl.BlockSpec((1, D_MODEL), lambda i, j: (0, 0)),
                  pl.BlockSpec((D_MODEL, 3 * D_MODEL), lambda i, j: (0, 0)),
                  pl.BlockSpec((SUBLANES, D_MODEL), lambda i, j: (0, 0)),
                  pl.BlockSpec((D_MODEL, D_MODEL), lambda i, j: (0, 0)), st],
        out_specs=[tok, st],
        out_shape=[jax.ShapeDtypeStruct((b, t, D_MODEL), F32), jax.ShapeDtypeStruct((b, SUBLANES, D_MODEL), F32)],
        scratch_shapes=[pltpu.VMEM((tm + SUBLANES, D_MODEL), F32)],
        compiler_params=_cp(("arbitrary", "arbitrary"), 48),
        name="short_conv_mixer",
    )(h, g, w_in, cw, w_out, past)


def _top2(logits, lane_f):
    lg = jnp.where(lane_f < N_EXPERTS, logits, -jnp.inf)
    m1 = jnp.max(lg, axis=1, keepdims=True)
    i1 = jnp.min(jnp.where(lg == m1, lane_f, float(LANES)), axis=1, keepdims=True)
    lg2 = jnp.where(lane_f == i1, -jnp.inf, lg)
    m2 = jnp.max(lg2, axis=1, keepdims=True)
    i2 = jnp.min(jnp.where(lg2 == m2, lane_f, float(LANES)), axis=1, keepdims=True)
    e2 = jnp.exp(m2 - m1)
    inv = 1.0 / (1.0 + e2)
    return i1, i2, inv, e2 * inv


def _route(logits, lane_f):
    i1, i2, g1, g2 = _top2(logits, lane_f)
    return jnp.where(lane_f == i1, g1, 0.0) + jnp.where(lane_f == i2, g2, 0.0)


def _moe_kernel(h_ref, g_ref, rw_ref, wg_ref, wu_ref, wd_ref, fg_ref, o_ref, xn_sc, comb_sc):
    e = pl.program_id(1)
    j = pl.program_id(2)
    tm = h_ref.shape[0]
    lane = lax.broadcasted_iota(jnp.int32, (tm, LANES), 1)

    @pl.when((e == 0) & (j == 0))
    def _():
        x = h_ref[...]
        xn = _rms(x, g_ref[...]).astype(BF16)
        xn_sc[...] = xn
        logits = jnp.dot(xn, rw_ref[...], preferred_element_type=F32)
        comb_sc[...] = _route(logits, lane.astype(F32))
        o_ref[...] = x

    xn = xn_sc[...]
    a = _silu(jnp.dot(xn, wg_ref[...], preferred_element_type=F32)) * jnp.dot(xn, wu_ref[...], preferred_element_type=F32)
    y = jnp.dot(a.astype(BF16), wd_ref[...], preferred_element_type=F32)
    w_e = jnp.sum(jnp.where(lane == e, comb_sc[...], 0.0), axis=1, keepdims=True)
    o_ref[...] += y * w_e

    @pl.when((e == pl.num_programs(1) - 1) & (j == pl.num_programs(2) - 1))
    def _():
        o_ref[...] = _rms(o_ref[...], fg_ref[...])


def _moe(h, g, rw, wg, wu, wd, fg, *, tm):
    n = h.shape[0]
    tok = pl.BlockSpec((tm, D_MODEL), lambda i, e, j: (i, 0))
    vec = pl.BlockSpec((1, D_MODEL), lambda i, e, j: (0, 0))
    return pl.pallas_call(
        _moe_kernel,
        grid=(n // tm, N_EXPERTS, D_FF // FF_TILE),
        in_specs=[tok, vec, pl.BlockSpec((D_MODEL, LANES), lambda i, e, j: (0, 0)),
                  pl.BlockSpec((None, D_MODEL, FF_TILE), lambda i, e, j: (e, 0, j)),
                  pl.BlockSpec((None, D_MODEL, FF_TILE), lambda i, e, j: (e, 0, j)),
                  pl.BlockSpec((None, FF_TILE, D_MODEL), lambda i, e, j: (e, j, 0)), vec],
        out_specs=tok,
        out_shape=jax.ShapeDtypeStruct((n, D_MODEL), F32),
        scratch_shapes=[pltpu.VMEM((tm, D_MODEL), BF16), pltpu.VMEM((tm, LANES), F32)],
        compiler_params=_cp(("arbitrary", "arbitrary", "arbitrary"), 48),
        name="moe_swiglu",
    )(h, g, rw, wg, wu, wd, fg)


_R_E1, _R_E2, _R_G1, _R_G2, _R_P1, _R_P2 = range(6)


def _router_kernel(h_ref, g_ref, rw_ref, xn_ref, rt_ref, cnt_ref, carry_ref):
    @pl.when(pl.program_id(0) == 0)
    def _():
        carry_ref[...] = jnp.zeros_like(carry_ref)

    xn = _rms(h_ref[...], g_ref[...])
    xn_ref[...] = xn
    tm = xn.shape[0]
    lane = lax.broadcasted_iota(jnp.int32, (tm, LANES), 1)
    lane_f = lane.astype(F32)
    logits = jnp.dot(xn.astype(BF16), rw_ref[...], preferred_element_type=F32)
    i1, i2, g1, g2 = _top2(logits, lane_f)
    sel = jnp.where((lane_f == i1) | (lane_f == i2), 1.0, 0.0)
    incl = _cumsum_rows(sel)
    rank = incl - sel + carry_ref[...]
    p1 = jnp.sum(jnp.where(lane_f == i1, rank, 0.0), axis=1, keepdims=True)
    p2 = jnp.sum(jnp.where(lane_f == i2, rank, 0.0), axis=1, keepdims=True)
    carry_ref[...] = carry_ref[...] + incl[tm - 1:tm, :]
    cnt_ref[...] = carry_ref[...]
    rec = jnp.zeros((tm, LANES), F32)
    for ln, val in ((_R_E1, i1), (_R_E2, i2), (_R_G1, g1), (_R_G2, g2), (_R_P1, p1), (_R_P2, p2)):
        rec = jnp.where(lane == ln, val, rec)
    rt_ref[...] = rec


def _router(h, g, rw, *, tm):
    n = h.shape[0]
    return pl.pallas_call(
        _router_kernel,
        grid=(n // tm,),
        in_specs=[pl.BlockSpec((tm, D_MODEL), lambda i: (i, 0)), pl.BlockSpec((1, D_MODEL), lambda i: (0, 0)),
                  pl.BlockSpec((D_MODEL, LANES), lambda i: (0, 0))],
        out_specs=[pl.BlockSpec((tm, D_MODEL), lambda i: (i, 0)), pl.BlockSpec((tm, LANES), lambda i: (i, 0)),
                   pl.BlockSpec((1, LANES), lambda i: (0, 0))],
        out_shape=[jax.ShapeDtypeStruct((n, D_MODEL), F32), jax.ShapeDtypeStruct((n, LANES), F32),
                   jax.ShapeDtypeStruct((1, LANES), F32)],
        scratch_shapes=[pltpu.VMEM((1, LANES), F32)],
        compiler_params=_cp(("arbitrary",), 32),
        name="moe_router",
    )(h, g, rw)


def _row_copy(src, src_row, dst, dst_row, sem):
    return pltpu.make_async_copy(src.at[pl.ds(src_row, 1), :], dst.at[pl.ds(dst_row, 1), :], sem)


def _dispatch_kernel(ends_ref, dest_ref, x_ref, xs_ref, zero_sc, sem, *, tm_expert):
    tm = x_ref.shape[0]

    @pl.when(pl.program_id(0) == 0)
    def _():
        zero_sc[...] = jnp.zeros_like(zero_sc)

        def zero_tile(first_row, wanted):
            @pl.when(wanted)
            def _():
                cp = pltpu.make_async_copy(zero_sc, xs_ref.at[pl.ds(pl.multiple_of(first_row, tm_expert), tm_expert), :], sem)
                cp.start()
                cp.wait()

        for e in range(N_EXPERTS):
            begin = ends_ref[e - 1] if e else 0
            zero_tile(ends_ref[e] - tm_expert, ends_ref[e] > begin)
        for k in range(N_EXPERTS):
            first_row = ends_ref[N_EXPERTS - 1] + k * tm_expert
            zero_tile(first_row, first_row < xs_ref.shape[0])

    def start(t, c):
        _row_copy(x_ref, t, xs_ref, dest_ref[0, 2 * t], sem).start()
        _row_copy(x_ref, t, xs_ref, dest_ref[0, 2 * t + 1], sem).start()
        return c

    def wait(t, c):
        _row_copy(x_ref, t, xs_ref, dest_ref[0, 2 * t], sem).wait()
        _row_copy(x_ref, t, xs_ref, dest_ref[0, 2 * t + 1], sem).wait()
        return c

    lax.fori_loop(0, tm, start, 0, unroll=MOE_DMA_UNROLL)
    lax.fori_loop(0, tm, wait, 0, unroll=MOE_DMA_UNROLL)


def _dispatch(ends, dest, xn, *, rows, tm, tm_expert):
    n = xn.shape[0]
    return pl.pallas_call(
        functools.partial(_dispatch_kernel, tm_expert=tm_expert),
        grid_spec=pltpu.PrefetchScalarGridSpec(
            num_scalar_prefetch=1,
            grid=(n // tm,),
            in_specs=[pl.BlockSpec((None, 1, 2 * tm), lambda i, ends: (i, 0, 0), memory_space=pltpu.SMEM),
                      pl.BlockSpec((tm, D_MODEL), lambda i, ends: (i, 0))],
            out_specs=pl.BlockSpec(memory_space=pl.ANY),
            scratch_shapes=[pltpu.VMEM((tm_expert, D_MODEL), F32), pltpu.SemaphoreType.DMA(())]),
        out_shape=jax.ShapeDtypeStruct((rows, D_MODEL), F32),
        compiler_params=_cp(("arbitrary",), 32),
        name="moe_dispatch",
    )(ends, dest.reshape(n // tm, 1, 2 * tm), xn)


def _expert_kernel(te_ref, nv_ref, x_ref, wg_ref, wu_ref, wd_ref, o_ref, xb_sc):
    n_valid = nv_ref[pl.program_id(0)]

    @pl.when(pl.program_id(1) == 0)
    def _():
        xb_sc[...] = x_ref[...].astype(BF16)
        o_ref[...] = jnp.zeros_like(o_ref)

    @pl.when(n_valid > 0)
    def _():
        xb = xb_sc[...]
        a = _silu(jnp.dot(xb, wg_ref[...], preferred_element_type=F32)) * jnp.dot(xb, wu_ref[...], preferred_element_type=F32)
        o_ref[...] += jnp.dot(a.astype(BF16), wd_ref[...], preferred_element_type=F32)


def _experts(tile_expert, tile_valid, xs, wg, wu, wd, *, tm):
    rows = xs.shape[0]
    tok = pl.BlockSpec((tm, D_MODEL), lambda i, j, te, nv: (i, 0))
    return pl.pallas_call(
        _expert_kernel,
        grid_spec=pltpu.PrefetchScalarGridSpec(
            num_scalar_prefetch=2,
            grid=(rows // tm, D_FF // FF_TILE),
            in_specs=[tok,
                      pl.BlockSpec((None, D_MODEL, FF_TILE), lambda i, j, te, nv: (te[i], 0, j)),
                      pl.BlockSpec((None, D_MODEL, FF_TILE), lambda i, j, te, nv: (te[i], 0, j)),
                      pl.BlockSpec((None, FF_TILE, D_MODEL), lambda i, j, te, nv: (te[i], j, 0))],
            out_specs=tok,
            scratch_shapes=[pltpu.VMEM((tm, D_MODEL), BF16)]),
        out_shape=jax.ShapeDtypeStruct((rows, D_MODEL), F32),
        compiler_params=_cp(("arbitrary", "arbitrary"), 48),
        name="moe_experts",
    )(tile_expert, tile_valid, xs, wg, wu, wd)


def _combine_kernel(dest_ref, h_ref, rt_ref, fg_ref, ys_ref, o_ref, y1_sc, y2_sc, sem):
    tm = h_ref.shape[0]

    def start(t, c):
        _row_copy(ys_ref, dest_ref[0, 2 * t], y1_sc, t, sem).start()
        _row_copy(ys_ref, dest_ref[0, 2 * t + 1], y2_sc, t, sem).start()
        return c

    def wait(t, c):
        _row_copy(ys_ref, dest_ref[0, 2 * t], y1_sc, t, sem).wait()
        _row_copy(ys_ref, dest_ref[0, 2 * t + 1], y2_sc, t, sem).wait()
        return c

    lax.fori_loop(0, tm, start, 0, unroll=MOE_DMA_UNROLL)
    lax.fori_loop(0, tm, wait, 0, unroll=MOE_DMA_UNROLL)
    rt = rt_ref[...]
    moe = rt[:, _R_G1:_R_G1 + 1] * y1_sc[...] + rt[:, _R_G2:_R_G2 + 1] * y2_sc[...]
    o_ref[...] = _rms(h_ref[...] + moe, fg_ref[...])


def _combine(dest, h, rt, fg, ys, *, tm):
    n = h.shape[0]
    tok = pl.BlockSpec((tm, D_MODEL), lambda i: (i, 0))
    return pl.pallas_call(
        _combine_kernel,
        grid=(n // tm,),
        in_specs=[pl.BlockSpec((None, 1, 2 * tm), lambda i: (i, 0, 0), memory_space=pltpu.SMEM),
                  tok, pl.BlockSpec((tm, LANES), lambda i: (i, 0)), pl.BlockSpec((1, D_MODEL), lambda i: (0, 0)),
                  pl.BlockSpec(memory_space=pl.ANY)],
        out_specs=tok,
        out_shape=jax.ShapeDtypeStruct((n, D_MODEL), F32),
        scratch_shapes=[pltpu.VMEM((tm, D_MODEL), F32), pltpu.VMEM((tm, D_MODEL), F32), pltpu.SemaphoreType.DMA(())],
        compiler_params=_cp(("arbitrary",), 32),
        name="moe_combine",
    )(dest.reshape(n // tm, 1, 2 * tm), h, rt, fg, ys)


def _moe_routed(h, g, rw, wg, wu, wd, fg, *, tm, tm_expert):
    n = h.shape[0]
    xn, rt, counts = _router(h, g, rw, tm=tm)
    counts = counts[0, :N_EXPERTS].astype(jnp.int32)
    group = (counts + tm_expert - 1) // tm_expert * tm_expert
    ends = jnp.cumsum(group)
    starts = ends - group
    e = rt[:, _R_E1:_R_E2 + 1].astype(jnp.int32)
    dest = (starts[e] + rt[:, _R_P1:_R_P2 + 1].astype(jnp.int32)).reshape(2 * n)
    n_tiles = (TOP_K * n) // tm_expert + N_EXPERTS
    tile_start = jnp.arange(n_tiles, dtype=jnp.int32) * tm_expert
    active = tile_start < ends[-1]
    last_active = ends[-1] // tm_expert - 1
    tile_expert = jnp.searchsorted(ends, jnp.minimum(tile_start, last_active * tm_expert), side='right').astype(jnp.int32)
    tile_valid = jnp.where(active, jnp.clip(starts[tile_expert] + counts[tile_expert] - tile_start, 0, tm_expert), 0)
    xs = _dispatch(ends.astype(jnp.int32), dest, xn, rows=n_tiles * tm_expert, tm=tm, tm_expert=tm_expert)
    ys = _experts(tile_expert, tile_valid.astype(jnp.int32), xs, wg, wu, wd, tm=tm_expert)
    return _combine(dest, h, rt, fg, ys, tm=tm)


def _pack_params(p):
    fq, fk, fv, ff, mq, mk, mv, mo, mi, mf = jnp.split(p['even_w_in'][0], list(EVEN_SPLITS), axis=1)
    gate_w = jnp.concatenate([ff, mi, mf, jnp.zeros((D_MODEL, LANES - _G_END), F32)], axis=1)
    gate_b = jnp.concatenate([p['fox_b_f'][0], p['mlstm_b_i'][0], p['mlstm_b_f'][0], jnp.zeros((LANES - _G_END,), F32)])
    row = lambda a: a.reshape(1, -1).astype(F32)
    return dict(
        even_w=jnp.concatenate([fq, fk, fv, mq, mk, mv, mo, gate_w], axis=1).astype(BF16),
        even_b=gate_b.reshape(1, LANES),
        even_w_out=p['even_w_out'][0].astype(BF16),
        mlstm_norm_g=row(p['mlstm_norm_g'][0]),
        norm_mix_g=[row(p['norm_mix_g'][l]) for l in range(2)],
        norm_cross_g=[row(p['norm_cross_g'][l]) for l in range(2)],
        norm_ffn_g=[row(p['norm_ffn_g'][l]) for l in range(2)],
        final_norm_g=row(p['final_norm_g']),
        mem_wq=p['mem_wq'].astype(BF16), mem_wo=p['mem_wo'].astype(BF16),
        ffn_wg=p['ffn_w_gate'][0].astype(BF16), ffn_wu=p['ffn_w_up'][0].astype(BF16), ffn_wd=p['ffn_w_down'][0].astype(BF16),
        odd_w_in=p['odd_w_in'][0].astype(BF16), odd_w_out=p['odd_w_out'][0].astype(BF16),
        conv_w=jnp.concatenate([p['conv_w'][0], jnp.zeros((SUBLANES - CONV_W, D_MODEL), F32)], axis=0),
        router_w=jnp.concatenate([p['router_w'][0], jnp.zeros((D_MODEL, LANES - N_EXPERTS), F32)], axis=1).astype(BF16),
        moe_wg=p['moe_w_gate'][0].astype(BF16), moe_wu=p['moe_w_up'][0].astype(BF16), moe_wd=p['moe_w_down'][0].astype(BF16),
    )


def _trunk(x, mem_k, mem_v, caches, w, *, tm, tm_wide, chunk):
    b, t, _ = x.shape
    n = b * t
    flat = lambda a: a.reshape(n, a.shape[-1])
    per_b = lambda a: a.reshape(b, t, a.shape[-1])

    if caches is None:
        f0 = jnp.zeros((b, 1, LANES), F32)
    else:
        cache_k, cache_v, cache_lf, c0, n0, m0, conv_st = caches
        past_len = cache_k.shape[1]
        lf_pad = jnp.pad(cache_lf.astype(F32), ((0, 0), (0, 0), (0, LANES - FOX_HEADS)))
        ka_cache, f0 = _cache_prep(cache_k.reshape(b, past_len, FOX_W).astype(F32), lf_pad, tm=FOX_TILE)
    qa, ka, fk, fv, vb, mq, mk, mv, mo, gt = _even_in(flat(x), w['norm_mix_g'][0], w['even_w'], w['even_b'], f0,
                                                      tm=tm, tiles_per_batch=t // tm)
    if caches is None:
        fox_t = _fox_attention(per_b(qa), per_b(ka), per_b(vb), q0=0)
        c_ext0 = jnp.zeros((b, MLSTM_HEADS, MLSTM_EXT, LANES), F32)
        m_ext0 = jnp.zeros((b, SUBLANES, LANES), F32)
        past = jnp.zeros((b, SUBLANES, D_MODEL), F32)
    else:
        pad_t = lambda a: jnp.pad(per_b(a), ((0, 0), (0, FOX_TILE - t), (0, 0)))
        k_all = jnp.concatenate([ka_cache, pad_t(ka)], axis=1)
        v_all = jnp.concatenate([cache_v.reshape(b, past_len, FOX_W).astype(BF16), pad_t(vb)], axis=1)
        fox_full = _fox_attention(pad_t(qa), k_all, v_all, q0=past_len // FOX_TILE)
        fox_t = jnp.swapaxes(fox_full[:, :, :t], 1, 2)
        c_ext0 = jnp.concatenate([
            jnp.pad(c0.astype(F32), ((0, 0), (0, 0), (0, 0), (0, LANES - MLSTM_DK))),
            jnp.pad(n0.astype(F32)[:, :, None, :], ((0, 0), (0, 0), (0, MLSTM_EXT - MLSTM_DV - 1), (0, LANES - MLSTM_DK)))],
            axis=2)
        m_ext0 = jnp.broadcast_to(jnp.pad(m0.astype(F32), ((0, 0), (0, SUBLANES - MLSTM_HEADS)))[:, :, None],
                                  (b, SUBLANES, LANES))
        past = jnp.pad(conv_st.astype(F32), ((0, 0), (SUBLANES - (CONV_W - 1), 0), (0, 0)))
    hm, c_ext, m_ext = _mlstm(per_b(mq), per_b(mk), per_b(mv), per_b(mo), per_b(gt), c_ext0, m_ext0,
                              w['mlstm_norm_g'], chunk=chunk)
    h = _even_out(x, fox_t, hm, w['even_w_out'], tm=tm, fox_transposed=caches is None)
    h = _cross(h, w['norm_cross_g'][0], w['mem_wq'][0], w['mem_wo'][0], mem_k[0], mem_v[0], tm=tm)
    h = _ffn(flat(h), w['norm_ffn_g'][0], w['ffn_wg'], w['ffn_wu'], w['ffn_wd'], tm=tm_wide)

    h, conv_new = _odd(per_b(h), w['norm_mix_g'][1], w['odd_w_in'], w['conv_w'], w['odd_w_out'], past, tm=tm)
    h = _cross(h, w['norm_cross_g'][1], w['mem_wq'][1], w['mem_wo'][1], mem_k[1], mem_v[1], tm=tm)
    moe_args = (flat(h), w['norm_ffn_g'][1], w['router_w'], w['moe_wg'], w['moe_wu'], w['moe_wd'], w['final_norm_g'])
    if n * TOP_K >= N_EXPERTS * MOE_EXPERT_TILE:
        y = _moe_routed(*moe_args, tm=MOE_TOKEN_TILE, tm_expert=MOE_EXPERT_TILE)
    else:
        y = _moe(*moe_args, tm=tm_wide)

    states = (
        fk.reshape(1, b, t, FOX_HEADS, FOX_DH), fv.reshape(1, b, t, FOX_HEADS, FOX_DH),
        per_b(gt)[None, :, :, _G_FOX:_G_IG],
        c_ext[None, :, :, 0:MLSTM_DV, 0:MLSTM_DK], c_ext[None, :, :, MLSTM_DV, 0:MLSTM_DK], m_ext[None, :, 0:MLSTM_HEADS, 0],
        conv_new[None, :, SUBLANES - (CONV_W - 1):, :],
    )
    return per_b(y), states


def kernel(x_prompt, x_sample, mem_prompt, cache_fox_k, cache_fox_v, cache_fox_logf, state_mlstm_c, state_mlstm_n, state_mlstm_m, state_conv, cache_mem_k, cache_mem_v, norm_mix_g, norm_mem_g, norm_cross_g, norm_ffn_g, final_norm_g, even_w_in, fox_b_f, mlstm_b_i, mlstm_b_f, mlstm_norm_g, even_w_out, odd_w_in, conv_w, odd_w_out, mem_wq, mem_wk, mem_wv, mem_wo, ffn_w_gate, ffn_w_up, ffn_w_down, router_w, moe_w_gate, moe_w_up, moe_w_down):
    w = _pack_params(dict(
        norm_mix_g=norm_mix_g, norm_cross_g=norm_cross_g, norm_ffn_g=norm_ffn_g, final_norm_g=final_norm_g,
        even_w_in=even_w_in, fox_b_f=fox_b_f, mlstm_b_i=mlstm_b_i, mlstm_b_f=mlstm_b_f, mlstm_norm_g=mlstm_norm_g,
        even_w_out=even_w_out, odd_w_in=odd_w_in, conv_w=conv_w, odd_w_out=odd_w_out, mem_wq=mem_wq, mem_wo=mem_wo,
        ffn_w_gate=ffn_w_gate, ffn_w_up=ffn_w_up, ffn_w_down=ffn_w_down, router_w=router_w,
        moe_w_gate=moe_w_gate, moe_w_up=moe_w_up, moe_w_down=moe_w_down))

    bp, tp, _ = x_prompt.shape
    bs, ts, _ = x_sample.shape
    depth = norm_mem_g.shape[0]

    mem_k_p, mem_v_p, mem_kb, mem_vb = _memkv(mem_prompt.reshape(bp * MEM_TOKENS, D_MODEL),
                                              norm_mem_g.reshape(depth, 1, D_MODEL).astype(F32),
                                              mem_wk.astype(BF16), mem_wv.astype(BF16), tm=512)
    per_layer = lambda a, nb: a.reshape(depth, nb, MEM_TOKENS, MEM_W)
    y_prompt, st_p = _trunk(x_prompt, per_layer(mem_kb, bp), per_layer(mem_vb, bp), None, w,
                            tm=256, tm_wide=1024, chunk=MLSTM_CHUNK)

    caches = (cache_fox_k[0], cache_fox_v[0], cache_fox_logf[0], state_mlstm_c[0], state_mlstm_n[0],
              state_mlstm_m[0], state_conv[0])
    y_sample, st_s = _trunk(x_sample, per_layer(cache_mem_k.astype(BF16), bs), per_layer(cache_mem_v.astype(BF16), bs),
                            caches, w, tm=ts, tm_wide=bs * ts, chunk=ts)

    mem_shape = (depth, bp, MEM_TOKENS, MEM_HEADS, MEM_DH)
    return (y_prompt, y_sample) + st_p + (mem_k_p.reshape(mem_shape), mem_v_p.reshape(mem_shape)) + st_s
```

```python
import functools

import jax
import jax.numpy as jnp
from jax import lax
from jax.experimental import pallas as pl
from jax.experimental.pallas import tpu as pltpu

F32 = jnp.float32
BF16 = jnp.bfloat16

D_MODEL = 1024
EPS = 1e-6
FOX_HEADS = 8
FOX_DH = 64
FOX_W = FOX_HEADS * FOX_DH
FOX_SCALE = FOX_DH ** -0.5
MLSTM_HEADS = 4
MLSTM_DV = 128
MLSTM_DK = 64
MLSTM_W = MLSTM_HEADS * MLSTM_DV
MLSTM_QK_W = MLSTM_HEADS * MLSTM_DK
MLSTM_SCALE = MLSTM_DK ** -0.5
MLSTM_CHUNK = 256
MLSTM_EXT = 256
MEM_TOKENS = 256
MEM_HEADS = 4
MEM_DH = 128
MEM_W = MEM_HEADS * MEM_DH
MEM_SCALE = MEM_DH ** -0.5
D_FF = 3584
N_EXPERTS = 8
TOP_K = 2
MOE_EXPERT_TILE = 1024
MOE_TOKEN_TILE = 512
MOE_DMA_UNROLL = 8
CONV_W = 3
EVEN_SIZES = (FOX_W, FOX_W, FOX_W, FOX_HEADS, MLSTM_QK_W, MLSTM_QK_W, MLSTM_W, MLSTM_W, MLSTM_HEADS, MLSTM_HEADS)
EVEN_SPLITS = tuple(sum(EVEN_SIZES[:i + 1]) for i in range(len(EVEN_SIZES) - 1))

LANES = 128
SUBLANES = 8
FOX_AUG_W = FOX_HEADS * LANES
FOX_TILE = 256
FOX_LOOKAHEAD = 3
FOX_PV_DELAY = 2
FF_TILE = 512
_C_Q, _C_K, _C_V, _C_MQ, _C_MK, _C_MV, _C_MO, _C_G, _C_END = 0, 512, 1024, 1536, 1792, 2048, 2560, 3072, 3200
_G_FOX, _G_IG, _G_LF, _G_END = 0, 8, 12, 16
NEG = -1e30
LOG2E = 1.4426950408889634


def _cp(sem, vmem_mb):
    return pltpu.CompilerParams(dimension_semantics=sem, vmem_limit_bytes=vmem_mb * 1024 * 1024)


def _rms(x, g):
    return x * lax.rsqrt(jnp.mean(x * x, axis=-1, keepdims=True) + EPS) * g


def _log_sigmoid(x):
    return jnp.minimum(x, 0.0) - jnp.log1p(jnp.exp(-jnp.abs(x)))


def _sigmoid(x):
    return 1.0 / (1.0 + jnp.exp(-x))


def _cumsum_rows(x):
    n = x.shape[0]
    row = lax.broadcasted_iota(jnp.int32, x.shape, 0)
    s = 1
    while s < n:
        x = x + jnp.where(row >= s, pltpu.roll(x, s, axis=0), 0.0)
        s *= 2
    return x


def _split3(f):
    hi = f.astype(BF16).astype(F32)
    r = f - hi
    mid = r.astype(BF16).astype(F32)
    return hi, mid, r - mid


def _head_block(src, h, lane):
    p, odd = divmod(h, 2)
    blk = src[:, LANES * p:LANES * (p + 1)]
    return pltpu.roll(blk, FOX_DH, axis=1) if odd else blk


def _fox_aug(src, cum_f, h, lane, is_query):
    blk = _head_block(src, h, lane)
    hi, mid, lo = _split3(jnp.broadcast_to(cum_f[:, h:h + 1] * LOG2E, blk.shape))
    if is_query:
        aug = jnp.where(lane == 64, hi, jnp.where(lane == 65, mid, jnp.where(lane == 66, lo,
                        jnp.where(lane < 70, 1.0, 0.0))))
    else:
        aug = jnp.where(lane < 67, 1.0, jnp.where(lane == 67, -hi, jnp.where(lane == 68, -mid,
                        jnp.where(lane == 69, -lo, 0.0))))
    return jnp.where(lane < FOX_DH, blk, aug).astype(BF16)


def _even_in_kernel(x_ref, g_ref, w_ref, b_ref, f0_ref,
                    qa_ref, ka_ref, fk_ref, fv_ref, vb_ref, mq_ref, mk_ref, mv_ref, mo_ref, gt_ref,
                    carry_ref, *, tiles_per_batch):
    @pl.when(pl.program_id(0) % tiles_per_batch == 0)
    def _():
        carry_ref[...] = f0_ref[...]

    xn = _rms(x_ref[...], g_ref[...]).astype(BF16)
    acc = jnp.dot(xn, w_ref[...], preferred_element_type=F32)
    tm = acc.shape[0]
    lane = lax.broadcasted_iota(jnp.int32, (tm, LANES), 1)

    gates = acc[:, _C_G:_C_END] + b_ref[...]
    ls = _log_sigmoid(gates)
    is_ig = (lane >= _G_IG) & (lane < _G_LF)
    gt_ref[...] = jnp.where(is_ig, gates, jnp.where(lane < _G_END, ls, 0.0))
    cum_f = _cumsum_rows(jnp.where(lane < _G_IG, ls, 0.0)) + carry_ref[...]
    carry_ref[...] = cum_f[tm - 1:tm, :]

    q_all = acc[:, _C_Q:_C_K] * (FOX_SCALE * LOG2E)
    k_all = acc[:, _C_K:_C_V]
    v_all = acc[:, _C_V:_C_MQ]
    for h in range(FOX_HEADS):
        qa_ref[:, LANES * h:LANES * (h + 1)] = _fox_aug(q_all, cum_f, h, lane, True)
        ka_ref[:, LANES * h:LANES * (h + 1)] = _fox_aug(k_all, cum_f, h, lane, False)
    fk_ref[...] = k_all
    fv_ref[...] = v_all
    vb_ref[...] = v_all.astype(BF16)

    mq_all = acc[:, _C_MQ:_C_MK] * MLSTM_SCALE
    mk_all = acc[:, _C_MK:_C_MV]
    for h in range(MLSTM_HEADS):
        mq_ref[:, LANES * h:LANES * (h + 1)] = jnp.where(lane < MLSTM_DK, _head_block(mq_all, h, lane), 0.0).astype(BF16)
        mk_ref[:, LANES * h:LANES * (h + 1)] = jnp.where(lane < MLSTM_DK, _head_block(mk_all, h, lane), 0.0).astype(BF16)
    mv_ref[...] = acc[:, _C_MV:_C_MO].astype(BF16)
    mo_ref[...] = acc[:, _C_MO:_C_G].astype(BF16)


def _even_in(x, g, w, bias, f0, *, tm, tiles_per_batch):
    n = x.shape[0]
    row = lambda i: (i, 0)
    fixed = lambda i: (0, 0)
    widths = (FOX_AUG_W, FOX_AUG_W, FOX_W, FOX_W, FOX_W, MLSTM_W, MLSTM_W, MLSTM_W, MLSTM_W, LANES)
    dtypes = (BF16, BF16, F32, F32, BF16, BF16, BF16, BF16, BF16, F32)
    return pl.pallas_call(
        functools.partial(_even_in_kernel, tiles_per_batch=tiles_per_batch),
        grid=(n // tm,),
        in_specs=[pl.BlockSpec((tm, D_MODEL), row), pl.BlockSpec((1, D_MODEL), fixed),
                  pl.BlockSpec((D_MODEL, _C_END), fixed), pl.BlockSpec((1, LANES), fixed),
                  pl.BlockSpec((None, 1, LANES), lambda i: (i // tiles_per_batch, 0, 0))],
        out_specs=[pl.BlockSpec((tm, wd), row) for wd in widths],
        out_shape=[jax.ShapeDtypeStruct((n, wd), dt) for wd, dt in zip(widths, dtypes)],
        scratch_shapes=[pltpu.VMEM((1, LANES), F32)],
        compiler_params=_cp(("arbitrary",), 48),
        name="even_in",
    )(x, g, w, bias, f0)


def _cache_prep_kernel(k_ref, lf_ref, ka_ref, fend_ref, carry_ref):
    @pl.when(pl.program_id(1) == 0)
    def _():
        carry_ref[...] = jnp.zeros_like(carry_ref)

    k_all = k_ref[...]
    tm = k_all.shape[0]
    lane = lax.broadcasted_iota(jnp.int32, (tm, LANES), 1)
    cum_f = _cumsum_rows(lf_ref[...]) + carry_ref[...]
    carry_ref[...] = cum_f[tm - 1:tm, :]
    fend_ref[...] = cum_f[tm - 1:tm, :]
    for h in range(FOX_HEADS):
        ka_ref[:, LANES * h:LANES * (h + 1)] = _fox_aug(k_all, cum_f, h, lane, False)


def _cache_prep(cache_k, cache_lf, *, tm):
    b, p, _ = cache_k.shape
    return pl.pallas_call(
        _cache_prep_kernel,
        grid=(b, p // tm),
        in_specs=[pl.BlockSpec((None, tm, FOX_W), lambda i, j: (i, j, 0)),
                  pl.BlockSpec((None, tm, LANES), lambda i, j: (i, j, 0))],
        out_specs=[pl.BlockSpec((None, tm, FOX_AUG_W), lambda i, j: (i, j, 0)),
                   pl.BlockSpec((None, 1, LANES), lambda i, j: (i, 0, 0))],
        out_shape=[jax.ShapeDtypeStruct((b, p, FOX_AUG_W), BF16), jax.ShapeDtypeStruct((b, 1, LANES), F32)],
        scratch_shapes=[pltpu.VMEM((1, LANES), F32)],
        compiler_params=_cp(("arbitrary", "arbitrary"), 32),
        name="fox_cache_prep",
    )(cache_k, cache_lf)


def _fox_kernel(q_ref, k_ref, v_ref, o_ref, m_sc, l_sc, acc_sc, s_sc, *, q0):
    t = FOX_TILE
    n_full = q0 + pl.program_id(1)
    visible = (lax.broadcasted_iota(jnp.int32, (t, t), 0) <= lax.broadcasted_iota(jnp.int32, (t, t), 1))
    m_sc[...] = jnp.full_like(m_sc, NEG)
    l_sc[...] = jnp.zeros_like(l_sc)
    acc_sc[...] = jnp.zeros_like(acc_sc)

    def scores(j, h):
        hl = slice(LANES * h, LANES * (h + 1))
        return lax.dot_general(k_ref[pl.ds(pl.multiple_of(j * t, t), t), hl], q_ref[:, hl], (((1,), (1,)), ((), ())),
                               preferred_element_type=F32)

    def step(j, diagonal):
        off = pl.multiple_of(j * t, t)

        def weighted_values(h, p, alpha):
            pair, hh = divmod(h, 2)
            v_t = v_ref[pl.ds(off, t), LANES * pair:LANES * (pair + 1)].T
            pv = jnp.dot(v_t[FOX_DH * hh:FOX_DH * (hh + 1), :], p, preferred_element_type=F32)
            rows = slice(FOX_DH * h, FOX_DH * (h + 1))
            acc_sc[rows, :] = alpha * acc_sc[rows, :] + pv

        pending = {h: s_sc[h] for h in range(FOX_LOOKAHEAD)}
        ready = {}
        for h in range(FOX_HEADS):
            s = pending.pop(h)
            if diagonal:
                s = jnp.where(visible, s, NEG)
            m_old = m_sc[h:h + 1, :]
            m_new = jnp.maximum(m_old, jnp.max(s, axis=0, keepdims=True))
            alpha = jnp.exp2(m_old - m_new)
            p = jnp.exp2(s - m_new)
            m_sc[h:h + 1, :] = m_new
            l_sc[h:h + 1, :] = alpha * l_sc[h:h + 1, :] + jnp.sum(p, axis=0, keepdims=True)
            ready[h] = (p.astype(BF16), alpha)
            ahead = h + FOX_LOOKAHEAD
            if ahead < FOX_HEADS:
                pending[ahead] = scores(j, ahead)
            elif not diagonal:
                s_sc[ahead - FOX_HEADS] = scores(j + 1, ahead - FOX_HEADS)
            if h - FOX_PV_DELAY in ready:
                weighted_values(h - FOX_PV_DELAY, *ready.pop(h - FOX_PV_DELAY))
        for h in sorted(ready):
            weighted_values(h, *ready[h])

    def body(j, c):
        step(j, False)
        return c

    for h in range(FOX_LOOKAHEAD):
        s_sc[h] = scores(0, h)
    lax.fori_loop(0, n_full, body, 0)
    step(n_full, True)
    for h in range(FOX_HEADS):
        rows = slice(FOX_DH * h, FOX_DH * (h + 1))
        o_ref[rows, :] = (acc_sc[rows, :] / l_sc[h:h + 1, :]).astype(BF16)


def _fox_attention(q_aug, k_aug, v, *, q0):
    b, tq_total, _ = q_aug.shape
    t_kv = k_aug.shape[1]
    nq = tq_total // FOX_TILE
    return pl.pallas_call(
        functools.partial(_fox_kernel, q0=q0),
        grid=(b, nq),
        in_specs=[pl.BlockSpec((None, FOX_TILE, FOX_AUG_W), lambda bi, i: (bi, i, 0)),
                  pl.BlockSpec((None, t_kv, FOX_AUG_W), lambda bi, i: (bi, 0, 0)),
                  pl.BlockSpec((None, t_kv, FOX_W), lambda bi, i: (bi, 0, 0))],
        out_specs=pl.BlockSpec((None, FOX_W, FOX_TILE), lambda bi, i: (bi, 0, i)),
        out_shape=jax.ShapeDtypeStruct((b, FOX_W, tq_total), BF16),
        scratch_shapes=[pltpu.VMEM((FOX_HEADS, FOX_TILE), F32), pltpu.VMEM((FOX_HEADS, FOX_TILE), F32),
                        pltpu.VMEM((FOX_W, FOX_TILE), F32), pltpu.VMEM((FOX_LOOKAHEAD, FOX_TILE, FOX_TILE), F32)],
        compiler_params=_cp(("arbitrary", "arbitrary"), 48),
        name="fox_attention",
    )(q_aug, k_aug, v)


def _mlstm_kernel(mq_ref, mk_ref, mv_ref, mo_ref, gt_ref, c0_ref, m0_ref, ng_ref,
                  hm_ref, co_ref, mout_ref, c_sc, m_sc, gt_t, cs_t, *, chunk):
    step = pl.program_id(1)

    @pl.when(step == 0)
    def _():
        c_sc[...] = c0_ref[...]
        m_sc[...] = m0_ref[...]

    ln = chunk
    nt = (((1,), (1,)), ((), ()))
    causal = lax.broadcasted_iota(jnp.int32, (ln, ln), 0) >= lax.broadcasted_iota(jnp.int32, (ln, ln), 1)
    lane = lax.broadcasted_iota(jnp.int32, (ln, LANES), 1)
    heads = range(MLSTM_HEADS)
    hl = [slice(LANES * h, LANES * (h + 1)) for h in heads]

    g = gt_ref[...]
    cs = _cumsum_rows(g)
    if ln < LANES:
        pad = jnp.zeros((LANES - ln, LANES), F32)
        gt_t[...] = jnp.concatenate([g, pad], axis=0).T
        cs_t[...] = jnp.concatenate([cs, pad], axis=0).T
    else:
        gt_t[...] = g.T
        cs_t[...] = cs.T
    ig_col = [g[:, _G_IG + h:_G_IG + h + 1] for h in heads]
    b_col = [cs[:, _G_LF + h:_G_LF + h + 1] for h in heads]
    b_last = [b[ln - 1:ln, :] for b in b_col]

    qk = [lax.dot_general(mq_ref[:, hl[h]], mk_ref[:, hl[h]], nt, preferred_element_type=F32) for h in heads]
    m_loc, a_sum, av, g_max, upd = [], [], [], [], []
    for h in heads:
        d = jnp.where(causal, b_col[h] - cs_t[_G_LF + h:_G_LF + h + 1, 0:ln] + gt_t[_G_IG + h:_G_IG + h + 1, 0:ln],
                      -jnp.inf)
        m_loc.append(jnp.max(d, axis=1, keepdims=True))
        a = qk[h] * jnp.exp(d - m_loc[h])
        a_sum.append(jnp.sum(a, axis=1, keepdims=True))
        av.append(jnp.dot(a.astype(BF16), mv_ref[:, hl[h]], preferred_element_type=F32))
    for h in heads:
        g_col = b_last[h] - b_col[h] + ig_col[h]
        g_max.append(jnp.max(g_col, axis=0, keepdims=True))
        w_loc = jnp.exp(g_col - g_max[h])
        vw = jnp.concatenate([mv_ref[:, hl[h]].astype(F32) * w_loc, jnp.where(lane == 0, w_loc, 0.0)], axis=1)
        upd.append(lax.dot_general(vw.astype(BF16), mk_ref[:, hl[h]], (((0,), (0,)), ((), ())),
                                   preferred_element_type=F32))

    for h in heads:
        m_prev = m_sc[h:h + 1, 0:1]
        c_prev = c_sc[h]
        cq = lax.dot_general(mq_ref[:, hl[h]], c_prev.astype(BF16), nt, preferred_element_type=F32)
        inter = b_col[h] + m_prev
        m_t = jnp.maximum(inter, m_loc[h])
        w_inter = jnp.exp(inter - m_t)
        w_intra = jnp.exp(m_loc[h] - m_t)
        num = w_inter * cq[:, 0:MLSTM_DV] + w_intra * av[h]
        den = w_inter * cq[:, MLSTM_DV:MLSTM_DV + 1] + w_intra * a_sum[h]
        h_cell = num / jnp.maximum(jnp.abs(den), jnp.exp(-m_t))

        m_new = jnp.maximum(b_last[h] + m_prev, g_max[h])
        c_sc[h] = jnp.exp(b_last[h] + m_prev - m_new) * c_prev + jnp.exp(g_max[h] - m_new) * upd[h]
        m_sc[h:h + 1, :] = jnp.broadcast_to(m_new, (1, LANES))

        hn = h_cell * lax.rsqrt(jnp.mean(h_cell * h_cell, axis=1, keepdims=True) + EPS)
        hm_ref[:, hl[h]] = (hn * ng_ref[:, hl[h]] * _sigmoid(mo_ref[:, hl[h]].astype(F32))).astype(BF16)

    @pl.when(step == pl.num_programs(1) - 1)
    def _():
        co_ref[...] = c_sc[...]
        mout_ref[...] = m_sc[...]


def _mlstm(mq, mk, mv, mo, gt, c0, m0, ng, *, chunk):
    b, t, _ = mq.shape
    tok = lambda wd: pl.BlockSpec((None, chunk, wd), lambda i, j: (i, j, 0))
    c_spec = pl.BlockSpec((None, MLSTM_HEADS, MLSTM_EXT, LANES), lambda i, j: (i, 0, 0, 0))
    m_spec = pl.BlockSpec((None, SUBLANES, LANES), lambda i, j: (i, 0, 0))
    t_cols = max(chunk, LANES)
    return pl.pallas_call(
        functools.partial(_mlstm_kernel, chunk=chunk),
        grid=(b, t // chunk),
        in_specs=[tok(MLSTM_W), tok(MLSTM_W), tok(MLSTM_W), tok(MLSTM_W), tok(LANES), c_spec, m_spec,
                  pl.BlockSpec((1, MLSTM_W), lambda i, j: (0, 0))],
        out_specs=[tok(MLSTM_W), c_spec, m_spec],
        out_shape=[jax.ShapeDtypeStruct((b, t, MLSTM_W), BF16),
                   jax.ShapeDtypeStruct((b, MLSTM_HEADS, MLSTM_EXT, LANES), F32),
                   jax.ShapeDtypeStruct((b, SUBLANES, LANES), F32)],
        scratch_shapes=[pltpu.VMEM((MLSTM_HEADS, MLSTM_EXT, LANES), F32), pltpu.VMEM((SUBLANES, LANES), F32),
                        pltpu.VMEM((LANES, t_cols), F32), pltpu.VMEM((LANES, t_cols), F32)],
        compiler_params=_cp(("arbitrary", "arbitrary"), 32),
        name="mlstm",
    )(mq, mk, mv, mo, gt, c0, m0, ng)


def _even_out_kernel(x_ref, fox_ref, hm_ref, w_ref, o_ref, *, fox_transposed):
    dims = (((0,), (0,)), ((), ())) if fox_transposed else (((1,), (0,)), ((), ()))
    y = lax.dot_general(fox_ref[...], w_ref[0:FOX_W, :], dims, preferred_element_type=F32)
    y = y + jnp.dot(hm_ref[...], w_ref[FOX_W:, :], preferred_element_type=F32)
    o_ref[...] = x_ref[...] + y


def _even_out(x, fox, hm, w, *, tm, fox_transposed):
    b, t, _ = x.shape
    fox_spec = (pl.BlockSpec((None, FOX_W, tm), lambda i, j: (i, 0, j)) if fox_transposed
                else pl.BlockSpec((None, tm, FOX_W), lambda i, j: (i, j, 0)))
    return pl.pallas_call(
        functools.partial(_even_out_kernel, fox_transposed=fox_transposed),
        grid=(b, t // tm),
        in_specs=[pl.BlockSpec((None, tm, D_MODEL), lambda i, j: (i, j, 0)), fox_spec,
                  pl.BlockSpec((None, tm, MLSTM_W), lambda i, j: (i, j, 0)),
                  pl.BlockSpec((D_MODEL, D_MODEL), lambda i, j: (0, 0))],
        out_specs=pl.BlockSpec((None, tm, D_MODEL), lambda i, j: (i, j, 0)),
        out_shape=jax.ShapeDtypeStruct((b, t, D_MODEL), F32),
        compiler_params=_cp(("arbitrary", "arbitrary"), 32),
        name="even_out",
    )(x, fox, hm, w)


def _cross_kernel(h_ref, g_ref, wq_ref, wo_ref, mk_ref, mv_ref, o_ref):
    x = h_ref[...]
    xn = _rms(x, g_ref[...]).astype(BF16)
    q = jnp.dot(xn, wq_ref[...], preferred_element_type=F32).astype(BF16)
    hls = [slice(MEM_DH * h, MEM_DH * (h + 1)) for h in range(MEM_HEADS)]
    scores = [lax.dot_general(q[:, hl], mk_ref[:, hl], (((1,), (1,)), ((), ())), preferred_element_type=F32) * MEM_SCALE
              for hl in hls]
    outs = []
    for s, hl in zip(scores, hls):
        e = jnp.exp(s - jnp.max(s, axis=1, keepdims=True))
        p = e * (1.0 / jnp.sum(e, axis=1, keepdims=True))
        outs.append(jnp.dot(p.astype(BF16), mv_ref[:, hl], preferred_element_type=F32).astype(BF16))
    o = jnp.concatenate(outs, axis=1)
    o_ref[...] = x + jnp.dot(o, wo_ref[...], preferred_element_type=F32)


def _cross(h, g, wq, wo, mem_k, mem_v, *, tm):
    b, t, _ = h.shape
    tok = pl.BlockSpec((None, tm, D_MODEL), lambda i, j: (i, j, 0))
    mem = pl.BlockSpec((None, MEM_TOKENS, MEM_W), lambda i, j: (i, 0, 0))
    return pl.pallas_call(
        _cross_kernel,
        grid=(b, t // tm),
        in_specs=[tok, pl.BlockSpec((1, D_MODEL), lambda i, j: (0, 0)),
                  pl.BlockSpec((D_MODEL, MEM_W), lambda i, j: (0, 0)),
                  pl.BlockSpec((MEM_W, D_MODEL), lambda i, j: (0, 0)), mem, mem],
        out_specs=tok,
        out_shape=jax.ShapeDtypeStruct((b, t, D_MODEL), F32),
        compiler_params=_cp(("arbitrary", "arbitrary"), 32),
        name="cross_attention",
    )(h, g, wq, wo, mem_k, mem_v)


def _memkv_kernel(mem_ref, g_ref, wk_ref, wv_ref, k_ref, v_ref, kb_ref, vb_ref):
    mn = _rms(mem_ref[...], g_ref[...]).astype(BF16)
    k = jnp.dot(mn, wk_ref[...], preferred_element_type=F32)
    v = jnp.dot(mn, wv_ref[...], preferred_element_type=F32)
    k_ref[...] = k
    v_ref[...] = v
    kb_ref[...] = k.astype(BF16)
    vb_ref[...] = v.astype(BF16)


def _memkv(mem, g, wk, wv, *, tm):
    n = mem.shape[0]
    depth = g.shape[0]
    w_spec = pl.BlockSpec((None, D_MODEL, MEM_W), lambda l, i: (l, 0, 0))
    o_spec = pl.BlockSpec((None, tm, MEM_W), lambda l, i: (l, i, 0))
    return pl.pallas_call(
        _memkv_kernel,
        grid=(depth, n // tm),
        in_specs=[pl.BlockSpec((tm, D_MODEL), lambda l, i: (i, 0)),
                  pl.BlockSpec((None, 1, D_MODEL), lambda l, i: (l, 0, 0)), w_spec, w_spec],
        out_specs=[o_spec] * 4,
        out_shape=[jax.ShapeDtypeStruct((depth, n, MEM_W), dt) for dt in (F32, F32, BF16, BF16)],
        compiler_params=_cp(("arbitrary", "arbitrary"), 32),
        name="memory_kv",
    )(mem, g, wk, wv)


def _silu(x):
    return x * _sigmoid(x)


def _ffn_kernel(h_ref, g_ref, wg_ref, wu_ref, wd_ref, o_ref, xn_sc):
    @pl.when(pl.program_id(1) == 0)
    def _():
        x = h_ref[...]
        xn_sc[...] = _rms(x, g_ref[...]).astype(BF16)
        o_ref[...] = x

    xn = xn_sc[...]
    a = _silu(jnp.dot(xn, wg_ref[...], preferred_element_type=F32)) * jnp.dot(xn, wu_ref[...], preferred_element_type=F32)
    o_ref[...] += jnp.dot(a.astype(BF16), wd_ref[...], preferred_element_type=F32)


def _ffn(h, g, wg, wu, wd, *, tm):
    n = h.shape[0]
    tok = pl.BlockSpec((tm, D_MODEL), lambda i, j: (i, 0))
    return pl.pallas_call(
        _ffn_kernel,
        grid=(n // tm, D_FF // FF_TILE),
        in_specs=[tok, pl.BlockSpec((1, D_MODEL), lambda i, j: (0, 0)),
                  pl.BlockSpec((D_MODEL, FF_TILE), lambda i, j: (0, j)),
                  pl.BlockSpec((D_MODEL, FF_TILE), lambda i, j: (0, j)),
                  pl.BlockSpec((FF_TILE, D_MODEL), lambda i, j: (j, 0))],
        out_specs=tok,
        out_shape=jax.ShapeDtypeStruct((n, D_MODEL), F32),
        scratch_shapes=[pltpu.VMEM((tm, D_MODEL), BF16)],
        compiler_params=_cp(("arbitrary", "arbitrary"), 48),
        name="dense_swiglu",
    )(h, g, wg, wu, wd)


def _odd_kernel(h_ref, g_ref, win_ref, cw_ref, wout_ref, past_ref, o_ref, st_ref, u_sc):
    j = pl.program_id(1)
    tm = h_ref.shape[0]

    @pl.when(j == 0)
    def _():
        u_sc[0:SUBLANES, :] = past_ref[...]

    @pl.when(j > 0)
    def _():
        u_sc[0:SUBLANES, :] = u_sc[tm:tm + SUBLANES, :]

    x = h_ref[...]
    xn = _rms(x, g_ref[...]).astype(BF16)
    z = jnp.dot(xn, win_ref[...], preferred_element_type=F32)
    gate_b = z[:, 0:D_MODEL]
    u = z[:, D_MODEL:2 * D_MODEL] * z[:, 2 * D_MODEL:]
    u_sc[SUBLANES:, :] = u
    conv = (cw_ref[0:1, :] * u_sc[SUBLANES - 2:SUBLANES - 2 + tm, :]
            + cw_ref[1:2, :] * u_sc[SUBLANES - 1:SUBLANES - 1 + tm, :]
            + cw_ref[2:3, :] * u)
    o_ref[...] = x + jnp.dot((gate_b * conv).astype(BF16), wout_ref[...], preferred_element_type=F32)

    @pl.when(j == pl.num_programs(1) - 1)
    def _():
        st_ref[...] = u_sc[tm:tm + SUBLANES, :]


def _odd(h, g, w_in, cw, w_out, past, *, tm):
    b, t, _ = h.shape
    tok = pl.BlockSpec((None, tm, D_MODEL), lambda i, j: (i, j, 0))
    st = pl.BlockSpec((None, SUBLANES, D_MODEL), lambda i, j: (i, 0, 0))
    return pl.pallas_call(
        _odd_kernel,
        grid=(b, t // tm),
        in_specs=[tok, pl.BlockSpec((1, D_MODEL), lambda i, j: (0, 0)),
                  pl.BlockSpec((D_MODEL, 3 * D_MODEL), lambda i, j: (0, 0)),
                  pl.BlockSpec((SUBLANES, D_MODEL), lambda i, j: (0, 0)),
                  pl.BlockSpec((D_MODEL, D_MODEL), lambda i, j: (0, 0)), st],
        out_specs=[tok, st],
        out_shape=[jax.ShapeDtypeStruct((b, t, D_MODEL), F32), jax.ShapeDtypeStruct((b, SUBLANES, D_MODEL), F32)],
        scratch_shapes=[pltpu.VMEM((tm + SUBLANES, D_MODEL), F32)],
        compiler_params=_cp(("arbitrary", "arbitrary"), 48),
        name="short_conv_mixer",
    )(h, g, w_in, cw, w_out, past)


def _top2(logits, lane_f):
    lg = jnp.where(lane_f < N_EXPERTS, logits, -jnp.inf)
    m1 = jnp.max(lg, axis=1, keepdims=True)
    i1 = jnp.min(jnp.where(lg == m1, lane_f, float(LANES)), axis=1, keepdims=True)
    lg2 = jnp.where(lane_f == i1, -jnp.inf, lg)
    m2 = jnp.max(lg2, axis=1, keepdims=True)
    i2 = jnp.min(jnp.where(lg2 == m2, lane_f, float(LANES)), axis=1, keepdims=True)
    e2 = jnp.exp(m2 - m1)
    inv = 1.0 / (1.0 + e2)
    return i1, i2, inv, e2 * inv


def _route(logits, lane_f):
    i1, i2, g1, g2 = _top2(logits, lane_f)
    return jnp.where(lane_f == i1, g1, 0.0) + jnp.where(lane_f == i2, g2, 0.0)


def _moe_kernel(h_ref, g_ref, rw_ref, wg_ref, wu_ref, wd_ref, fg_ref, o_ref, xn_sc, comb_sc):
    e = pl.program_id(1)
    j = pl.program_id(2)
    tm = h_ref.shape[0]
    lane = lax.broadcasted_iota(jnp.int32, (tm, LANES), 1)

    @pl.when((e == 0) & (j == 0))
    def _():
        x = h_ref[...]
        xn = _rms(x, g_ref[...]).astype(BF16)
        xn_sc[...] = xn
        logits = jnp.dot(xn, rw_ref[...], preferred_element_type=F32)
        comb_sc[...] = _route(logits, lane.astype(F32))
        o_ref[...] = x

    xn = xn_sc[...]
    a = _silu(jnp.dot(xn, wg_ref[...], preferred_element_type=F32)) * jnp.dot(xn, wu_ref[...], preferred_element_type=F32)
    y = jnp.dot(a.astype(BF16), wd_ref[...], preferred_element_type=F32)
    w_e = jnp.sum(jnp.where(lane == e, comb_sc[...], 0.0), axis=1, keepdims=True)
    o_ref[...] += y * w_e

    @pl.when((e == pl.num_programs(1) - 1) & (j == pl.num_programs(2) - 1))
    def _():
        o_ref[...] = _rms(o_ref[...], fg_ref[...])


def _moe(h, g, rw, wg, wu, wd, fg, *, tm):
    n = h.shape[0]
    tok = pl.BlockSpec((tm, D_MODEL), lambda i, e, j: (i, 0))
    vec = pl.BlockSpec((1, D_MODEL), lambda i, e, j: (0, 0))
    return pl.pallas_call(
        _moe_kernel,
        grid=(n // tm, N_EXPERTS, D_FF // FF_TILE),
        in_specs=[tok, vec, pl.BlockSpec((D_MODEL, LANES), lambda i, e, j: (0, 0)),
                  pl.BlockSpec((None, D_MODEL, FF_TILE), lambda i, e, j: (e, 0, j)),
                  pl.BlockSpec((None, D_MODEL, FF_TILE), lambda i, e, j: (e, 0, j)),
                  pl.BlockSpec((None, FF_TILE, D_MODEL), lambda i, e, j: (e, j, 0)), vec],
        out_specs=tok,
        out_shape=jax.ShapeDtypeStruct((n, D_MODEL), F32),
        scratch_shapes=[pltpu.VMEM((tm, D_MODEL), BF16), pltpu.VMEM((tm, LANES), F32)],
        compiler_params=_cp(("arbitrary", "arbitrary", "arbitrary"), 48),
        name="moe_swiglu",
    )(h, g, rw, wg, wu, wd, fg)


_R_E1, _R_E2, _R_G1, _R_G2, _R_P1, _R_P2 = range(6)


def _router_kernel(h_ref, g_ref, rw_ref, xn_ref, rt_ref, cnt_ref, carry_ref):
    @pl.when(pl.program_id(0) == 0)
    def _():
        carry_ref[...] = jnp.zeros_like(carry_ref)

    xn = _rms(h_ref[...], g_ref[...])
    xn_ref[...] = xn
    tm = xn.shape[0]
    lane = lax.broadcasted_iota(jnp.int32, (tm, LANES), 1)
    lane_f = lane.astype(F32)
    logits = jnp.dot(xn.astype(BF16), rw_ref[...], preferred_element_type=F32)
    i1, i2, g1, g2 = _top2(logits, lane_f)
    sel = jnp.where((lane_f == i1) | (lane_f == i2), 1.0, 0.0)
    incl = _cumsum_rows(sel)
    rank = incl - sel + carry_ref[...]
    p1 = jnp.sum(jnp.where(lane_f == i1, rank, 0.0), axis=1, keepdims=True)
    p2 = jnp.sum(jnp.where(lane_f == i2, rank, 0.0), axis=1, keepdims=True)
    carry_ref[...] = carry_ref[...] + incl[tm - 1:tm, :]
    cnt_ref[...] = carry_ref[...]
    rec = jnp.zeros((tm, LANES), F32)
    for ln, val in ((_R_E1, i1), (_R_E2, i2), (_R_G1, g1), (_R_G2, g2), (_R_P1, p1), (_R_P2, p2)):
        rec = jnp.where(lane == ln, val, rec)
    rt_ref[...] = rec


def _router(h, g, rw, *, tm):
    n = h.shape[0]
    return pl.pallas_call(
        _router_kernel,
        grid=(n // tm,),
        in_specs=[pl.BlockSpec((tm, D_MODEL), lambda i: (i, 0)), pl.BlockSpec((1, D_MODEL), lambda i: (0, 0)),
                  pl.BlockSpec((D_MODEL, LANES), lambda i: (0, 0))],
        out_specs=[pl.BlockSpec((tm, D_MODEL), lambda i: (i, 0)), pl.BlockSpec((tm, LANES), lambda i: (i, 0)),
                   pl.BlockSpec((1, LANES), lambda i: (0, 0))],
        out_shape=[jax.ShapeDtypeStruct((n, D_MODEL), F32), jax.ShapeDtypeStruct((n, LANES), F32),
                   jax.ShapeDtypeStruct((1, LANES), F32)],
        scratch_shapes=[pltpu.VMEM((1, LANES), F32)],
        compiler_params=_cp(("arbitrary",), 32),
        name="moe_router",
    )(h, g, rw)


def _row_copy(src, src_row, dst, dst_row, sem):
    return pltpu.make_async_copy(src.at[pl.ds(src_row, 1), :], dst.at[pl.ds(dst_row, 1), :], sem)


def _dispatch_kernel(ends_ref, dest_ref, x_ref, xs_ref, zero_sc, sem, *, tm_expert):
    tm = x_ref.shape[0]

    @pl.when(pl.program_id(0) == 0)
    def _():
        zero_sc[...] = jnp.zeros_like(zero_sc)

        def zero_tile(first_row, wanted):
            @pl.when(wanted)
            def _():
                cp = pltpu.make_async_copy(zero_sc, xs_ref.at[pl.ds(pl.multiple_of(first_row, tm_expert), tm_expert), :], sem)
                cp.start()
                cp.wait()

        for e in range(N_EXPERTS):
            begin = ends_ref[e - 1] if e else 0
            zero_tile(ends_ref[e] - tm_expert, ends_ref[e] > begin)
        for k in range(N_EXPERTS):
            first_row = ends_ref[N_EXPERTS - 1] + k * tm_expert
            zero_tile(first_row, first_row < xs_ref.shape[0])

    def start(t, c):
        _row_copy(x_ref, t, xs_ref, dest_ref[0, 2 * t], sem).start()
        _row_copy(x_ref, t, xs_ref, dest_ref[0, 2 * t + 1], sem).start()
        return c

    def wait(t, c):
        _row_copy(x_ref, t, xs_ref, dest_ref[0, 2 * t], sem).wait()
        _row_copy(x_ref, t, xs_ref, dest_ref[0, 2 * t + 1], sem).wait()
        return c

    lax.fori_loop(0, tm, start, 0, unroll=MOE_DMA_UNROLL)
    lax.fori_loop(0, tm, wait, 0, unroll=MOE_DMA_UNROLL)


def _dispatch(ends, dest, xn, *, rows, tm, tm_expert):
    n = xn.shape[0]
    return pl.pallas_call(
        functools.partial(_dispatch_kernel, tm_expert=tm_expert),
        grid_spec=pltpu.PrefetchScalarGridSpec(
            num_scalar_prefetch=1,
            grid=(n // tm,),
            in_specs=[pl.BlockSpec((None, 1, 2 * tm), lambda i, ends: (i, 0, 0), memory_space=pltpu.SMEM),
                      pl.BlockSpec((tm, D_MODEL), lambda i, ends: (i, 0))],
            out_specs=pl.BlockSpec(memory_space=pl.ANY),
            scratch_shapes=[pltpu.VMEM((tm_expert, D_MODEL), F32), pltpu.SemaphoreType.DMA(())]),
        out_shape=jax.ShapeDtypeStruct((rows, D_MODEL), F32),
        compiler_params=_cp(("arbitrary",), 32),
        name="moe_dispatch",
    )(ends, dest.reshape(n // tm, 1, 2 * tm), xn)


def _expert_kernel(te_ref, nv_ref, x_ref, wg_ref, wu_ref, wd_ref, o_ref, xb_sc):
    n_valid = nv_ref[pl.program_id(0)]

    @pl.when(pl.program_id(1) == 0)
    def _():
        xb_sc[...] = x_ref[...].astype(BF16)
        o_ref[...] = jnp.zeros_like(o_ref)

    @pl.when(n_valid > 0)
    def _():
        xb = xb_sc[...]
        a = _silu(jnp.dot(xb, wg_ref[...], preferred_element_type=F32)) * jnp.dot(xb, wu_ref[...], preferred_element_type=F32)
        o_ref[...] += jnp.dot(a.astype(BF16), wd_ref[...], preferred_element_type=F32)


def _experts(tile_expert, tile_valid, xs, wg, wu, wd, *, tm):
    rows = xs.shape[0]
    tok = pl.BlockSpec((tm, D_MODEL), lambda i, j, te, nv: (i, 0))
    return pl.pallas_call(
        _expert_kernel,
        grid_spec=pltpu.PrefetchScalarGridSpec(
            num_scalar_prefetch=2,
            grid=(rows // tm, D_FF // FF_TILE),
            in_specs=[tok,
                      pl.BlockSpec((None, D_MODEL, FF_TILE), lambda i, j, te, nv: (te[i], 0, j)),
                      pl.BlockSpec((None, D_MODEL, FF_TILE), lambda i, j, te, nv: (te[i], 0, j)),
                      pl.BlockSpec((None, FF_TILE, D_MODEL), lambda i, j, te, nv: (te[i], j, 0))],
            out_specs=tok,
            scratch_shapes=[pltpu.VMEM((tm, D_MODEL), BF16)]),
        out_shape=jax.ShapeDtypeStruct((rows, D_MODEL), F32),
        compiler_params=_cp(("arbitrary", "arbitrary"), 48),
        name="moe_experts",
    )(tile_expert, tile_valid, xs, wg, wu, wd)


def _combine_kernel(dest_ref, h_ref, rt_ref, fg_ref, ys_ref, o_ref, y1_sc, y2_sc, sem):
    tm = h_ref.shape[0]

    def start(t, c):
        _row_copy(ys_ref, dest_ref[0, 2 * t], y1_sc, t, sem).start()
        _row_copy(ys_ref, dest_ref[0, 2 * t + 1], y2_sc, t, sem).start()
        return c

    def wait(t, c):
        _row_copy(ys_ref, dest_ref[0, 2 * t], y1_sc, t, sem).wait()
        _row_copy(ys_ref, dest_ref[0, 2 * t + 1], y2_sc, t, sem).wait()
        return c

    lax.fori_loop(0, tm, start, 0, unroll=MOE_DMA_UNROLL)
    lax.fori_loop(0, tm, wait, 0, unroll=MOE_DMA_UNROLL)
    rt = rt_ref[...]
    moe = rt[:, _R_G1:_R_G1 + 1] * y1_sc[...] + rt[:, _R_G2:_R_G2 + 1] * y2_sc[...]
    o_ref[...] = _rms(h_ref[...] + moe, fg_ref[...])


def _combine(dest, h, rt, fg, ys, *, tm):
    n = h.shape[0]
    tok = pl.BlockSpec((tm, D_MODEL), lambda i: (i, 0))
    return pl.pallas_call(
        _combine_kernel,
        grid=(n // tm,),
        in_specs=[pl.BlockSpec((None, 1, 2 * tm), lambda i: (i, 0, 0), memory_space=pltpu.SMEM),
                  tok, pl.BlockSpec((tm, LANES), lambda i: (i, 0)), pl.BlockSpec((1, D_MODEL), lambda i: (0, 0)),
                  pl.BlockSpec(memory_space=pl.ANY)],
        out_specs=tok,
        out_shape=jax.ShapeDtypeStruct((n, D_MODEL), F32),
        scratch_shapes=[pltpu.VMEM((tm, D_MODEL), F32), pltpu.VMEM((tm, D_MODEL), F32), pltpu.SemaphoreType.DMA(())],
        compiler_params=_cp(("arbitrary",), 32),
        name="moe_combine",
    )(dest.reshape(n // tm, 1, 2 * tm), h, rt, fg, ys)


def _moe_routed(h, g, rw, wg, wu, wd, fg, *, tm, tm_expert):
    n = h.shape[0]
    xn, rt, counts = _router(h, g, rw, tm=tm)
    counts = counts[0, :N_EXPERTS].astype(jnp.int32)
    group = (counts + tm_expert - 1) // tm_expert * tm_expert
    ends = jnp.cumsum(group)
    starts = ends - group
    e = rt[:, _R_E1:_R_E2 + 1].astype(jnp.int32)
    dest = (starts[e] + rt[:, _R_P1:_R_P2 + 1].astype(jnp.int32)).reshape(2 * n)
    n_tiles = (TOP_K * n) // tm_expert + N_EXPERTS
    tile_start = jnp.arange(n_tiles, dtype=jnp.int32) * tm_expert
    active = tile_start < ends[-1]
    last_active = ends[-1] // tm_expert - 1
    probe = jnp.minimum(tile_start, last_active * tm_expert)
    tile_expert = jnp.sum((probe[:, None] >= ends[None, :]).astype(jnp.int32), axis=1)
    tile_valid = jnp.where(active, jnp.clip(starts[tile_expert] + counts[tile_expert] - tile_start, 0, tm_expert), 0)
    xs = _dispatch(ends.astype(jnp.int32), dest, xn, rows=n_tiles * tm_expert, tm=tm, tm_expert=tm_expert)
    ys = _experts(tile_expert, tile_valid.astype(jnp.int32), xs, wg, wu, wd, tm=tm_expert)
    return _combine(dest, h, rt, fg, ys, tm=tm)


def _pack_params(p):
    fq, fk, fv, ff, mq, mk, mv, mo, mi, mf = jnp.split(p['even_w_in'][0], list(EVEN_SPLITS), axis=1)
    gate_w = jnp.concatenate([ff, mi, mf, jnp.zeros((D_MODEL, LANES - _G_END), F32)], axis=1)
    gate_b = jnp.concatenate([p['fox_b_f'][0], p['mlstm_b_i'][0], p['mlstm_b_f'][0], jnp.zeros((LANES - _G_END,), F32)])
    row = lambda a: a.reshape(1, -1).astype(F32)
    return dict(
        even_w=jnp.concatenate([fq, fk, fv, mq, mk, mv, mo, gate_w], axis=1).astype(BF16),
        even_b=gate_b.reshape(1, LANES),
        even_w_out=p['even_w_out'][0].astype(BF16),
        mlstm_norm_g=row(p['mlstm_norm_g'][0]),
        norm_mix_g=[row(p['norm_mix_g'][l]) for l in range(2)],
        norm_cross_g=[row(p['norm_cross_g'][l]) for l in range(2)],
        norm_ffn_g=[row(p['norm_ffn_g'][l]) for l in range(2)],
        final_norm_g=row(p['final_norm_g']),
        mem_wq=p['mem_wq'].astype(BF16), mem_wo=p['mem_wo'].astype(BF16),
        ffn_wg=p['ffn_w_gate'][0].astype(BF16), ffn_wu=p['ffn_w_up'][0].astype(BF16), ffn_wd=p['ffn_w_down'][0].astype(BF16),
        odd_w_in=p['odd_w_in'][0].astype(BF16), odd_w_out=p['odd_w_out'][0].astype(BF16),
        conv_w=jnp.concatenate([p['conv_w'][0], jnp.zeros((SUBLANES - CONV_W, D_MODEL), F32)], axis=0),
        router_w=jnp.concatenate([p['router_w'][0], jnp.zeros((D_MODEL, LANES - N_EXPERTS), F32)], axis=1).astype(BF16),
        moe_wg=p['moe_w_gate'][0].astype(BF16), moe_wu=p['moe_w_up'][0].astype(BF16), moe_wd=p['moe_w_down'][0].astype(BF16),
    )


def _trunk(x, mem_k, mem_v, caches, w, *, tm, tm_wide, chunk):
    b, t, _ = x.shape
    n = b * t
    flat = lambda a: a.reshape(n, a.shape[-1])
    per_b = lambda a: a.reshape(b, t, a.shape[-1])

    if caches is None:
        f0 = jnp.zeros((b, 1, LANES), F32)
    else:
        cache_k, cache_v, cache_lf, c0, n0, m0, conv_st = caches
        past_len = cache_k.shape[1]
        lf_pad = jnp.pad(cache_lf.astype(F32), ((0, 0), (0, 0), (0, LANES - FOX_HEADS)))
        ka_cache, f0 = _cache_prep(cache_k.reshape(b, past_len, FOX_W).astype(F32), lf_pad, tm=FOX_TILE)
    qa, ka, fk, fv, vb, mq, mk, mv, mo, gt = _even_in(flat(x), w['norm_mix_g'][0], w['even_w'], w['even_b'], f0,
                                                      tm=tm, tiles_per_batch=t // tm)
    if caches is None:
        fox_t = _fox_attention(per_b(qa), per_b(ka), per_b(vb), q0=0)
        c_ext0 = jnp.zeros((b, MLSTM_HEADS, MLSTM_EXT, LANES), F32)
        m_ext0 = jnp.zeros((b, SUBLANES, LANES), F32)
        past = jnp.zeros((b, SUBLANES, D_MODEL), F32)
    else:
        pad_t = lambda a: jnp.pad(per_b(a), ((0, 0), (0, FOX_TILE - t), (0, 0)))
        k_all = jnp.concatenate([ka_cache, pad_t(ka)], axis=1)
        v_all = jnp.concatenate([cache_v.reshape(b, past_len, FOX_W).astype(BF16), pad_t(vb)], axis=1)
        fox_full = _fox_attention(pad_t(qa), k_all, v_all, q0=past_len // FOX_TILE)
        fox_t = jnp.swapaxes(fox_full[:, :, :t], 1, 2)
        c_ext0 = jnp.concatenate([
            jnp.pad(c0.astype(F32), ((0, 0), (0, 0), (0, 0), (0, LANES - MLSTM_DK))),
            jnp.pad(n0.astype(F32)[:, :, None, :], ((0, 0), (0, 0), (0, MLSTM_EXT - MLSTM_DV - 1), (0, LANES - MLSTM_DK)))],
            axis=2)
        m_ext0 = jnp.broadcast_to(jnp.pad(m0.astype(F32), ((0, 0), (0, SUBLANES - MLSTM_HEADS)))[:, :, None],
                                  (b, SUBLANES, LANES))
        past = jnp.pad(conv_st.astype(F32), ((0, 0), (SUBLANES - (CONV_W - 1), 0), (0, 0)))
    hm, c_ext, m_ext = _mlstm(per_b(mq), per_b(mk), per_b(mv), per_b(mo), per_b(gt), c_ext0, m_ext0,
                              w['mlstm_norm_g'], chunk=chunk)
    h = _even_out(x, fox_t, hm, w['even_w_out'], tm=tm, fox_transposed=caches is None)
    h = _cross(h, w['norm_cross_g'][0], w['mem_wq'][0], w['mem_wo'][0], mem_k[0], mem_v[0], tm=tm)
    h = _ffn(flat(h), w['norm_ffn_g'][0], w['ffn_wg'], w['ffn_wu'], w['ffn_wd'], tm=tm_wide)

    h, conv_new = _odd(per_b(h), w['norm_mix_g'][1], w['odd_w_in'], w['conv_w'], w['odd_w_out'], past, tm=tm)
    h = _cross(h, w['norm_cross_g'][1], w['mem_wq'][1], w['mem_wo'][1], mem_k[1], mem_v[1], tm=tm)
    moe_args = (flat(h), w['norm_ffn_g'][1], w['router_w'], w['moe_wg'], w['moe_wu'], w['moe_wd'], w['final_norm_g'])
    if n * TOP_K >= N_EXPERTS * MOE_EXPERT_TILE:
        y = _moe_routed(*moe_args, tm=MOE_TOKEN_TILE, tm_expert=MOE_EXPERT_TILE)
    else:
        y = _moe(*moe_args, tm=tm_wide)

    states = (
        fk.reshape(1, b, t, FOX_HEADS, FOX_DH), fv.reshape(1, b, t, FOX_HEADS, FOX_DH),
        per_b(gt)[None, :, :, _G_FOX:_G_IG],
        c_ext[None, :, :, 0:MLSTM_DV, 0:MLSTM_DK], c_ext[None, :, :, MLSTM_DV, 0:MLSTM_DK], m_ext[None, :, 0:MLSTM_HEADS, 0],
        conv_new[None, :, SUBLANES - (CONV_W - 1):, :],
    )
    return per_b(y), states


def kernel(x_prompt, x_sample, mem_prompt, cache_fox_k, cache_fox_v, cache_fox_logf, state_mlstm_c, state_mlstm_n, state_mlstm_m, state_conv, cache_mem_k, cache_mem_v, norm_mix_g, norm_mem_g, norm_cross_g, norm_ffn_g, final_norm_g, even_w_in, fox_b_f, mlstm_b_i, mlstm_b_f, mlstm_norm_g, even_w_out, odd_w_in, conv_w, odd_w_out, mem_wq, mem_wk, mem_wv, mem_wo, ffn_w_gate, ffn_w_up, ffn_w_down, router_w, moe_w_gate, moe_w_up, moe_w_down):
    w = _pack_params(dict(
        norm_mix_g=norm_mix_g, norm_cross_g=norm_cross_g, norm_ffn_g=norm_ffn_g, final_norm_g=final_norm_g,
        even_w_in=even_w_in, fox_b_f=fox_b_f, mlstm_b_i=mlstm_b_i, mlstm_b_f=mlstm_b_f, mlstm_norm_g=mlstm_norm_g,
        even_w_out=even_w_out, odd_w_in=odd_w_in, conv_w=conv_w, odd_w_out=odd_w_out, mem_wq=mem_wq, mem_wo=mem_wo,
        ffn_w_gate=ffn_w_gate, ffn_w_up=ffn_w_up, ffn_w_down=ffn_w_down, router_w=router_w,
        moe_w_gate=moe_w_gate, moe_w_up=moe_w_up, moe_w_down=moe_w_down))

    bp, tp, _ = x_prompt.shape
    bs, ts, _ = x_sample.shape
    depth = norm_mem_g.shape[0]

    mem_k_p, mem_v_p, mem_kb, mem_vb = _memkv(mem_prompt.reshape(bp * MEM_TOKENS, D_MODEL),
                                              norm_mem_g.reshape(depth, 1, D_MODEL).astype(F32),
                                              mem_wk.astype(BF16), mem_wv.astype(BF16), tm=512)
    per_layer = lambda a, nb: a.reshape(depth, nb, MEM_TOKENS, MEM_W)
    y_prompt, st_p = _trunk(x_prompt, per_layer(mem_kb, bp), per_layer(mem_vb, bp), None, w,
                            tm=512, tm_wide=1024, chunk=MLSTM_CHUNK)

    caches = (cache_fox_k[0], cache_fox_v[0], cache_fox_logf[0], state_mlstm_c[0], state_mlstm_n[0],
              state_mlstm_m[0], state_conv[0])
    y_sample, st_s = _trunk(x_sample, per_layer(cache_mem_k.astype(BF16), bs), per_layer(cache_mem_v.astype(BF16), bs),
                            caches, w, tm=ts, tm_wide=bs * ts, chunk=ts)

    mem_shape = (depth, bp, MEM_TOKENS, MEM_HEADS, MEM_DH)
    return (y_prompt, y_sample) + st_p + (mem_k_p.reshape(mem_shape), mem_v_p.reshape(mem_shape)) + st_s
```

```python
import functools

import jax
import jax.numpy as jnp
from jax import lax
from jax.experimental import pallas as pl
from jax.experimental.pallas import tpu as pltpu

F32 = jnp.float32
BF16 = jnp.bfloat16

D_MODEL = 1024
EPS = 1e-6
FOX_HEADS = 8
FOX_DH = 64
FOX_W = FOX_HEADS * FOX_DH
FOX_SCALE = FOX_DH ** -0.5
MLSTM_HEADS = 4
MLSTM_DV = 128
MLSTM_DK = 64
MLSTM_W = MLSTM_HEADS * MLSTM_DV
MLSTM_QK_W = MLSTM_HEADS * MLSTM_DK
MLSTM_SCALE = MLSTM_DK ** -0.5
MLSTM_CHUNK = 256
MLSTM_EXT = 256
MEM_TOKENS = 256
MEM_HEADS = 4
MEM_DH = 128
MEM_W = MEM_HEADS * MEM_DH
MEM_SCALE = MEM_DH ** -0.5
D_FF = 3584
N_EXPERTS = 8
TOP_K = 2
MOE_EXPERT_TILE = 1024
MOE_TOKEN_TILE = 512
MOE_DMA_UNROLL = 8
CONV_W = 3
EVEN_SIZES = (FOX_W, FOX_W, FOX_W, FOX_HEADS, MLSTM_QK_W, MLSTM_QK_W, MLSTM_W, MLSTM_W, MLSTM_HEADS, MLSTM_HEADS)
EVEN_SPLITS = tuple(sum(EVEN_SIZES[:i + 1]) for i in range(len(EVEN_SIZES) - 1))

LANES = 128
SUBLANES = 8
FOX_AUG_W = FOX_HEADS * LANES
FOX_TILE = 256
FOX_LOOKAHEAD = 3
FOX_PV_DELAY = 2
FF_TILE = 512
FF_SUB = 256
_C_Q, _C_K, _C_V, _C_MQ, _C_MK, _C_MV, _C_MO, _C_G, _C_END = 0, 512, 1024, 1536, 1792, 2048, 2560, 3072, 3200
_G_FOX, _G_IG, _G_LF, _G_END = 0, 8, 12, 16
NEG = -1e30
LOG2E = 1.4426950408889634


def _cp(sem, vmem_mb):
    return pltpu.CompilerParams(dimension_semantics=sem, vmem_limit_bytes=vmem_mb * 1024 * 1024)


def _rms(x, g):
    return x * lax.rsqrt(jnp.mean(x * x, axis=-1, keepdims=True) + EPS) * g


def _log_sigmoid(x):
    return jnp.minimum(x, 0.0) - jnp.log1p(jnp.exp(-jnp.abs(x)))


def _sigmoid(x):
    return 1.0 / (1.0 + jnp.exp(-x))


def _cumsum_rows(x):
    n = x.shape[0]
    row = lax.broadcasted_iota(jnp.int32, x.shape, 0)
    s = 1
    while s < n:
        x = x + jnp.where(row >= s, pltpu.roll(x, s, axis=0), 0.0)
        s *= 2
    return x


def _split3(f):
    hi = f.astype(BF16).astype(F32)
    r = f - hi
    mid = r.astype(BF16).astype(F32)
    return hi, mid, r - mid


def _head_block(src, h, lane):
    p, odd = divmod(h, 2)
    blk = src[:, LANES * p:LANES * (p + 1)]
    return pltpu.roll(blk, FOX_DH, axis=1) if odd else blk


def _fox_aug(src, cum_f, h, lane, is_query):
    blk = _head_block(src, h, lane)
    hi, mid, lo = _split3(jnp.broadcast_to(cum_f[:, h:h + 1] * LOG2E, blk.shape))
    if is_query:
        aug = jnp.where(lane == 64, hi, jnp.where(lane == 65, mid, jnp.where(lane == 66, lo,
                        jnp.where(lane < 70, 1.0, 0.0))))
    else:
        aug = jnp.where(lane < 67, 1.0, jnp.where(lane == 67, -hi, jnp.where(lane == 68, -mid,
                        jnp.where(lane == 69, -lo, 0.0))))
    return jnp.where(lane < FOX_DH, blk, aug).astype(BF16)


def _even_in_kernel(x_ref, g_ref, w_ref, b_ref, f0_ref,
                    qa_ref, ka_ref, fk_ref, fv_ref, vb_ref, mq_ref, mk_ref, mv_ref, mo_ref, gt_ref,
                    carry_ref, *, tiles_per_batch):
    @pl.when(pl.program_id(0) % tiles_per_batch == 0)
    def _():
        carry_ref[...] = f0_ref[...]

    xn = _rms(x_ref[...], g_ref[...]).astype(BF16)
    acc = jnp.dot(xn, w_ref[...], preferred_element_type=F32)
    tm = acc.shape[0]
    lane = lax.broadcasted_iota(jnp.int32, (tm, LANES), 1)

    gates = acc[:, _C_G:_C_END] + b_ref[...]
    ls = _log_sigmoid(gates)
    is_ig = (lane >= _G_IG) & (lane < _G_LF)
    gt_ref[...] = jnp.where(is_ig, gates, jnp.where(lane < _G_END, ls, 0.0))
    cum_f = _cumsum_rows(jnp.where(lane < _G_IG, ls, 0.0)) + carry_ref[...]
    carry_ref[...] = cum_f[tm - 1:tm, :]

    q_all = acc[:, _C_Q:_C_K] * (FOX_SCALE * LOG2E)
    k_all = acc[:, _C_K:_C_V]
    v_all = acc[:, _C_V:_C_MQ]
    for h in range(FOX_HEADS):
        qa_ref[:, LANES * h:LANES * (h + 1)] = _fox_aug(q_all, cum_f, h, lane, True)
        ka_ref[:, LANES * h:LANES * (h + 1)] = _fox_aug(k_all, cum_f, h, lane, False)
    for h in range(FOX_HEADS):
        fk_ref[:, h, :] = _head_block(k_all, h, lane)[:, 0:FOX_DH]
        fv_ref[:, h, :] = _head_block(v_all, h, lane)[:, 0:FOX_DH]
    vb_ref[...] = v_all.astype(BF16)

    mq_all = acc[:, _C_MQ:_C_MK] * MLSTM_SCALE
    mk_all = acc[:, _C_MK:_C_MV]
    for h in range(MLSTM_HEADS):
        mq_ref[:, LANES * h:LANES * (h + 1)] = jnp.where(lane < MLSTM_DK, _head_block(mq_all, h, lane), 0.0).astype(BF16)
        mk_ref[:, LANES * h:LANES * (h + 1)] = jnp.where(lane < MLSTM_DK, _head_block(mk_all, h, lane), 0.0).astype(BF16)
    mv_ref[...] = acc[:, _C_MV:_C_MO].astype(BF16)
    mo_ref[...] = acc[:, _C_MO:_C_G].astype(BF16)


def _even_in(x, g, w, bias, f0, *, tm, tiles_per_batch):
    n = x.shape[0]
    row = lambda i: (i, 0)
    fixed = lambda i: (0, 0)
    head_major = (FOX_HEADS, FOX_DH)
    widths = (FOX_AUG_W, FOX_AUG_W, head_major, head_major, FOX_W, MLSTM_W, MLSTM_W, MLSTM_W, MLSTM_W, LANES)
    dtypes = (BF16, BF16, F32, F32, BF16, BF16, BF16, BF16, BF16, F32)
    spec = lambda wd: (pl.BlockSpec((tm,) + wd, lambda i: (i, 0, 0)) if isinstance(wd, tuple) else pl.BlockSpec((tm, wd), row))
    shape = lambda wd: (n,) + wd if isinstance(wd, tuple) else (n, wd)
    return pl.pallas_call(
        functools.partial(_even_in_kernel, tiles_per_batch=tiles_per_batch),
        grid=(n // tm,),
        in_specs=[pl.BlockSpec((tm, D_MODEL), row), pl.BlockSpec((1, D_MODEL), fixed),
                  pl.BlockSpec((D_MODEL, _C_END), fixed), pl.BlockSpec((1, LANES), fixed),
                  pl.BlockSpec((None, 1, LANES), lambda i: (i // tiles_per_batch, 0, 0))],
        out_specs=[spec(wd) for wd in widths],
        out_shape=[jax.ShapeDtypeStruct(shape(wd), dt) for wd, dt in zip(widths, dtypes)],
        scratch_shapes=[pltpu.VMEM((1, LANES), F32)],
        compiler_params=_cp(("arbitrary",), 48),
        name="even_in",
    )(x, g, w, bias, f0)


def _cache_prep_kernel(k_ref, lf_ref, ka_ref, fend_ref, carry_ref):
    @pl.when(pl.program_id(1) == 0)
    def _():
        carry_ref[...] = jnp.zeros_like(carry_ref)

    k_all = k_ref[...]
    tm = k_all.shape[0]
    lane = lax.broadcasted_iota(jnp.int32, (tm, LANES), 1)
    cum_f = _cumsum_rows(lf_ref[...]) + carry_ref[...]
    carry_ref[...] = cum_f[tm - 1:tm, :]
    fend_ref[...] = cum_f[tm - 1:tm, :]
    for h in range(FOX_HEADS):
        ka_ref[:, LANES * h:LANES * (h + 1)] = _fox_aug(k_all, cum_f, h, lane, False)


def _cache_prep(cache_k, cache_lf, *, tm):
    b, p, _ = cache_k.shape
    return pl.pallas_call(
        _cache_prep_kernel,
        grid=(b, p // tm),
        in_specs=[pl.BlockSpec((None, tm, FOX_W), lambda i, j: (i, j, 0)),
                  pl.BlockSpec((None, tm, LANES), lambda i, j: (i, j, 0))],
        out_specs=[pl.BlockSpec((None, tm, FOX_AUG_W), lambda i, j: (i, j, 0)),
                   pl.BlockSpec((None, 1, LANES), lambda i, j: (i, 0, 0))],
        out_shape=[jax.ShapeDtypeStruct((b, p, FOX_AUG_W), BF16), jax.ShapeDtypeStruct((b, 1, LANES), F32)],
        scratch_shapes=[pltpu.VMEM((1, LANES), F32)],
        compiler_params=_cp(("arbitrary", "arbitrary"), 32),
        name="fox_cache_prep",
    )(cache_k, cache_lf)


def _fox_kernel(q_ref, k_ref, v_ref, o_ref, m_sc, l_sc, acc_sc, s_sc, *, q0):
    t = FOX_TILE
    n_full = q0 + pl.program_id(1)
    visible = (lax.broadcasted_iota(jnp.int32, (t, t), 0) <= lax.broadcasted_iota(jnp.int32, (t, t), 1))
    m_sc[...] = jnp.full_like(m_sc, NEG)
    l_sc[...] = jnp.zeros_like(l_sc)
    acc_sc[...] = jnp.zeros_like(acc_sc)

    def scores(j, h):
        hl = slice(LANES * h, LANES * (h + 1))
        return lax.dot_general(k_ref[pl.ds(pl.multiple_of(j * t, t), t), hl], q_ref[:, hl], (((1,), (1,)), ((), ())),
                               preferred_element_type=F32)

    def step(j, diagonal):
        off = pl.multiple_of(j * t, t)

        def weighted_values(h, p, alpha):
            pair, hh = divmod(h, 2)
            v_t = v_ref[pl.ds(off, t), LANES * pair:LANES * (pair + 1)].T
            pv = jnp.dot(v_t[FOX_DH * hh:FOX_DH * (hh + 1), :], p, preferred_element_type=F32)
            rows = slice(FOX_DH * h, FOX_DH * (h + 1))
            acc_sc[rows, :] = alpha * acc_sc[rows, :] + pv

        pending = {h: s_sc[h] for h in range(FOX_LOOKAHEAD)}
        ready = {}
        for h in range(FOX_HEADS):
            s = pending.pop(h)
            if diagonal:
                s = jnp.where(visible, s, NEG)
            m_old = m_sc[h:h + 1, :]
            m_new = jnp.maximum(m_old, jnp.max(s, axis=0, keepdims=True))
            alpha = jnp.exp2(m_old - m_new)
            p = jnp.exp2(s - m_new)
            m_sc[h:h + 1, :] = m_new
            l_sc[h:h + 1, :] = alpha * l_sc[h:h + 1, :] + jnp.sum(p, axis=0, keepdims=True)
            ready[h] = (p.astype(BF16), alpha)
            ahead = h + FOX_LOOKAHEAD
            if ahead < FOX_HEADS:
                pending[ahead] = scores(j, ahead)
            elif not diagonal:
                s_sc[ahead - FOX_HEADS] = scores(j + 1, ahead - FOX_HEADS)
            if h - FOX_PV_DELAY in ready:
                weighted_values(h - FOX_PV_DELAY, *ready.pop(h - FOX_PV_DELAY))
        for h in sorted(ready):
            weighted_values(h, *ready[h])

    def body(j, c):
        step(j, False)
        return c

    for h in range(FOX_LOOKAHEAD):
        s_sc[h] = scores(0, h)
    lax.fori_loop(0, n_full, body, 0)
    step(n_full, True)
    for h in range(FOX_HEADS):
        rows = slice(FOX_DH * h, FOX_DH * (h + 1))
        o_ref[rows, :] = (acc_sc[rows, :] / l_sc[h:h + 1, :]).astype(BF16)


def _fox_attention(q_aug, k_aug, v, *, q0):
    b, tq_total, _ = q_aug.shape
    t_kv = k_aug.shape[1]
    nq = tq_total // FOX_TILE
    return pl.pallas_call(
        functools.partial(_fox_kernel, q0=q0),
        grid=(b, nq),
        in_specs=[pl.BlockSpec((None, FOX_TILE, FOX_AUG_W), lambda bi, i: (bi, i, 0)),
                  pl.BlockSpec((None, t_kv, FOX_AUG_W), lambda bi, i: (bi, 0, 0)),
                  pl.BlockSpec((None, t_kv, FOX_W), lambda bi, i: (bi, 0, 0))],
        out_specs=pl.BlockSpec((None, FOX_W, FOX_TILE), lambda bi, i: (bi, 0, i)),
        out_shape=jax.ShapeDtypeStruct((b, FOX_W, tq_total), BF16),
        scratch_shapes=[pltpu.VMEM((FOX_HEADS, FOX_TILE), F32), pltpu.VMEM((FOX_HEADS, FOX_TILE), F32),
                        pltpu.VMEM((FOX_W, FOX_TILE), F32), pltpu.VMEM((FOX_LOOKAHEAD, FOX_TILE, FOX_TILE), F32)],
        compiler_params=_cp(("arbitrary", "arbitrary"), 48),
        name="fox_attention",
    )(q_aug, k_aug, v)


def _mlstm_kernel(mq_ref, mk_ref, mv_ref, mo_ref, gt_ref, c0_ref, m0_ref, ng_ref,
                  hm_ref, co_ref, mout_ref, c_sc, m_sc, gt_t, cs_t, *, chunk):
    step = pl.program_id(1)

    @pl.when(step == 0)
    def _():
        c_sc[...] = c0_ref[...]
        m_sc[...] = m0_ref[...]

    ln = chunk
    nt = (((1,), (1,)), ((), ()))
    causal = lax.broadcasted_iota(jnp.int32, (ln, ln), 0) >= lax.broadcasted_iota(jnp.int32, (ln, ln), 1)
    heads = range(MLSTM_HEADS)
    hl = [slice(LANES * h, LANES * (h + 1)) for h in heads]

    g = gt_ref[...]
    cs = _cumsum_rows(g)
    if ln < LANES:
        pad = jnp.zeros((LANES - ln, LANES), F32)
        gt_t[...] = jnp.concatenate([g, pad], axis=0).T
        cs_t[...] = jnp.concatenate([cs, pad], axis=0).T
    else:
        gt_t[...] = g.T
        cs_t[...] = cs.T
    rep = lambda col: jnp.broadcast_to(col, (ln, LANES))
    wide = (lambda r: jnp.concatenate([r] * (ln // LANES), axis=1)) if ln >= LANES else (lambda r: r[:, 0:ln])
    ig = [rep(g[:, _G_IG + h:_G_IG + h + 1]) for h in heads]
    b = [rep(cs[:, _G_LF + h:_G_LF + h + 1]) for h in heads]
    b_last = [x[ln - 1:ln, :] for x in b]

    qk = [lax.dot_general(mq_ref[:, hl[h]], mk_ref[:, hl[h]], nt, preferred_element_type=F32) for h in heads]
    m_loc, a_sum, av, g_max, upd = [], [], [], [], []
    for h in heads:
        d = jnp.where(causal, wide(b[h]) - cs_t[_G_LF + h:_G_LF + h + 1, 0:ln] + gt_t[_G_IG + h:_G_IG + h + 1, 0:ln],
                      -jnp.inf)
        m_loc.append(rep(jnp.max(d, axis=1, keepdims=True)))
        a = qk[h] * jnp.exp(d - wide(m_loc[h]))
        a_sum.append(rep(jnp.sum(a, axis=1, keepdims=True)))
        av.append(jnp.dot(a.astype(BF16), mv_ref[:, hl[h]], preferred_element_type=F32))
    for h in heads:
        g_tok = b_last[h] - b[h] + ig[h]
        g_max.append(jnp.max(g_tok, axis=0, keepdims=True))
        w_loc = jnp.exp(g_tok - g_max[h])
        vw = jnp.concatenate([mv_ref[:, hl[h]].astype(F32) * w_loc, w_loc], axis=1)
        upd.append(lax.dot_general(vw.astype(BF16), mk_ref[:, hl[h]], (((0,), (0,)), ((), ())),
                                   preferred_element_type=F32))

    for h in heads:
        m_prev = m_sc[h:h + 1, :]
        c_prev = c_sc[h]
        cq = lax.dot_general(mq_ref[:, hl[h]], c_prev.astype(BF16), nt, preferred_element_type=F32)
        inter = b[h] + m_prev
        m_t = jnp.maximum(inter, m_loc[h])
        w_inter = jnp.exp(inter - m_t)
        w_intra = jnp.exp(m_loc[h] - m_t)
        num = w_inter * cq[:, 0:MLSTM_DV] + w_intra * av[h]
        den = w_inter * cq[:, MLSTM_DV:] + w_intra * a_sum[h]
        h_cell = num / jnp.maximum(jnp.abs(den), jnp.exp(-m_t))

        m_new = jnp.maximum(b_last[h] + m_prev, g_max[h])
        c_sc[h] = jnp.exp(b_last[h] + m_prev - m_new) * c_prev + jnp.exp(g_max[h] - m_new) * upd[h]
        m_sc[h:h + 1, :] = m_new

        hn = h_cell * lax.rsqrt(jnp.mean(h_cell * h_cell, axis=1, keepdims=True) + EPS)
        hm_ref[:, hl[h]] = (hn * ng_ref[:, hl[h]] * _sigmoid(mo_ref[:, hl[h]].astype(F32))).astype(BF16)

    @pl.when(step == pl.num_programs(1) - 1)
    def _():
        co_ref[...] = c_sc[...]
        mout_ref[...] = m_sc[...]


def _mlstm(mq, mk, mv, mo, gt, c0, m0, ng, *, chunk):
    b, t, _ = mq.shape
    tok = lambda wd: pl.BlockSpec((None, chunk, wd), lambda i, j: (i, j, 0))
    c_spec = pl.BlockSpec((None, MLSTM_HEADS, MLSTM_EXT, LANES), lambda i, j: (i, 0, 0, 0))
    m_spec = pl.BlockSpec((None, SUBLANES, LANES), lambda i, j: (i, 0, 0))
    t_cols = max(chunk, LANES)
    return pl.pallas_call(
        functools.partial(_mlstm_kernel, chunk=chunk),
        grid=(b, t // chunk),
        in_specs=[tok(MLSTM_W), tok(MLSTM_W), tok(MLSTM_W), tok(MLSTM_W), tok(LANES), c_spec, m_spec,
                  pl.BlockSpec((1, MLSTM_W), lambda i, j: (0, 0))],
        out_specs=[tok(MLSTM_W), c_spec, m_spec],
        out_shape=[jax.ShapeDtypeStruct((b, t, MLSTM_W), BF16),
                   jax.ShapeDtypeStruct((b, MLSTM_HEADS, MLSTM_EXT, LANES), F32),
                   jax.ShapeDtypeStruct((b, SUBLANES, LANES), F32)],
        scratch_shapes=[pltpu.VMEM((MLSTM_HEADS, MLSTM_EXT, LANES), F32), pltpu.VMEM((SUBLANES, LANES), F32),
                        pltpu.VMEM((LANES, t_cols), F32), pltpu.VMEM((LANES, t_cols), F32)],
        compiler_params=_cp(("arbitrary", "arbitrary"), 32),
        name="mlstm",
    )(mq, mk, mv, mo, gt, c0, m0, ng)


def _even_out_kernel(x_ref, fox_ref, hm_ref, w_ref, o_ref, *, fox_transposed):
    dims = (((0,), (0,)), ((), ())) if fox_transposed else (((1,), (0,)), ((), ()))
    y = lax.dot_general(fox_ref[...], w_ref[0:FOX_W, :], dims, preferred_element_type=F32)
    y = y + jnp.dot(hm_ref[...], w_ref[FOX_W:, :], preferred_element_type=F32)
    o_ref[...] = x_ref[...] + y


def _even_out(x, fox, hm, w, *, tm, fox_transposed):
    b, t, _ = x.shape
    fox_spec = (pl.BlockSpec((None, FOX_W, tm), lambda i, j: (i, 0, j)) if fox_transposed
                else pl.BlockSpec((None, tm, FOX_W), lambda i, j: (i, j, 0)))
    return pl.pallas_call(
        functools.partial(_even_out_kernel, fox_transposed=fox_transposed),
        grid=(b, t // tm),
        in_specs=[pl.BlockSpec((None, tm, D_MODEL), lambda i, j: (i, j, 0)), fox_spec,
                  pl.BlockSpec((None, tm, MLSTM_W), lambda i, j: (i, j, 0)),
                  pl.BlockSpec((D_MODEL, D_MODEL), lambda i, j: (0, 0))],
        out_specs=pl.BlockSpec((None, tm, D_MODEL), lambda i, j: (i, j, 0)),
        out_shape=jax.ShapeDtypeStruct((b, t, D_MODEL), F32),
        compiler_params=_cp(("arbitrary", "arbitrary"), 32),
        name="even_out",
    )(x, fox, hm, w)


def _cross_kernel(h_ref, g_ref, wq_ref, wo_ref, mk_ref, mv_ref, o_ref):
    x = h_ref[...]
    xn = _rms(x, g_ref[...]).astype(BF16)
    q = jnp.dot(xn, wq_ref[...], preferred_element_type=F32).astype(BF16)
    hls = [slice(MEM_DH * h, MEM_DH * (h + 1)) for h in range(MEM_HEADS)]
    scores = [lax.dot_general(q[:, hl], mk_ref[:, hl], (((1,), (1,)), ((), ())), preferred_element_type=F32) * MEM_SCALE
              for hl in hls]
    outs = []
    for s, hl in zip(scores, hls):
        e = jnp.exp(s - jnp.max(s, axis=1, keepdims=True))
        p = e * (1.0 / jnp.sum(e, axis=1, keepdims=True))
        outs.append(jnp.dot(p.astype(BF16), mv_ref[:, hl], preferred_element_type=F32).astype(BF16))
    o = jnp.concatenate(outs, axis=1)
    o_ref[...] = x + jnp.dot(o, wo_ref[...], preferred_element_type=F32)


def _cross(h, g, wq, wo, mem_k, mem_v, *, tm):
    b, t, _ = h.shape
    tok = pl.BlockSpec((None, tm, D_MODEL), lambda i, j: (i, j, 0))
    mem = pl.BlockSpec((None, MEM_TOKENS, MEM_W), lambda i, j: (i, 0, 0))
    return pl.pallas_call(
        _cross_kernel,
        grid=(b, t // tm),
        in_specs=[tok, pl.BlockSpec((1, D_MODEL), lambda i, j: (0, 0)),
                  pl.BlockSpec((D_MODEL, MEM_W), lambda i, j: (0, 0)),
                  pl.BlockSpec((MEM_W, D_MODEL), lambda i, j: (0, 0)), mem, mem],
        out_specs=tok,
        out_shape=jax.ShapeDtypeStruct((b, t, D_MODEL), F32),
        compiler_params=_cp(("arbitrary", "arbitrary"), 32),
        name="cross_attention",
    )(h, g, wq, wo, mem_k, mem_v)


def _memkv_kernel(mem_ref, g_ref, wk_ref, wv_ref, k_ref, v_ref, kb_ref, vb_ref):
    mn = _rms(mem_ref[...], g_ref[...]).astype(BF16)
    k = jnp.dot(mn, wk_ref[...], preferred_element_type=F32)
    v = jnp.dot(mn, wv_ref[...], preferred_element_type=F32)
    k_ref[...] = k
    v_ref[...] = v
    kb_ref[...] = k.astype(BF16)
    vb_ref[...] = v.astype(BF16)


def _memkv(mem, g, wk, wv, *, tm):
    n = mem.shape[0]
    depth = g.shape[0]
    w_spec = pl.BlockSpec((None, D_MODEL, MEM_W), lambda l, i: (l, 0, 0))
    o_spec = pl.BlockSpec((None, tm, MEM_W), lambda l, i: (l, i, 0))
    return pl.pallas_call(
        _memkv_kernel,
        grid=(depth, n // tm),
        in_specs=[pl.BlockSpec((tm, D_MODEL), lambda l, i: (i, 0)),
                  pl.BlockSpec((None, 1, D_MODEL), lambda l, i: (l, 0, 0)), w_spec, w_spec],
        out_specs=[o_spec] * 4,
        out_shape=[jax.ShapeDtypeStruct((depth, n, MEM_W), dt) for dt in (F32, F32, BF16, BF16)],
        compiler_params=_cp(("arbitrary", "arbitrary"), 32),
        name="memory_kv",
    )(mem, g, wk, wv)


def _silu(x):
    return x * _sigmoid(x)


def _swiglu_chunk(x, wg_ref, wu_ref, wd_ref):
    gate_up = []
    for s in range(FF_TILE // FF_SUB):
        cols = slice(s * FF_SUB, (s + 1) * FF_SUB)
        gate_up.append((jnp.dot(x, wg_ref[:, cols], preferred_element_type=F32),
                        jnp.dot(x, wu_ref[:, cols], preferred_element_type=F32)))
    y = None
    for s, (g, u) in enumerate(gate_up):
        part = jnp.dot((_silu(g) * u).astype(BF16), wd_ref[s * FF_SUB:(s + 1) * FF_SUB, :], preferred_element_type=F32)
        y = part if y is None else y + part
    return y


def _ffn_kernel(h_ref, g_ref, wg_ref, wu_ref, wd_ref, o_ref, xn_sc):
    @pl.when(pl.program_id(1) == 0)
    def _():
        x = h_ref[...]
        xn_sc[...] = _rms(x, g_ref[...]).astype(BF16)
        o_ref[...] = x

    o_ref[...] += _swiglu_chunk(xn_sc[...], wg_ref, wu_ref, wd_ref)


def _ffn(h, g, wg, wu, wd, *, tm):
    n = h.shape[0]
    tok = pl.BlockSpec((tm, D_MODEL), lambda i, j: (i, 0))
    return pl.pallas_call(
        _ffn_kernel,
        grid=(n // tm, D_FF // FF_TILE),
        in_specs=[tok, pl.BlockSpec((1, D_MODEL), lambda i, j: (0, 0)),
                  pl.BlockSpec((D_MODEL, FF_TILE), lambda i, j: (0, j)),
                  pl.BlockSpec((D_MODEL, FF_TILE), lambda i, j: (0, j)),
                  pl.BlockSpec((FF_TILE, D_MODEL), lambda i, j: (j, 0))],
        out_specs=tok,
        out_shape=jax.ShapeDtypeStruct((n, D_MODEL), F32),
        scratch_shapes=[pltpu.VMEM((tm, D_MODEL), BF16)],
        compiler_params=_cp(("arbitrary", "arbitrary"), 48),
        name="dense_swiglu",
    )(h, g, wg, wu, wd)


def _odd_kernel(h_ref, g_ref, win_ref, cw_ref, wout_ref, past_ref, o_ref, st_ref, u_sc):
    j = pl.program_id(1)
    tm = h_ref.shape[0]

    @pl.when(j == 0)
    def _():
        u_sc[0:SUBLANES, :] = past_ref[...]

    @pl.when(j > 0)
    def _():
        u_sc[0:SUBLANES, :] = u_sc[tm:tm + SUBLANES, :]

    x = h_ref[...]
    xn = _rms(x, g_ref[...]).astype(BF16)
    z = jnp.dot(xn, win_ref[...], preferred_element_type=F32)
    gate_b = z[:, 0:D_MODEL]
    u = z[:, D_MODEL:2 * D_MODEL] * z[:, 2 * D_MODEL:]
    u_sc[SUBLANES:, :] = u
    conv = (cw_ref[0:1, :] * u_sc[SUBLANES - 2:SUBLANES - 2 + tm, :]
            + cw_ref[1:2, :] * u_sc[SUBLANES - 1:SUBLANES - 1 + tm, :]
            + cw_ref[2:3, :] * u)
    o_ref[...] = x + jnp.dot((gate_b * conv).astype(BF16), wout_ref[...], preferred_element_type=F32)

    @pl.when(j == pl.num_programs(1) - 1)
    def _():
        st_ref[...] = u_sc[tm:tm + SUBLANES, :]


def _odd(h, g, w_in, cw, w_out, past, *, tm):
    b, t, _ = h.shape
    tok = pl.BlockSpec((None, tm, D_MODEL), lambda i, j: (i, j, 0))
    st = pl.BlockSpec((None, SUBLANES, D_MODEL), lambda i, j: (i, 0, 0))
    return pl.pallas_call(
        _odd_kernel,
        grid=(b, t // tm),
        in_specs=[tok, pl.BlockSpec((1, D_MODEL), lambda i, j: (0, 0)),
                  pl.BlockSpec((D_MODEL, 3 * D_MODEL), lambda i, j: (0, 0)),
                  pl.BlockSpec((SUBLANES, D_MODEL), lambda i, j: (0, 0)),
                  pl.BlockSpec((D_MODEL, D_MODEL), lambda i, j: (0, 0)), st],
        out_specs=[tok, st],
        out_shape=[jax.ShapeDtypeStruct((b, t, D_MODEL), F32), jax.ShapeDtypeStruct((b, SUBLANES, D_MODEL), F32)],
        scratch_shapes=[pltpu.VMEM((tm + SUBLANES, D_MODEL), F32)],
        compiler_params=_cp(("arbitrary", "arbitrary"), 48),
        name="short_conv_mixer",
    )(h, g, w_in, cw, w_out, past)


def _top2(logits, lane_f):
    lg = jnp.where(lane_f < N_EXPERTS, logits, -jnp.inf)
    m1 = jnp.max(lg, axis=1, keepdims=True)
    i1 = jnp.min(jnp.where(lg == m1, lane_f, float(LANES)), axis=1, keepdims=True)
    lg2 = jnp.where(lane_f == i1, -jnp.inf, lg)
    m2 = jnp.max(lg2, axis=1, keepdims=True)
    i2 = jnp.min(jnp.where(lg2 == m2, lane_f, float(LANES)), axis=1, keepdims=True)
    e2 = jnp.exp(m2 - m1)
    inv = 1.0 / (1.0 + e2)
    return i1, i2, inv, e2 * inv


def _route(logits, lane_f):
    i1, i2, g1, g2 = _top2(logits, lane_f)
    return jnp.where(lane_f == i1, g1, 0.0) + jnp.where(lane_f == i2, g2, 0.0)


def _moe_kernel(h_ref, g_ref, rw_ref, wg_ref, wu_ref, wd_ref, fg_ref, o_ref, xn_sc, comb_sc):
    e = pl.program_id(1)
    j = pl.program_id(2)
    tm = h_ref.shape[0]
    lane = lax.broadcasted_iota(jnp.int32, (tm, LANES), 1)

    @pl.when((e == 0) & (j == 0))
    def _():
        x = h_ref[...]
        xn = _rms(x, g_ref[...]).astype(BF16)
        xn_sc[...] = xn
        logits = jnp.dot(xn, rw_ref[...], preferred_element_type=F32)
        comb_sc[...] = _route(logits, lane.astype(F32))
        o_ref[...] = x

    y = _swiglu_chunk(xn_sc[...], wg_ref, wu_ref, wd_ref)
    w_e = jnp.sum(jnp.where(lane == e, comb_sc[...], 0.0), axis=1, keepdims=True)
    o_ref[...] += y * w_e

    @pl.when((e == pl.num_programs(1) - 1) & (j == pl.num_programs(2) - 1))
    def _():
        o_ref[...] = _rms(o_ref[...], fg_ref[...])


def _moe(h, g, rw, wg, wu, wd, fg, *, tm):
    n = h.shape[0]
    tok = pl.BlockSpec((tm, D_MODEL), lambda i, e, j: (i, 0))
    vec = pl.BlockSpec((1, D_MODEL), lambda i, e, j: (0, 0))
    return pl.pallas_call(
        _moe_kernel,
        grid=(n // tm, N_EXPERTS, D_FF // FF_TILE),
        in_specs=[tok, vec, pl.BlockSpec((D_MODEL, LANES), lambda i, e, j: (0, 0)),
                  pl.BlockSpec((None, D_MODEL, FF_TILE), lambda i, e, j: (e, 0, j)),
                  pl.BlockSpec((None, D_MODEL, FF_TILE), lambda i, e, j: (e, 0, j)),
                  pl.BlockSpec((None, FF_TILE, D_MODEL), lambda i, e, j: (e, j, 0)), vec],
        out_specs=tok,
        out_shape=jax.ShapeDtypeStruct((n, D_MODEL), F32),
        scratch_shapes=[pltpu.VMEM((tm, D_MODEL), BF16), pltpu.VMEM((tm, LANES), F32)],
        compiler_params=_cp(("arbitrary", "arbitrary", "arbitrary"), 48),
        name="moe_swiglu",
    )(h, g, rw, wg, wu, wd, fg)


_R_E1, _R_E2, _R_G1, _R_G2, _R_P1, _R_P2 = range(6)


def _router_kernel(h_ref, g_ref, rw_ref, xn_ref, rt_ref, cnt_ref, carry_ref):
    @pl.when(pl.program_id(0) == 0)
    def _():
        carry_ref[...] = jnp.zeros_like(carry_ref)

    xn = _rms(h_ref[...], g_ref[...])
    xn_ref[...] = xn
    tm = xn.shape[0]
    lane = lax.broadcasted_iota(jnp.int32, (tm, LANES), 1)
    lane_f = lane.astype(F32)
    logits = jnp.dot(xn.astype(BF16), rw_ref[...], preferred_element_type=F32)
    i1, i2, g1, g2 = _top2(logits, lane_f)
    sel = jnp.where((lane_f == i1) | (lane_f == i2), 1.0, 0.0)
    incl = _cumsum_rows(sel)
    rank = incl - sel + carry_ref[...]
    p1 = jnp.sum(jnp.where(lane_f == i1, rank, 0.0), axis=1, keepdims=True)
    p2 = jnp.sum(jnp.where(lane_f == i2, rank, 0.0), axis=1, keepdims=True)
    carry_ref[...] = carry_ref[...] + incl[tm - 1:tm, :]
    cnt_ref[...] = carry_ref[...]
    rec = jnp.zeros((tm, LANES), F32)
    for ln, val in ((_R_E1, i1), (_R_E2, i2), (_R_G1, g1), (_R_G2, g2), (_R_P1, p1), (_R_P2, p2)):
        rec = jnp.where(lane == ln, val, rec)
    rt_ref[...] = rec


def _router(h, g, rw, *, tm):
    n = h.shape[0]
    return pl.pallas_call(
        _router_kernel,
        grid=(n // tm,),
        in_specs=[pl.BlockSpec((tm, D_MODEL), lambda i: (i, 0)), pl.BlockSpec((1, D_MODEL), lambda i: (0, 0)),
                  pl.BlockSpec((D_MODEL, LANES), lambda i: (0, 0))],
        out_specs=[pl.BlockSpec((tm, D_MODEL), lambda i: (i, 0)), pl.BlockSpec((tm, LANES), lambda i: (i, 0)),
                   pl.BlockSpec((1, LANES), lambda i: (0, 0))],
        out_shape=[jax.ShapeDtypeStruct((n, D_MODEL), F32), jax.ShapeDtypeStruct((n, LANES), F32),
                   jax.ShapeDtypeStruct((1, LANES), F32)],
        scratch_shapes=[pltpu.VMEM((1, LANES), F32)],
        compiler_params=_cp(("arbitrary",), 32),
        name="moe_router",
    )(h, g, rw)


def _row_copy(src, src_row, dst, dst_row, sem):
    return pltpu.make_async_copy(src.at[pl.ds(src_row, 1), :], dst.at[pl.ds(dst_row, 1), :], sem)


def _dispatch_kernel(ends_ref, dest_ref, x_ref, xs_ref, zero_sc, sem, *, tm_expert):
    tm = x_ref.shape[0]

    @pl.when(pl.program_id(0) == 0)
    def _():
        zero_sc[...] = jnp.zeros_like(zero_sc)

        def zero_tile(first_row, wanted):
            @pl.when(wanted)
            def _():
                cp = pltpu.make_async_copy(zero_sc, xs_ref.at[pl.ds(pl.multiple_of(first_row, tm_expert), tm_expert), :], sem)
                cp.start()
                cp.wait()

        for e in range(N_EXPERTS):
            begin = ends_ref[e - 1] if e else 0
            zero_tile(ends_ref[e] - tm_expert, ends_ref[e] > begin)
        for k in range(N_EXPERTS):
            first_row = ends_ref[N_EXPERTS - 1] + k * tm_expert
            zero_tile(first_row, first_row < xs_ref.shape[0])

    def start(t, c):
        _row_copy(x_ref, t, xs_ref, dest_ref[0, 2 * t], sem).start()
        _row_copy(x_ref, t, xs_ref, dest_ref[0, 2 * t + 1], sem).start()
        return c

    def wait(t, c):
        _row_copy(x_ref, t, xs_ref, dest_ref[0, 2 * t], sem).wait()
        _row_copy(x_ref, t, xs_ref, dest_ref[0, 2 * t + 1], sem).wait()
        return c

    lax.fori_loop(0, tm, start, 0, unroll=MOE_DMA_UNROLL)
    lax.fori_loop(0, tm, wait, 0, unroll=MOE_DMA_UNROLL)


def _dispatch(ends, dest, xn, *, rows, tm, tm_expert):
    n = xn.shape[0]
    return pl.pallas_call(
        functools.partial(_dispatch_kernel, tm_expert=tm_expert),
        grid_spec=pltpu.PrefetchScalarGridSpec(
            num_scalar_prefetch=1,
            grid=(n // tm,),
            in_specs=[pl.BlockSpec((None, 1, 2 * tm), lambda i, ends: (i, 0, 0), memory_space=pltpu.SMEM),
                      pl.BlockSpec((tm, D_MODEL), lambda i, ends: (i, 0))],
            out_specs=pl.BlockSpec(memory_space=pl.ANY),
            scratch_shapes=[pltpu.VMEM((tm_expert, D_MODEL), F32), pltpu.SemaphoreType.DMA(())]),
        out_shape=jax.ShapeDtypeStruct((rows, D_MODEL), F32),
        compiler_params=_cp(("arbitrary",), 32),
        name="moe_dispatch",
    )(ends, dest.reshape(n // tm, 1, 2 * tm), xn)


def _expert_kernel(te_ref, nv_ref, x_ref, wg_ref, wu_ref, wd_ref, o_ref, xb_sc):
    n_valid = nv_ref[pl.program_id(0)]

    @pl.when(pl.program_id(1) == 0)
    def _():
        xb_sc[...] = x_ref[...].astype(BF16)
        o_ref[...] = jnp.zeros_like(o_ref)

    @pl.when(n_valid > 0)
    def _():
        o_ref[...] += _swiglu_chunk(xb_sc[...], wg_ref, wu_ref, wd_ref)


def _experts(tile_expert, tile_valid, xs, wg, wu, wd, *, tm):
    rows = xs.shape[0]
    tok = pl.BlockSpec((tm, D_MODEL), lambda i, j, te, nv: (i, 0))
    return pl.pallas_call(
        _expert_kernel,
        grid_spec=pltpu.PrefetchScalarGridSpec(
            num_scalar_prefetch=2,
            grid=(rows // tm, D_FF // FF_TILE),
            in_specs=[tok,
                      pl.BlockSpec((None, D_MODEL, FF_TILE), lambda i, j, te, nv: (te[i], 0, j)),
                      pl.BlockSpec((None, D_MODEL, FF_TILE), lambda i, j, te, nv: (te[i], 0, j)),
                      pl.BlockSpec((None, FF_TILE, D_MODEL), lambda i, j, te, nv: (te[i], j, 0))],
            out_specs=tok,
            scratch_shapes=[pltpu.VMEM((tm, D_MODEL), BF16)]),
        out_shape=jax.ShapeDtypeStruct((rows, D_MODEL), F32),
        compiler_params=_cp(("arbitrary", "arbitrary"), 48),
        name="moe_experts",
    )(tile_expert, tile_valid, xs, wg, wu, wd)


def _combine_kernel(dest_ref, h_ref, rt_ref, fg_ref, ys_ref, o_ref, y1_sc, y2_sc, sem):
    tm = h_ref.shape[0]

    def start(t, c):
        _row_copy(ys_ref, dest_ref[0, 2 * t], y1_sc, t, sem).start()
        _row_copy(ys_ref, dest_ref[0, 2 * t + 1], y2_sc, t, sem).start()
        return c

    def wait(t, c):
        _row_copy(ys_ref, dest_ref[0, 2 * t], y1_sc, t, sem).wait()
        _row_copy(ys_ref, dest_ref[0, 2 * t + 1], y2_sc, t, sem).wait()
        return c

    lax.fori_loop(0, tm, start, 0, unroll=MOE_DMA_UNROLL)
    lax.fori_loop(0, tm, wait, 0, unroll=MOE_DMA_UNROLL)
    rt = rt_ref[...]
    moe = rt[:, _R_G1:_R_G1 + 1] * y1_sc[...] + rt[:, _R_G2:_R_G2 + 1] * y2_sc[...]
    o_ref[...] = _rms(h_ref[...] + moe, fg_ref[...])


def _combine(dest, h, rt, fg, ys, *, tm):
    n = h.shape[0]
    tok = pl.BlockSpec((tm, D_MODEL), lambda i: (i, 0))
    return pl.pallas_call(
        _combine_kernel,
        grid=(n // tm,),
        in_specs=[pl.BlockSpec((None, 1, 2 * tm), lambda i: (i, 0, 0), memory_space=pltpu.SMEM),
                  tok, pl.BlockSpec((tm, LANES), lambda i: (i, 0)), pl.BlockSpec((1, D_MODEL), lambda i: (0, 0)),
                  pl.BlockSpec(memory_space=pl.ANY)],
        out_specs=tok,
        out_shape=jax.ShapeDtypeStruct((n, D_MODEL), F32),
        scratch_shapes=[pltpu.VMEM((tm, D_MODEL), F32), pltpu.VMEM((tm, D_MODEL), F32), pltpu.SemaphoreType.DMA(())],
        compiler_params=_cp(("arbitrary",), 32),
        name="moe_combine",
    )(dest.reshape(n // tm, 1, 2 * tm), h, rt, fg, ys)


def _moe_routed(h, g, rw, wg, wu, wd, fg, *, tm, tm_expert):
    n = h.shape[0]
    xn, rt, counts = _router(h, g, rw, tm=tm)
    counts = counts[0, :N_EXPERTS].astype(jnp.int32)
    group = (counts + tm_expert - 1) // tm_expert * tm_expert
    ends = jnp.cumsum(group)
    starts = ends - group
    e = rt[:, _R_E1:_R_E2 + 1].astype(jnp.int32)
    dest = (starts[e] + rt[:, _R_P1:_R_P2 + 1].astype(jnp.int32)).reshape(2 * n)
    n_tiles = (TOP_K * n) // tm_expert + N_EXPERTS
    tile_start = jnp.arange(n_tiles, dtype=jnp.int32) * tm_expert
    active = tile_start < ends[-1]
    last_active = ends[-1] // tm_expert - 1
    probe = jnp.minimum(tile_start, last_active * tm_expert)
    tile_expert = jnp.sum((probe[:, None] >= ends[None, :]).astype(jnp.int32), axis=1)
    tile_valid = jnp.where(active, jnp.clip(starts[tile_expert] + counts[tile_expert] - tile_start, 0, tm_expert), 0)
    xs = _dispatch(ends.astype(jnp.int32), dest, xn, rows=n_tiles * tm_expert, tm=tm, tm_expert=tm_expert)
    ys = _experts(tile_expert, tile_valid.astype(jnp.int32), xs, wg, wu, wd, tm=tm_expert)
    return _combine(dest, h, rt, fg, ys, tm=tm)


def _pack_params(p):
    fq, fk, fv, ff, mq, mk, mv, mo, mi, mf = jnp.split(p['even_w_in'][0], list(EVEN_SPLITS), axis=1)
    gate_w = jnp.concatenate([ff, mi, mf, jnp.zeros((D_MODEL, LANES - _G_END), F32)], axis=1)
    gate_b = jnp.concatenate([p['fox_b_f'][0], p['mlstm_b_i'][0], p['mlstm_b_f'][0], jnp.zeros((LANES - _G_END,), F32)])
    row = lambda a: a.reshape(1, -1).astype(F32)
    return dict(
        even_w=jnp.concatenate([fq, fk, fv, mq, mk, mv, mo, gate_w], axis=1).astype(BF16),
        even_b=gate_b.reshape(1, LANES),
        even_w_out=p['even_w_out'][0].astype(BF16),
        mlstm_norm_g=row(p['mlstm_norm_g'][0]),
        norm_mix_g=[row(p['norm_mix_g'][l]) for l in range(2)],
        norm_cross_g=[row(p['norm_cross_g'][l]) for l in range(2)],
        norm_ffn_g=[row(p['norm_ffn_g'][l]) for l in range(2)],
        final_norm_g=row(p['final_norm_g']),
        mem_wq=p['mem_wq'].astype(BF16), mem_wo=p['mem_wo'].astype(BF16),
        ffn_wg=p['ffn_w_gate'][0].astype(BF16), ffn_wu=p['ffn_w_up'][0].astype(BF16), ffn_wd=p['ffn_w_down'][0].astype(BF16),
        odd_w_in=p['odd_w_in'][0].astype(BF16), odd_w_out=p['odd_w_out'][0].astype(BF16),
        conv_w=jnp.concatenate([p['conv_w'][0], jnp.zeros((SUBLANES - CONV_W, D_MODEL), F32)], axis=0),
        router_w=jnp.concatenate([p['router_w'][0], jnp.zeros((D_MODEL, LANES - N_EXPERTS), F32)], axis=1).astype(BF16),
        moe_wg=p['moe_w_gate'][0].astype(BF16), moe_wu=p['moe_w_up'][0].astype(BF16), moe_wd=p['moe_w_down'][0].astype(BF16),
    )


def _trunk(x, mem_k, mem_v, caches, w, *, tm, tm_wide, chunk):
    b, t, _ = x.shape
    n = b * t
    flat = lambda a: a.reshape(n, a.shape[-1])
    per_b = lambda a: a.reshape(b, t, a.shape[-1])

    if caches is None:
        f0 = jnp.zeros((b, 1, LANES), F32)
    else:
        cache_k, cache_v, cache_lf, c0, n0, m0, conv_st = caches
        past_len = cache_k.shape[1]
        lf_pad = jnp.pad(cache_lf.astype(F32), ((0, 0), (0, 0), (0, LANES - FOX_HEADS)))
        ka_cache, f0 = _cache_prep(cache_k.reshape(b, past_len, FOX_W).astype(F32), lf_pad, tm=FOX_TILE)
    qa, ka, fk, fv, vb, mq, mk, mv, mo, gt = _even_in(flat(x), w['norm_mix_g'][0], w['even_w'], w['even_b'], f0,
                                                      tm=tm, tiles_per_batch=t // tm)
    if caches is None:
        fox_t = _fox_attention(per_b(qa), per_b(ka), per_b(vb), q0=0)
        c_ext0 = jnp.zeros((b, MLSTM_HEADS, MLSTM_EXT, LANES), F32)
        m_ext0 = jnp.zeros((b, SUBLANES, LANES), F32)
        past = jnp.zeros((b, SUBLANES, D_MODEL), F32)
    else:
        pad_t = lambda a: jnp.pad(per_b(a), ((0, 0), (0, FOX_TILE - t), (0, 0)))
        k_all = jnp.concatenate([ka_cache, pad_t(ka)], axis=1)
        v_all = jnp.concatenate([cache_v.reshape(b, past_len, FOX_W).astype(BF16), pad_t(vb)], axis=1)
        fox_full = _fox_attention(pad_t(qa), k_all, v_all, q0=past_len // FOX_TILE)
        fox_t = jnp.swapaxes(fox_full[:, :, :t], 1, 2)
        c_ext0 = jnp.concatenate([
            jnp.pad(c0.astype(F32), ((0, 0), (0, 0), (0, 0), (0, LANES - MLSTM_DK))),
            jnp.broadcast_to(jnp.pad(n0.astype(F32), ((0, 0), (0, 0), (0, LANES - MLSTM_DK)))[:, :, None, :],
                             (b, MLSTM_HEADS, MLSTM_EXT - MLSTM_DV, LANES))],
            axis=2)
        m_ext0 = jnp.broadcast_to(jnp.pad(m0.astype(F32), ((0, 0), (0, SUBLANES - MLSTM_HEADS)))[:, :, None],
                                  (b, SUBLANES, LANES))
        past = jnp.pad(conv_st.astype(F32), ((0, 0), (SUBLANES - (CONV_W - 1), 0), (0, 0)))
    hm, c_ext, m_ext = _mlstm(per_b(mq), per_b(mk), per_b(mv), per_b(mo), per_b(gt), c_ext0, m_ext0,
                              w['mlstm_norm_g'], chunk=chunk)
    h = _even_out(x, fox_t, hm, w['even_w_out'], tm=tm, fox_transposed=caches is None)
    h = _cross(h, w['norm_cross_g'][0], w['mem_wq'][0], w['mem_wo'][0], mem_k[0], mem_v[0], tm=tm)
    h = _ffn(flat(h), w['norm_ffn_g'][0], w['ffn_wg'], w['ffn_wu'], w['ffn_wd'], tm=tm_wide)

    h, conv_new = _odd(per_b(h), w['norm_mix_g'][1], w['odd_w_in'], w['conv_w'], w['odd_w_out'], past, tm=tm)
    h = _cross(h, w['norm_cross_g'][1], w['mem_wq'][1], w['mem_wo'][1], mem_k[1], mem_v[1], tm=tm)
    moe_args = (flat(h), w['norm_ffn_g'][1], w['router_w'], w['moe_wg'], w['moe_wu'], w['moe_wd'], w['final_norm_g'])
    if n * TOP_K >= N_EXPERTS * MOE_EXPERT_TILE:
        y = _moe_routed(*moe_args, tm=MOE_TOKEN_TILE, tm_expert=MOE_EXPERT_TILE)
    else:
        y = _moe(*moe_args, tm=tm_wide)

    states = (
        fk.reshape(1, b, t, FOX_HEADS, FOX_DH), fv.reshape(1, b, t, FOX_HEADS, FOX_DH),
        per_b(gt)[None, :, :, _G_FOX:_G_IG],
        c_ext[None, :, :, 0:MLSTM_DV, 0:MLSTM_DK], c_ext[None, :, :, MLSTM_DV, 0:MLSTM_DK], m_ext[None, :, 0:MLSTM_HEADS, 0],
        conv_new[None, :, SUBLANES - (CONV_W - 1):, :],
    )
    return per_b(y), states


def kernel(x_prompt, x_sample, mem_prompt, cache_fox_k, cache_fox_v, cache_fox_logf, state_mlstm_c, state_mlstm_n, state_mlstm_m, state_conv, cache_mem_k, cache_mem_v, norm_mix_g, norm_mem_g, norm_cross_g, norm_ffn_g, final_norm_g, even_w_in, fox_b_f, mlstm_b_i, mlstm_b_f, mlstm_norm_g, even_w_out, odd_w_in, conv_w, odd_w_out, mem_wq, mem_wk, mem_wv, mem_wo, ffn_w_gate, ffn_w_up, ffn_w_down, router_w, moe_w_gate, moe_w_up, moe_w_down):
    w = _pack_params(dict(
        norm_mix_g=norm_mix_g, norm_cross_g=norm_cross_g, norm_ffn_g=norm_ffn_g, final_norm_g=final_norm_g,
        even_w_in=even_w_in, fox_b_f=fox_b_f, mlstm_b_i=mlstm_b_i, mlstm_b_f=mlstm_b_f, mlstm_norm_g=mlstm_norm_g,
        even_w_out=even_w_out, odd_w_in=odd_w_in, conv_w=conv_w, odd_w_out=odd_w_out, mem_wq=mem_wq, mem_wo=mem_wo,
        ffn_w_gate=ffn_w_gate, ffn_w_up=ffn_w_up, ffn_w_down=ffn_w_down, router_w=router_w,
        moe_w_gate=moe_w_gate, moe_w_up=moe_w_up, moe_w_down=moe_w_down))

    bp, tp, _ = x_prompt.shape
    bs, ts, _ = x_sample.shape
    depth = norm_mem_g.shape[0]

    mem_k_p, mem_v_p, mem_kb, mem_vb = _memkv(mem_prompt.reshape(bp * MEM_TOKENS, D_MODEL),
                                              norm_mem_g.reshape(depth, 1, D_MODEL).astype(F32),
                                              mem_wk.astype(BF16), mem_wv.astype(BF16), tm=512)
    per_layer = lambda a, nb: a.reshape(depth, nb, MEM_TOKENS, MEM_W)
    y_prompt, st_p = _trunk(x_prompt, per_layer(mem_kb, bp), per_layer(mem_vb, bp), None, w,
                            tm=512, tm_wide=1024, chunk=MLSTM_CHUNK)

    caches = (cache_fox_k[0], cache_fox_v[0], cache_fox_logf[0], state_mlstm_c[0], state_mlstm_n[0],
              state_mlstm_m[0], state_conv[0])
    y_sample, st_s = _trunk(x_sample, per_layer(cache_mem_k.astype(BF16), bs), per_layer(cache_mem_v.astype(BF16), bs),
                            caches, w, tm=ts, tm_wide=bs * ts, chunk=ts)

    mem_shape = (depth, bp, MEM_TOKENS, MEM_HEADS, MEM_DH)
    return (y_prompt, y_sample) + st_p + (mem_k_p.reshape(mem_shape), mem_v_p.reshape(mem_shape)) + st_s
```

```python
import functools

import jax
import jax.numpy as jnp
from jax import lax
from jax.experimental import pallas as pl
from jax.experimental.pallas import tpu as pltpu

F32 = jnp.float32
BF16 = jnp.bfloat16

D_MODEL = 1024
EPS = 1e-6
FOX_HEADS = 8
FOX_DH = 64
FOX_W = FOX_HEADS * FOX_DH
FOX_SCALE = FOX_DH ** -0.5
MLSTM_HEADS = 4
MLSTM_DV = 128
MLSTM_DK = 64
MLSTM_W = MLSTM_HEADS * MLSTM_DV
MLSTM_QK_W = MLSTM_HEADS * MLSTM_DK
MLSTM_SCALE = MLSTM_DK ** -0.5
MLSTM_CHUNK = 256
MLSTM_EXT = 256
MEM_TOKENS = 256
MEM_HEADS = 4
MEM_DH = 128
MEM_W = MEM_HEADS * MEM_DH
MEM_SCALE = MEM_DH ** -0.5
D_FF = 3584
N_EXPERTS = 8
TOP_K = 2
MOE_EXPERT_TILE = 1024
MOE_TOKEN_TILE = 512
MOE_DMA_UNROLL = 8
CONV_W = 3
EVEN_SIZES = (FOX_W, FOX_W, FOX_W, FOX_HEADS, MLSTM_QK_W, MLSTM_QK_W, MLSTM_W, MLSTM_W, MLSTM_HEADS, MLSTM_HEADS)
EVEN_SPLITS = tuple(sum(EVEN_SIZES[:i + 1]) for i in range(len(EVEN_SIZES) - 1))

LANES = 128
SUBLANES = 8
FOX_AUG_W = FOX_HEADS * LANES
FOX_TILE = 256
FOX_LOOKAHEAD = 3
FOX_PV_DELAY = 2
FF_TILE = 512
FF_SUB = 256
_C_Q, _C_K, _C_V, _C_MQ, _C_MK, _C_MV, _C_MO, _C_G, _C_END = 0, 512, 1024, 1536, 1792, 2048, 2560, 3072, 3200
_G_FOX, _G_IG, _G_LF, _G_END = 0, 8, 12, 16
NEG = -1e30
LOG2E = 1.4426950408889634


def _cp(sem, vmem_mb):
    return pltpu.CompilerParams(dimension_semantics=sem, vmem_limit_bytes=vmem_mb * 1024 * 1024)


def _rms(x, g):
    return x * lax.rsqrt(jnp.mean(x * x, axis=-1, keepdims=True) + EPS) * g


def _log_sigmoid(x):
    return jnp.minimum(x, 0.0) - jnp.log1p(jnp.exp(-jnp.abs(x)))


def _sigmoid(x):
    return 1.0 / (1.0 + jnp.exp(-x))


def _cumsum_rows(x):
    n = x.shape[0]
    row = lax.broadcasted_iota(jnp.int32, x.shape, 0)
    s = 1
    while s < n:
        x = x + jnp.where(row >= s, pltpu.roll(x, s, axis=0), 0.0)
        s *= 2
    return x


def _split3(f):
    hi = f.astype(BF16).astype(F32)
    r = f - hi
    mid = r.astype(BF16).astype(F32)
    return hi, mid, r - mid


def _head_block(src, h, lane):
    p, odd = divmod(h, 2)
    blk = src[:, LANES * p:LANES * (p + 1)]
    return pltpu.roll(blk, FOX_DH, axis=1) if odd else blk


_AUG_Q = (64, 72, 80)
_AUG_K = (88, 96, 104)


def _fox_bias_lanes(cum_f, lane):
    parts = _split3(jnp.where(lane < FOX_HEADS, cum_f * LOG2E, 0.0))
    q_side = sum(pltpu.roll(p, off, axis=1) for p, off in zip(parts, _AUG_Q))
    k_side = sum(pltpu.roll(-p, off, axis=1) for p, off in zip(parts, _AUG_K))
    return q_side, k_side


def _fox_aug(src, bias, h, is_query):
    lane = lax.broadcasted_iota(jnp.int32, (1, LANES), 1)
    blk = _head_block(src, h, lane)
    q_lanes = (lane == _AUG_Q[0] + h) | (lane == _AUG_Q[1] + h) | (lane == _AUG_Q[2] + h)
    k_lanes = (lane == _AUG_K[0] + h) | (lane == _AUG_K[1] + h) | (lane == _AUG_K[2] + h)
    own, ones = (q_lanes, k_lanes) if is_query else (k_lanes, q_lanes)
    aug = jnp.where(own, bias, jnp.where(ones, 1.0, 0.0))
    return jnp.where(lane < FOX_DH, blk, aug).astype(BF16)


def _even_in_kernel(x_ref, g_ref, w_ref, b_ref, f0_ref,
                    qa_ref, ka_ref, fk_ref, fv_ref, vb_ref, mq_ref, mk_ref, mv_ref, mo_ref, gt_ref,
                    carry_ref, *, tiles_per_batch):
    @pl.when(pl.program_id(0) % tiles_per_batch == 0)
    def _():
        carry_ref[...] = f0_ref[...]

    xn = _rms(x_ref[...], g_ref[...]).astype(BF16)
    acc = jnp.dot(xn, w_ref[...], preferred_element_type=F32)
    tm = acc.shape[0]
    lane = lax.broadcasted_iota(jnp.int32, (tm, LANES), 1)

    gates = acc[:, _C_G:_C_END] + b_ref[...]
    ls = _log_sigmoid(gates)
    is_ig = (lane >= _G_IG) & (lane < _G_LF)
    gt_ref[...] = jnp.where(is_ig, gates, jnp.where(lane < _G_END, ls, 0.0))
    cum_f = _cumsum_rows(jnp.where(lane < _G_IG, ls, 0.0)) + carry_ref[...]
    carry_ref[...] = cum_f[tm - 1:tm, :]

    q_all = acc[:, _C_Q:_C_K] * (FOX_SCALE * LOG2E)
    k_all = acc[:, _C_K:_C_V]
    v_all = acc[:, _C_V:_C_MQ]
    q_bias, k_bias = _fox_bias_lanes(cum_f, lane)
    for h in range(FOX_HEADS):
        qa_ref[:, LANES * h:LANES * (h + 1)] = _fox_aug(q_all, q_bias, h, True)
        ka_ref[:, LANES * h:LANES * (h + 1)] = _fox_aug(k_all, k_bias, h, False)
    for h in range(FOX_HEADS):
        fk_ref[:, h, :] = _head_block(k_all, h, lane)[:, 0:FOX_DH]
        fv_ref[:, h, :] = _head_block(v_all, h, lane)[:, 0:FOX_DH]
    vb_ref[...] = v_all.astype(BF16)

    mq_all = acc[:, _C_MQ:_C_MK] * MLSTM_SCALE
    mk_all = acc[:, _C_MK:_C_MV]
    for h in range(MLSTM_HEADS):
        mq_ref[:, LANES * h:LANES * (h + 1)] = jnp.where(lane < MLSTM_DK, _head_block(mq_all, h, lane), 0.0).astype(BF16)
        mk_ref[:, LANES * h:LANES * (h + 1)] = jnp.where(lane < MLSTM_DK, _head_block(mk_all, h, lane), 0.0).astype(BF16)
    mv_ref[...] = acc[:, _C_MV:_C_MO].astype(BF16)
    mo_ref[...] = acc[:, _C_MO:_C_G].astype(BF16)


def _even_in(x, g, w, bias, f0, *, tm, tiles_per_batch):
    n = x.shape[0]
    row = lambda i: (i, 0)
    fixed = lambda i: (0, 0)
    head_major = (FOX_HEADS, FOX_DH)
    widths = (FOX_AUG_W, FOX_AUG_W, head_major, head_major, FOX_W, MLSTM_W, MLSTM_W, MLSTM_W, MLSTM_W, LANES)
    dtypes = (BF16, BF16, F32, F32, BF16, BF16, BF16, BF16, BF16, F32)
    spec = lambda wd: (pl.BlockSpec((tm,) + wd, lambda i: (i, 0, 0)) if isinstance(wd, tuple) else pl.BlockSpec((tm, wd), row))
    shape = lambda wd: (n,) + wd if isinstance(wd, tuple) else (n, wd)
    return pl.pallas_call(
        functools.partial(_even_in_kernel, tiles_per_batch=tiles_per_batch),
        grid=(n // tm,),
        in_specs=[pl.BlockSpec((tm, D_MODEL), row), pl.BlockSpec((1, D_MODEL), fixed),
                  pl.BlockSpec((D_MODEL, _C_END), fixed), pl.BlockSpec((1, LANES), fixed),
                  pl.BlockSpec((None, 1, LANES), lambda i: (i // tiles_per_batch, 0, 0))],
        out_specs=[spec(wd) for wd in widths],
        out_shape=[jax.ShapeDtypeStruct(shape(wd), dt) for wd, dt in zip(widths, dtypes)],
        scratch_shapes=[pltpu.VMEM((1, LANES), F32)],
        compiler_params=_cp(("arbitrary",), 48),
        name="even_in",
    )(x, g, w, bias, f0)


def _cache_prep_kernel(k_ref, lf_ref, ka_ref, fend_ref, carry_ref):
    @pl.when(pl.program_id(1) == 0)
    def _():
        carry_ref[...] = jnp.zeros_like(carry_ref)

    k_all = k_ref[...]
    tm = k_all.shape[0]
    lane = lax.broadcasted_iota(jnp.int32, (tm, LANES), 1)
    cum_f = _cumsum_rows(lf_ref[...]) + carry_ref[...]
    carry_ref[...] = cum_f[tm - 1:tm, :]
    fend_ref[...] = cum_f[tm - 1:tm, :]
    _, k_bias = _fox_bias_lanes(cum_f, lane)
    for h in range(FOX_HEADS):
        ka_ref[:, LANES * h:LANES * (h + 1)] = _fox_aug(k_all, k_bias, h, False)


def _cache_prep(cache_k, cache_lf, *, tm):
    b, p, _ = cache_k.shape
    return pl.pallas_call(
        _cache_prep_kernel,
        grid=(b, p // tm),
        in_specs=[pl.BlockSpec((None, tm, FOX_W), lambda i, j: (i, j, 0)),
                  pl.BlockSpec((None, tm, LANES), lambda i, j: (i, j, 0))],
        out_specs=[pl.BlockSpec((None, tm, FOX_AUG_W), lambda i, j: (i, j, 0)),
                   pl.BlockSpec((None, 1, LANES), lambda i, j: (i, 0, 0))],
        out_shape=[jax.ShapeDtypeStruct((b, p, FOX_AUG_W), BF16), jax.ShapeDtypeStruct((b, 1, LANES), F32)],
        scratch_shapes=[pltpu.VMEM((1, LANES), F32)],
        compiler_params=_cp(("arbitrary", "arbitrary"), 32),
        name="fox_cache_prep",
    )(cache_k, cache_lf)


def _fox_kernel(q_ref, k_ref, v_ref, o_ref, m_sc, l_sc, acc_sc, s_sc, *, q0):
    t = FOX_TILE
    n_full = q0 + pl.program_id(1)
    visible = (lax.broadcasted_iota(jnp.int32, (t, t), 0) <= lax.broadcasted_iota(jnp.int32, (t, t), 1))
    m_sc[...] = jnp.full_like(m_sc, NEG)
    l_sc[...] = jnp.zeros_like(l_sc)
    acc_sc[...] = jnp.zeros_like(acc_sc)

    def scores(j, h):
        hl = slice(LANES * h, LANES * (h + 1))
        return lax.dot_general(k_ref[pl.ds(pl.multiple_of(j * t, t), t), hl], q_ref[:, hl], (((1,), (1,)), ((), ())),
                               preferred_element_type=F32)

    def step(j, diagonal):
        off = pl.multiple_of(j * t, t)

        def weighted_values(h, p, alpha):
            pair, hh = divmod(h, 2)
            v_t = v_ref[pl.ds(off, t), LANES * pair:LANES * (pair + 1)].T
            pv = jnp.dot(v_t[FOX_DH * hh:FOX_DH * (hh + 1), :], p, preferred_element_type=F32)
            rows = slice(FOX_DH * h, FOX_DH * (h + 1))
            acc_sc[rows, :] = alpha * acc_sc[rows, :] + pv

        pending = {h: s_sc[h] for h in range(FOX_LOOKAHEAD)}
        ready = {}
        for h in range(FOX_HEADS):
            s = pending.pop(h)
            if diagonal:
                s = jnp.where(visible, s, NEG)
            m_old = m_sc[h:h + 1, :]
            m_new = jnp.maximum(m_old, jnp.max(s, axis=0, keepdims=True))
            alpha = jnp.exp2(m_old - m_new)
            p = jnp.exp2(s - m_new)
            m_sc[h:h + 1, :] = m_new
            l_sc[h:h + 1, :] = alpha * l_sc[h:h + 1, :] + jnp.sum(p, axis=0, keepdims=True)
            ready[h] = (p.astype(BF16), alpha)
            ahead = h + FOX_LOOKAHEAD
            if ahead < FOX_HEADS:
                pending[ahead] = scores(j, ahead)
            elif not diagonal:
                s_sc[ahead - FOX_HEADS] = scores(j + 1, ahead - FOX_HEADS)
            if h - FOX_PV_DELAY in ready:
                weighted_values(h - FOX_PV_DELAY, *ready.pop(h - FOX_PV_DELAY))
        for h in sorted(ready):
            weighted_values(h, *ready[h])

    def body(j, c):
        step(j, False)
        return c

    for h in range(FOX_LOOKAHEAD):
        s_sc[h] = scores(0, h)
    lax.fori_loop(0, n_full, body, 0)
    step(n_full, True)
    for h in range(FOX_HEADS):
        rows = slice(FOX_DH * h, FOX_DH * (h + 1))
        o_ref[rows, :] = (acc_sc[rows, :] / l_sc[h:h + 1, :]).astype(BF16)


def _fox_attention(q_aug, k_aug, v, *, q0):
    b, tq_total, _ = q_aug.shape
    t_kv = k_aug.shape[1]
    nq = tq_total // FOX_TILE
    return pl.pallas_call(
        functools.partial(_fox_kernel, q0=q0),
        grid=(b, nq),
        in_specs=[pl.BlockSpec((None, FOX_TILE, FOX_AUG_W), lambda bi, i: (bi, i, 0)),
                  pl.BlockSpec((None, t_kv, FOX_AUG_W), lambda bi, i: (bi, 0, 0)),
                  pl.BlockSpec((None, t_kv, FOX_W), lambda bi, i: (bi, 0, 0))],
        out_specs=pl.BlockSpec((None, FOX_W, FOX_TILE), lambda bi, i: (bi, 0, i)),
        out_shape=jax.ShapeDtypeStruct((b, FOX_W, tq_total), BF16),
        scratch_shapes=[pltpu.VMEM((FOX_HEADS, FOX_TILE), F32), pltpu.VMEM((FOX_HEADS, FOX_TILE), F32),
                        pltpu.VMEM((FOX_W, FOX_TILE), F32), pltpu.VMEM((FOX_LOOKAHEAD, FOX_TILE, FOX_TILE), F32)],
        compiler_params=_cp(("arbitrary", "arbitrary"), 48),
        name="fox_attention",
    )(q_aug, k_aug, v)


def _mlstm_kernel(mq_ref, mk_ref, mv_ref, mo_ref, gt_ref, c0_ref, m0_ref, ng_ref,
                  hm_ref, co_ref, mout_ref, c_sc, m_sc, gt_t, cs_t, *, chunk):
    step = pl.program_id(1)

    @pl.when(step == 0)
    def _():
        c_sc[...] = c0_ref[...]
        m_sc[...] = m0_ref[...]

    ln = chunk
    nt = (((1,), (1,)), ((), ()))
    causal = lax.broadcasted_iota(jnp.int32, (ln, ln), 0) >= lax.broadcasted_iota(jnp.int32, (ln, ln), 1)
    heads = range(MLSTM_HEADS)
    hl = [slice(LANES * h, LANES * (h + 1)) for h in heads]

    g = gt_ref[...]
    cs = _cumsum_rows(g)
    if ln < LANES:
        pad = jnp.zeros((LANES - ln, LANES), F32)
        gt_t[...] = jnp.concatenate([g, pad], axis=0).T
        cs_t[...] = jnp.concatenate([cs, pad], axis=0).T
    else:
        gt_t[...] = g.T
        cs_t[...] = cs.T
    rep = lambda col: jnp.broadcast_to(col, (ln, LANES))
    wide = (lambda r: jnp.concatenate([r] * (ln // LANES), axis=1)) if ln >= LANES else (lambda r: r[:, 0:ln])
    ig = [rep(g[:, _G_IG + h:_G_IG + h + 1]) for h in heads]
    b = [rep(cs[:, _G_LF + h:_G_LF + h + 1]) for h in heads]
    b_last = [x[ln - 1:ln, :] for x in b]

    qk = [lax.dot_general(mq_ref[:, hl[h]], mk_ref[:, hl[h]], nt, preferred_element_type=F32) for h in heads]
    m_loc, a_sum, av, g_max, upd = [], [], [], [], []
    for h in heads:
        d = jnp.where(causal, wide(b[h]) - cs_t[_G_LF + h:_G_LF + h + 1, 0:ln] + gt_t[_G_IG + h:_G_IG + h + 1, 0:ln],
                      -jnp.inf)
        m_loc.append(rep(jnp.max(d, axis=1, keepdims=True)))
        a = qk[h] * jnp.exp(d - wide(m_loc[h]))
        a_sum.append(rep(jnp.sum(a, axis=1, keepdims=True)))
        av.append(jnp.dot(a.astype(BF16), mv_ref[:, hl[h]], preferred_element_type=F32))
    for h in heads:
        g_tok = b_last[h] - b[h] + ig[h]
        g_max.append(jnp.max(g_tok, axis=0, keepdims=True))
        w_loc = jnp.exp(g_tok - g_max[h])
        vw = jnp.concatenate([mv_ref[:, hl[h]].astype(F32) * w_loc, w_loc], axis=1)
        upd.append(lax.dot_general(vw.astype(BF16), mk_ref[:, hl[h]], (((0,), (0,)), ((), ())),
                                   preferred_element_type=F32))

    for h in heads:
        m_prev = m_sc[h:h + 1, :]
        c_prev = c_sc[h]
        cq = lax.dot_general(mq_ref[:, hl[h]], c_prev.astype(BF16), nt, preferred_element_type=F32)
        inter = b[h] + m_prev
        m_t = jnp.maximum(inter, m_loc[h])
        w_inter = jnp.exp(inter - m_t)
        w_intra = jnp.exp(m_loc[h] - m_t)
        num = w_inter * cq[:, 0:MLSTM_DV] + w_intra * av[h]
        den = w_inter * cq[:, MLSTM_DV:] + w_intra * a_sum[h]
        h_cell = num / jnp.maximum(jnp.abs(den), jnp.exp(-m_t))

        m_new = jnp.maximum(b_last[h] + m_prev, g_max[h])
        c_sc[h] = jnp.exp(b_last[h] + m_prev - m_new) * c_prev + jnp.exp(g_max[h] - m_new) * upd[h]
        m_sc[h:h + 1, :] = m_new

        hn = h_cell * lax.rsqrt(jnp.mean(h_cell * h_cell, axis=1, keepdims=True) + EPS)
        hm_ref[:, hl[h]] = (hn * ng_ref[:, hl[h]] * _sigmoid(mo_ref[:, hl[h]].astype(F32))).astype(BF16)

    @pl.when(step == pl.num_programs(1) - 1)
    def _():
        co_ref[...] = c_sc[...]
        mout_ref[...] = m_sc[...]


def _mlstm(mq, mk, mv, mo, gt, c0, m0, ng, *, chunk):
    b, t, _ = mq.shape
    tok = lambda wd: pl.BlockSpec((None, chunk, wd), lambda i, j: (i, j, 0))
    c_spec = pl.BlockSpec((None, MLSTM_HEADS, MLSTM_EXT, LANES), lambda i, j: (i, 0, 0, 0))
    m_spec = pl.BlockSpec((None, SUBLANES, LANES), lambda i, j: (i, 0, 0))
    t_cols = max(chunk, LANES)
    return pl.pallas_call(
        functools.partial(_mlstm_kernel, chunk=chunk),
        grid=(b, t // chunk),
        in_specs=[tok(MLSTM_W), tok(MLSTM_W), tok(MLSTM_W), tok(MLSTM_W), tok(LANES), c_spec, m_spec,
                  pl.BlockSpec((1, MLSTM_W), lambda i, j: (0, 0))],
        out_specs=[tok(MLSTM_W), c_spec, m_spec],
        out_shape=[jax.ShapeDtypeStruct((b, t, MLSTM_W), BF16),
                   jax.ShapeDtypeStruct((b, MLSTM_HEADS, MLSTM_EXT, LANES), F32),
                   jax.ShapeDtypeStruct((b, SUBLANES, LANES), F32)],
        scratch_shapes=[pltpu.VMEM((MLSTM_HEADS, MLSTM_EXT, LANES), F32), pltpu.VMEM((SUBLANES, LANES), F32),
                        pltpu.VMEM((LANES, t_cols), F32), pltpu.VMEM((LANES, t_cols), F32)],
        compiler_params=_cp(("arbitrary", "arbitrary"), 32),
        name="mlstm",
    )(mq, mk, mv, mo, gt, c0, m0, ng)


def _even_out_kernel(x_ref, fox_ref, hm_ref, w_ref, o_ref, *, fox_transposed):
    dims = (((0,), (0,)), ((), ())) if fox_transposed else (((1,), (0,)), ((), ()))
    y = lax.dot_general(fox_ref[...], w_ref[0:FOX_W, :], dims, preferred_element_type=F32)
    y = y + jnp.dot(hm_ref[...], w_ref[FOX_W:, :], preferred_element_type=F32)
    o_ref[...] = x_ref[...] + y


def _even_out(x, fox, hm, w, *, tm, fox_transposed):
    b, t, _ = x.shape
    fox_spec = (pl.BlockSpec((None, FOX_W, tm), lambda i, j: (i, 0, j)) if fox_transposed
                else pl.BlockSpec((None, tm, FOX_W), lambda i, j: (i, j, 0)))
    return pl.pallas_call(
        functools.partial(_even_out_kernel, fox_transposed=fox_transposed),
        grid=(b, t // tm),
        in_specs=[pl.BlockSpec((None, tm, D_MODEL), lambda i, j: (i, j, 0)), fox_spec,
                  pl.BlockSpec((None, tm, MLSTM_W), lambda i, j: (i, j, 0)),
                  pl.BlockSpec((D_MODEL, D_MODEL), lambda i, j: (0, 0))],
        out_specs=pl.BlockSpec((None, tm, D_MODEL), lambda i, j: (i, j, 0)),
        out_shape=jax.ShapeDtypeStruct((b, t, D_MODEL), F32),
        compiler_params=_cp(("arbitrary", "arbitrary"), 32),
        name="even_out",
    )(x, fox, hm, w)


def _cross_kernel(h_ref, g_ref, wq_ref, wo_ref, mk_ref, mv_ref, o_ref):
    x = h_ref[...]
    xn = _rms(x, g_ref[...]).astype(BF16)
    q = jnp.dot(xn, wq_ref[...], preferred_element_type=F32).astype(BF16)
    hls = [slice(MEM_DH * h, MEM_DH * (h + 1)) for h in range(MEM_HEADS)]
    scores = [lax.dot_general(q[:, hl], mk_ref[:, hl], (((1,), (1,)), ((), ())), preferred_element_type=F32) * MEM_SCALE
              for hl in hls]
    outs = []
    for s, hl in zip(scores, hls):
        e = jnp.exp(s - jnp.max(s, axis=1, keepdims=True))
        p = e * (1.0 / jnp.sum(e, axis=1, keepdims=True))
        outs.append(jnp.dot(p.astype(BF16), mv_ref[:, hl], preferred_element_type=F32).astype(BF16))
    o = jnp.concatenate(outs, axis=1)
    o_ref[...] = x + jnp.dot(o, wo_ref[...], preferred_element_type=F32)


def _cross(h, g, wq, wo, mem_k, mem_v, *, tm):
    b, t, _ = h.shape
    tok = pl.BlockSpec((None, tm, D_MODEL), lambda i, j: (i, j, 0))
    mem = pl.BlockSpec((None, MEM_TOKENS, MEM_W), lambda i, j: (i, 0, 0))
    return pl.pallas_call(
        _cross_kernel,
        grid=(b, t // tm),
        in_specs=[tok, pl.BlockSpec((1, D_MODEL), lambda i, j: (0, 0)),
                  pl.BlockSpec((D_MODEL, MEM_W), lambda i, j: (0, 0)),
                  pl.BlockSpec((MEM_W, D_MODEL), lambda i, j: (0, 0)), mem, mem],
        out_specs=tok,
        out_shape=jax.ShapeDtypeStruct((b, t, D_MODEL), F32),
        compiler_params=_cp(("arbitrary", "arbitrary"), 32),
        name="cross_attention",
    )(h, g, wq, wo, mem_k, mem_v)


def _memkv_kernel(mem_ref, g_ref, wk_ref, wv_ref, k_ref, v_ref, kb_ref, vb_ref):
    mn = _rms(mem_ref[...], g_ref[...]).astype(BF16)
    k = jnp.dot(mn, wk_ref[...], preferred_element_type=F32)
    v = jnp.dot(mn, wv_ref[...], preferred_element_type=F32)
    k_ref[...] = k
    v_ref[...] = v
    kb_ref[...] = k.astype(BF16)
    vb_ref[...] = v.astype(BF16)


def _memkv(mem, g, wk, wv, *, tm):
    n = mem.shape[0]
    depth = g.shape[0]
    w_spec = pl.BlockSpec((None, D_MODEL, MEM_W), lambda l, i: (l, 0, 0))
    o_spec = pl.BlockSpec((None, tm, MEM_W), lambda l, i: (l, i, 0))
    return pl.pallas_call(
        _memkv_kernel,
        grid=(depth, n // tm),
        in_specs=[pl.BlockSpec((tm, D_MODEL), lambda l, i: (i, 0)),
                  pl.BlockSpec((None, 1, D_MODEL), lambda l, i: (l, 0, 0)), w_spec, w_spec],
        out_specs=[o_spec] * 4,
        out_shape=[jax.ShapeDtypeStruct((depth, n, MEM_W), dt) for dt in (F32, F32, BF16, BF16)],
        compiler_params=_cp(("arbitrary", "arbitrary"), 32),
        name="memory_kv",
    )(mem, g, wk, wv)


def _silu(x):
    return x * _sigmoid(x)


def _swiglu_chunk(x, wg_ref, wu_ref, wd_ref):
    gate_up = []
    for s in range(FF_TILE // FF_SUB):
        cols = slice(s * FF_SUB, (s + 1) * FF_SUB)
        gate_up.append((jnp.dot(x, wg_ref[:, cols], preferred_element_type=F32),
                        jnp.dot(x, wu_ref[:, cols], preferred_element_type=F32)))
    y = None
    for s, (g, u) in enumerate(gate_up):
        part = jnp.dot((_silu(g) * u).astype(BF16), wd_ref[s * FF_SUB:(s + 1) * FF_SUB, :], preferred_element_type=F32)
        y = part if y is None else y + part
    return y


def _ffn_kernel(h_ref, g_ref, wg_ref, wu_ref, wd_ref, o_ref, xn_sc):
    @pl.when(pl.program_id(1) == 0)
    def _():
        x = h_ref[...]
        xn_sc[...] = _rms(x, g_ref[...]).astype(BF16)
        o_ref[...] = x

    o_ref[...] += _swiglu_chunk(xn_sc[...], wg_ref, wu_ref, wd_ref)


def _ffn(h, g, wg, wu, wd, *, tm):
    n = h.shape[0]
    tok = pl.BlockSpec((tm, D_MODEL), lambda i, j: (i, 0))
    return pl.pallas_call(
        _ffn_kernel,
        grid=(n // tm, D_FF // FF_TILE),
        in_specs=[tok, pl.BlockSpec((1, D_MODEL), lambda i, j: (0, 0)),
                  pl.BlockSpec((D_MODEL, FF_TILE), lambda i, j: (0, j)),
                  pl.BlockSpec((D_MODEL, FF_TILE), lambda i, j: (0, j)),
                  pl.BlockSpec((FF_TILE, D_MODEL), lambda i, j: (j, 0))],
        out_specs=tok,
        out_shape=jax.ShapeDtypeStruct((n, D_MODEL), F32),
        scratch_shapes=[pltpu.VMEM((tm, D_MODEL), BF16)],
        compiler_params=_cp(("arbitrary", "arbitrary"), 48),
        name="dense_swiglu",
    )(h, g, wg, wu, wd)


def _odd_kernel(h_ref, g_ref, win_ref, cw_ref, wout_ref, past_ref, o_ref, st_ref, u_sc):
    j = pl.program_id(1)
    tm = h_ref.shape[0]

    @pl.when(j == 0)
    def _():
        u_sc[0:SUBLANES, :] = past_ref[...]

    @pl.when(j > 0)
    def _():
        u_sc[0:SUBLANES, :] = u_sc[tm:tm + SUBLANES, :]

    x = h_ref[...]
    xn = _rms(x, g_ref[...]).astype(BF16)
    z = jnp.dot(xn, win_ref[...], preferred_element_type=F32)
    gate_b = z[:, 0:D_MODEL]
    u = z[:, D_MODEL:2 * D_MODEL] * z[:, 2 * D_MODEL:]
    u_sc[SUBLANES:, :] = u
    conv = (cw_ref[0:1, :] * u_sc[SUBLANES - 2:SUBLANES - 2 + tm, :]
            + cw_ref[1:2, :] * u_sc[SUBLANES - 1:SUBLANES - 1 + tm, :]
            + cw_ref[2:3, :] * u)
    o_ref[...] = x + jnp.dot((gate_b * conv).astype(BF16), wout_ref[...], preferred_element_type=F32)

    @pl.when(j == pl.num_programs(1) - 1)
    def _():
        st_ref[...] = u_sc[tm:tm + SUBLANES, :]


def _odd(h, g, w_in, cw, w_out, past, *, tm):
    b, t, _ = h.shape
    tok = pl.BlockSpec((None, tm, D_MODEL), lambda i, j: (i, j, 0))
    st = pl.BlockSpec((None, SUBLANES, D_MODEL), lambda i, j: (i, 0, 0))
    return pl.pallas_call(
        _odd_kernel,
        grid=(b, t // tm),
        in_specs=[tok, pl.BlockSpec((1, D_MODEL), lambda i, j: (0, 0)),
                  pl.BlockSpec((D_MODEL, 3 * D_MODEL), lambda i, j: (0, 0)),
                  pl.BlockSpec((SUBLANES, D_MODEL), lambda i, j: (0, 0)),
                  pl.BlockSpec((D_MODEL, D_MODEL), lambda i, j: (0, 0)), st],
        out_specs=[tok, st],
        out_shape=[jax.ShapeDtypeStruct((b, t, D_MODEL), F32), jax.ShapeDtypeStruct((b, SUBLANES, D_MODEL), F32)],
        scratch_shapes=[pltpu.VMEM((tm + SUBLANES, D_MODEL), F32)],
        compiler_params=_cp(("arbitrary", "arbitrary"), 48),
        name="short_conv_mixer",
    )(h, g, w_in, cw, w_out, past)


def _top2(logits, lane_f):
    lg = jnp.where(lane_f < N_EXPERTS, logits, -jnp.inf)
    m1 = jnp.max(lg, axis=1, keepdims=True)
    i1 = jnp.min(jnp.where(lg == m1, lane_f, float(LANES)), axis=1, keepdims=True)
    lg2 = jnp.where(lane_f == i1, -jnp.inf, lg)
    m2 = jnp.max(lg2, axis=1, keepdims=True)
    i2 = jnp.min(jnp.where(lg2 == m2, lane_f, float(LANES)), axis=1, keepdims=True)
    e2 = jnp.exp(m2 - m1)
    inv = 1.0 / (1.0 + e2)
    return i1, i2, inv, e2 * inv


def _route(logits, lane_f):
    i1, i2, g1, g2 = _top2(logits, lane_f)
    return jnp.where(lane_f == i1, g1, 0.0) + jnp.where(lane_f == i2, g2, 0.0)


def _moe_kernel(h_ref, g_ref, rw_ref, wg_ref, wu_ref, wd_ref, fg_ref, o_ref, xn_sc, comb_sc):
    e = pl.program_id(1)
    j = pl.program_id(2)
    tm = h_ref.shape[0]
    lane = lax.broadcasted_iota(jnp.int32, (tm, LANES), 1)

    @pl.when((e == 0) & (j == 0))
    def _():
        x = h_ref[...]
        xn = _rms(x, g_ref[...]).astype(BF16)
        xn_sc[...] = xn
        logits = jnp.dot(xn, rw_ref[...], preferred_element_type=F32)
        comb_sc[...] = _route(logits, lane.astype(F32))
        o_ref[...] = x

    y = _swiglu_chunk(xn_sc[...], wg_ref, wu_ref, wd_ref)
    w_e = jnp.sum(jnp.where(lane == e, comb_sc[...], 0.0), axis=1, keepdims=True)
    o_ref[...] += y * w_e

    @pl.when((e == pl.num_programs(1) - 1) & (j == pl.num_programs(2) - 1))
    def _():
        o_ref[...] = _rms(o_ref[...], fg_ref[...])


def _moe(h, g, rw, wg, wu, wd, fg, *, tm):
    n = h.shape[0]
    tok = pl.BlockSpec((tm, D_MODEL), lambda i, e, j: (i, 0))
    vec = pl.BlockSpec((1, D_MODEL), lambda i, e, j: (0, 0))
    return pl.pallas_call(
        _moe_kernel,
        grid=(n // tm, N_EXPERTS, D_FF // FF_TILE),
        in_specs=[tok, vec, pl.BlockSpec((D_MODEL, LANES), lambda i, e, j: (0, 0)),
                  pl.BlockSpec((None, D_MODEL, FF_TILE), lambda i, e, j: (e, 0, j)),
                  pl.BlockSpec((None, D_MODEL, FF_TILE), lambda i, e, j: (e, 0, j)),
                  pl.BlockSpec((None, FF_TILE, D_MODEL), lambda i, e, j: (e, j, 0)), vec],
        out_specs=tok,
        out_shape=jax.ShapeDtypeStruct((n, D_MODEL), F32),
        scratch_shapes=[pltpu.VMEM((tm, D_MODEL), BF16), pltpu.VMEM((tm, LANES), F32)],
        compiler_params=_cp(("arbitrary", "arbitrary", "arbitrary"), 48),
        name="moe_swiglu",
    )(h, g, rw, wg, wu, wd, fg)


_R_E1, _R_E2, _R_G1, _R_G2, _R_P1, _R_P2 = range(6)


def _router_kernel(h_ref, g_ref, rw_ref, xn_ref, rt_ref, cnt_ref, carry_ref):
    @pl.when(pl.program_id(0) == 0)
    def _():
        carry_ref[...] = jnp.zeros_like(carry_ref)

    xn = _rms(h_ref[...], g_ref[...])
    xn_ref[...] = xn
    tm = xn.shape[0]
    lane = lax.broadcasted_iota(jnp.int32, (tm, LANES), 1)
    lane_f = lane.astype(F32)
    logits = jnp.dot(xn.astype(BF16), rw_ref[...], preferred_element_type=F32)
    i1, i2, g1, g2 = _top2(logits, lane_f)
    sel = jnp.where((lane_f == i1) | (lane_f == i2), 1.0, 0.0)
    incl = _cumsum_rows(sel)
    rank = incl - sel + carry_ref[...]
    p1 = jnp.sum(jnp.where(lane_f == i1, rank, 0.0), axis=1, keepdims=True)
    p2 = jnp.sum(jnp.where(lane_f == i2, rank, 0.0), axis=1, keepdims=True)
    carry_ref[...] = carry_ref[...] + incl[tm - 1:tm, :]
    cnt_ref[...] = carry_ref[...]
    rec = jnp.zeros((tm, LANES), F32)
    for ln, val in ((_R_E1, i1), (_R_E2, i2), (_R_G1, g1), (_R_G2, g2), (_R_P1, p1), (_R_P2, p2)):
        rec = jnp.where(lane == ln, val, rec)
    rt_ref[...] = rec


def _router(h, g, rw, *, tm):
    n = h.shape[0]
    return pl.pallas_call(
        _router_kernel,
        grid=(n // tm,),
        in_specs=[pl.BlockSpec((tm, D_MODEL), lambda i: (i, 0)), pl.BlockSpec((1, D_MODEL), lambda i: (0, 0)),
                  pl.BlockSpec((D_MODEL, LANES), lambda i: (0, 0))],
        out_specs=[pl.BlockSpec((tm, D_MODEL), lambda i: (i, 0)), pl.BlockSpec((tm, LANES), lambda i: (i, 0)),
                   pl.BlockSpec((1, LANES), lambda i: (0, 0))],
        out_shape=[jax.ShapeDtypeStruct((n, D_MODEL), F32), jax.ShapeDtypeStruct((n, LANES), F32),
                   jax.ShapeDtypeStruct((1, LANES), F32)],
        scratch_shapes=[pltpu.VMEM((1, LANES), F32)],
        compiler_params=_cp(("arbitrary",), 32),
        name="moe_router",
    )(h, g, rw)


def _row_copy(src, src_row, dst, dst_row, sem):
    return pltpu.make_async_copy(src.at[pl.ds(src_row, 1), :], dst.at[pl.ds(dst_row, 1), :], sem)


def _dispatch_kernel(ends_ref, dest_ref, x_ref, xs_ref, zero_sc, sem, *, tm_expert):
    tm = x_ref.shape[0]

    @pl.when(pl.program_id(0) == 0)
    def _():
        zero_sc[...] = jnp.zeros_like(zero_sc)

        def zero_tile(first_row, wanted):
            @pl.when(wanted)
            def _():
                cp = pltpu.make_async_copy(zero_sc, xs_ref.at[pl.ds(pl.multiple_of(first_row, tm_expert), tm_expert), :], sem)
                cp.start()
                cp.wait()

        for e in range(N_EXPERTS):
            begin = ends_ref[e - 1] if e else 0
            zero_tile(ends_ref[e] - tm_expert, ends_ref[e] > begin)
        for k in range(N_EXPERTS):
            first_row = ends_ref[N_EXPERTS - 1] + k * tm_expert
            zero_tile(first_row, first_row < xs_ref.shape[0])

    def start(t, c):
        _row_copy(x_ref, t, xs_ref, dest_ref[0, 2 * t], sem).start()
        _row_copy(x_ref, t, xs_ref, dest_ref[0, 2 * t + 1], sem).start()
        return c

    def wait(t, c):
        _row_copy(x_ref, t, xs_ref, dest_ref[0, 2 * t], sem).wait()
        _row_copy(x_ref, t, xs_ref, dest_ref[0, 2 * t + 1], sem).wait()
        return c

    lax.fori_loop(0, tm, start, 0, unroll=MOE_DMA_UNROLL)
    lax.fori_loop(0, tm, wait, 0, unroll=MOE_DMA_UNROLL)


def _dispatch(ends, dest, xn, *, rows, tm, tm_expert):
    n = xn.shape[0]
    return pl.pallas_call(
        functools.partial(_dispatch_kernel, tm_expert=tm_expert),
        grid_spec=pltpu.PrefetchScalarGridSpec(
            num_scalar_prefetch=1,
            grid=(n // tm,),
            in_specs=[pl.BlockSpec((None, 1, 2 * tm), lambda i, ends: (i, 0, 0), memory_space=pltpu.SMEM),
                      pl.BlockSpec((tm, D_MODEL), lambda i, ends: (i, 0))],
            out_specs=pl.BlockSpec(memory_space=pl.ANY),
            scratch_shapes=[pltpu.VMEM((tm_expert, D_MODEL), F32), pltpu.SemaphoreType.DMA(())]),
        out_shape=jax.ShapeDtypeStruct((rows, D_MODEL), F32),
        compiler_params=_cp(("arbitrary",), 32),
        name="moe_dispatch",
    )(ends, dest.reshape(n // tm, 1, 2 * tm), xn)


def _expert_kernel(te_ref, nv_ref, x_ref, wg_ref, wu_ref, wd_ref, o_ref, xb_sc):
    n_valid = nv_ref[pl.program_id(0)]

    @pl.when(pl.program_id(1) == 0)
    def _():
        xb_sc[...] = x_ref[...].astype(BF16)
        o_ref[...] = jnp.zeros_like(o_ref)

    @pl.when(n_valid > 0)
    def _():
        o_ref[...] += _swiglu_chunk(xb_sc[...], wg_ref, wu_ref, wd_ref)


def _experts(tile_expert, tile_valid, xs, wg, wu, wd, *, tm):
    rows = xs.shape[0]
    tok = pl.BlockSpec((tm, D_MODEL), lambda i, j, te, nv: (i, 0))
    return pl.pallas_call(
        _expert_kernel,
        grid_spec=pltpu.PrefetchScalarGridSpec(
            num_scalar_prefetch=2,
            grid=(rows // tm, D_FF // FF_TILE),
            in_specs=[tok,
                      pl.BlockSpec((None, D_MODEL, FF_TILE), lambda i, j, te, nv: (te[i], 0, j)),
                      pl.BlockSpec((None, D_MODEL, FF_TILE), lambda i, j, te, nv: (te[i], 0, j)),
                      pl.BlockSpec((None, FF_TILE, D_MODEL), lambda i, j, te, nv: (te[i], j, 0))],
            out_specs=tok,
            scratch_shapes=[pltpu.VMEM((tm, D_MODEL), BF16)]),
        out_shape=jax.ShapeDtypeStruct((rows, D_MODEL), F32),
        compiler_params=_cp(("arbitrary", "arbitrary"), 48),
        name="moe_experts",
    )(tile_expert, tile_valid, xs, wg, wu, wd)


def _combine_kernel(dest_ref, h_ref, rt_ref, fg_ref, ys_ref, o_ref, y1_sc, y2_sc, sem):
    tm = h_ref.shape[0]

    def start(t, c):
        _row_copy(ys_ref, dest_ref[0, 2 * t], y1_sc, t, sem).start()
        _row_copy(ys_ref, dest_ref[0, 2 * t + 1], y2_sc, t, sem).start()
        return c

    def wait(t, c):
        _row_copy(ys_ref, dest_ref[0, 2 * t], y1_sc, t, sem).wait()
        _row_copy(ys_ref, dest_ref[0, 2 * t + 1], y2_sc, t, sem).wait()
        return c

    lax.fori_loop(0, tm, start, 0, unroll=MOE_DMA_UNROLL)
    lax.fori_loop(0, tm, wait, 0, unroll=MOE_DMA_UNROLL)
    rt = rt_ref[...]
    moe = rt[:, _R_G1:_R_G1 + 1] * y1_sc[...] + rt[:, _R_G2:_R_G2 + 1] * y2_sc[...]
    o_ref[...] = _rms(h_ref[...] + moe, fg_ref[...])


def _combine(dest, h, rt, fg, ys, *, tm):
    n = h.shape[0]
    tok = pl.BlockSpec((tm, D_MODEL), lambda i: (i, 0))
    return pl.pallas_call(
        _combine_kernel,
        grid=(n // tm,),
        in_specs=[pl.BlockSpec((None, 1, 2 * tm), lambda i: (i, 0, 0), memory_space=pltpu.SMEM),
                  tok, pl.BlockSpec((tm, LANES), lambda i: (i, 0)), pl.BlockSpec((1, D_MODEL), lambda i: (0, 0)),
                  pl.BlockSpec(memory_space=pl.ANY)],
        out_specs=tok,
        out_shape=jax.ShapeDtypeStruct((n, D_MODEL), F32),
        scratch_shapes=[pltpu.VMEM((tm, D_MODEL), F32), pltpu.VMEM((tm, D_MODEL), F32), pltpu.SemaphoreType.DMA(())],
        compiler_params=_cp(("arbitrary",), 32),
        name="moe_combine",
    )(dest.reshape(n // tm, 1, 2 * tm), h, rt, fg, ys)


def _moe_routed(h, g, rw, wg, wu, wd, fg, *, tm, tm_expert):
    n = h.shape[0]
    xn, rt, counts = _router(h, g, rw, tm=tm)
    counts = counts[0, :N_EXPERTS].astype(jnp.int32)
    group = (counts + tm_expert - 1) // tm_expert * tm_expert
    ends = jnp.cumsum(group)
    starts = ends - group
    e = rt[:, _R_E1:_R_E2 + 1].astype(jnp.int32)
    dest = (starts[e] + rt[:, _R_P1:_R_P2 + 1].astype(jnp.int32)).reshape(2 * n)
    n_tiles = (TOP_K * n) // tm_expert + N_EXPERTS
    tile_start = jnp.arange(n_tiles, dtype=jnp.int32) * tm_expert
    active = tile_start < ends[-1]
    last_active = ends[-1] // tm_expert - 1
    probe = jnp.minimum(tile_start, last_active * tm_expert)
    tile_expert = jnp.sum((probe[:, None] >= ends[None, :]).astype(jnp.int32), axis=1)
    tile_valid = jnp.where(active, jnp.clip(starts[tile_expert] + counts[tile_expert] - tile_start, 0, tm_expert), 0)
    xs = _dispatch(ends.astype(jnp.int32), dest, xn, rows=n_tiles * tm_expert, tm=tm, tm_expert=tm_expert)
    ys = _experts(tile_expert, tile_valid.astype(jnp.int32), xs, wg, wu, wd, tm=tm_expert)
    return _combine(dest, h, rt, fg, ys, tm=tm)


def _pack_params(p):
    fq, fk, fv, ff, mq, mk, mv, mo, mi, mf = jnp.split(p['even_w_in'][0], list(EVEN_SPLITS), axis=1)
    gate_w = jnp.concatenate([ff, mi, mf, jnp.zeros((D_MODEL, LANES - _G_END), F32)], axis=1)
    gate_b = jnp.concatenate([p['fox_b_f'][0], p['mlstm_b_i'][0], p['mlstm_b_f'][0], jnp.zeros((LANES - _G_END,), F32)])
    row = lambda a: a.reshape(1, -1).astype(F32)
    return dict(
        even_w=jnp.concatenate([fq, fk, fv, mq, mk, mv, mo, gate_w], axis=1).astype(BF16),
        even_b=gate_b.reshape(1, LANES),
        even_w_out=p['even_w_out'][0].astype(BF16),
        mlstm_norm_g=row(p['mlstm_norm_g'][0]),
        norm_mix_g=[row(p['norm_mix_g'][l]) for l in range(2)],
        norm_cross_g=[row(p['norm_cross_g'][l]) for l in range(2)],
        norm_ffn_g=[row(p['norm_ffn_g'][l]) for l in range(2)],
        final_norm_g=row(p['final_norm_g']),
        mem_wq=p['mem_wq'].astype(BF16), mem_wo=p['mem_wo'].astype(BF16),
        ffn_wg=p['ffn_w_gate'][0].astype(BF16), ffn_wu=p['ffn_w_up'][0].astype(BF16), ffn_wd=p['ffn_w_down'][0].astype(BF16),
        odd_w_in=p['odd_w_in'][0].astype(BF16), odd_w_out=p['odd_w_out'][0].astype(BF16),
        conv_w=jnp.concatenate([p['conv_w'][0], jnp.zeros((SUBLANES - CONV_W, D_MODEL), F32)], axis=0),
        router_w=jnp.concatenate([p['router_w'][0], jnp.zeros((D_MODEL, LANES - N_EXPERTS), F32)], axis=1).astype(BF16),
        moe_wg=p['moe_w_gate'][0].astype(BF16), moe_wu=p['moe_w_up'][0].astype(BF16), moe_wd=p['moe_w_down'][0].astype(BF16),
    )


def _trunk(x, mem_k, mem_v, caches, w, *, tm, tm_wide, chunk):
    b, t, _ = x.shape
    n = b * t
    flat = lambda a: a.reshape(n, a.shape[-1])
    per_b = lambda a: a.reshape(b, t, a.shape[-1])

    if caches is None:
        f0 = jnp.zeros((b, 1, LANES), F32)
    else:
        cache_k, cache_v, cache_lf, c0, n0, m0, conv_st = caches
        past_len = cache_k.shape[1]
        lf_pad = jnp.pad(cache_lf.astype(F32), ((0, 0), (0, 0), (0, LANES - FOX_HEADS)))
        ka_cache, f0 = _cache_prep(cache_k.reshape(b, past_len, FOX_W).astype(F32), lf_pad, tm=FOX_TILE)
    qa, ka, fk, fv, vb, mq, mk, mv, mo, gt = _even_in(flat(x), w['norm_mix_g'][0], w['even_w'], w['even_b'], f0,
                                                      tm=tm, tiles_per_batch=t // tm)
    if caches is None:
        fox_t = _fox_attention(per_b(qa), per_b(ka), per_b(vb), q0=0)
        c_ext0 = jnp.zeros((b, MLSTM_HEADS, MLSTM_EXT, LANES), F32)
        m_ext0 = jnp.zeros((b, SUBLANES, LANES), F32)
        past = jnp.zeros((b, SUBLANES, D_MODEL), F32)
    else:
        pad_t = lambda a: jnp.pad(per_b(a), ((0, 0), (0, FOX_TILE - t), (0, 0)))
        k_all = jnp.concatenate([ka_cache, pad_t(ka)], axis=1)
        v_all = jnp.concatenate([cache_v.reshape(b, past_len, FOX_W).astype(BF16), pad_t(vb)], axis=1)
        fox_full = _fox_attention(pad_t(qa), k_all, v_all, q0=past_len // FOX_TILE)
        fox_t = jnp.swapaxes(fox_full[:, :, :t], 1, 2)
        c_ext0 = jnp.concatenate([
            jnp.pad(c0.astype(F32), ((0, 0), (0, 0), (0, 0), (0, LANES - MLSTM_DK))),
            jnp.broadcast_to(jnp.pad(n0.astype(F32), ((0, 0), (0, 0), (0, LANES - MLSTM_DK)))[:, :, None, :],
                             (b, MLSTM_HEADS, MLSTM_EXT - MLSTM_DV, LANES))],
            axis=2)
        m_ext0 = jnp.broadcast_to(jnp.pad(m0.astype(F32), ((0, 0), (0, SUBLANES - MLSTM_HEADS)))[:, :, None],
                                  (b, SUBLANES, LANES))
        past = jnp.pad(conv_st.astype(F32), ((0, 0), (SUBLANES - (CONV_W - 1), 0), (0, 0)))
    hm, c_ext, m_ext = _mlstm(per_b(mq), per_b(mk), per_b(mv), per_b(mo), per_b(gt), c_ext0, m_ext0,
                              w['mlstm_norm_g'], chunk=chunk)
    h = _even_out(x, fox_t, hm, w['even_w_out'], tm=tm, fox_transposed=caches is None)
    h = _cross(h, w['norm_cross_g'][0], w['mem_wq'][0], w['mem_wo'][0], mem_k[0], mem_v[0], tm=tm)
    h = _ffn(flat(h), w['norm_ffn_g'][0], w['ffn_wg'], w['ffn_wu'], w['ffn_wd'], tm=tm_wide)

    h, conv_new = _odd(per_b(h), w['norm_mix_g'][1], w['odd_w_in'], w['conv_w'], w['odd_w_out'], past, tm=tm)
    h = _cross(h, w['norm_cross_g'][1], w['mem_wq'][1], w['mem_wo'][1], mem_k[1], mem_v[1], tm=tm)
    moe_args = (flat(h), w['norm_ffn_g'][1], w['router_w'], w['moe_wg'], w['moe_wu'], w['moe_wd'], w['final_norm_g'])
    if n * TOP_K >= N_EXPERTS * MOE_EXPERT_TILE:
        y = _moe_routed(*moe_args, tm=MOE_TOKEN_TILE, tm_expert=MOE_EXPERT_TILE)
    else:
        y = _moe(*moe_args, tm=tm_wide)

    states = (
        fk.reshape(1, b, t, FOX_HEADS, FOX_DH), fv.reshape(1, b, t, FOX_HEADS, FOX_DH),
        per_b(gt)[None, :, :, _G_FOX:_G_IG],
        c_ext[None, :, :, 0:MLSTM_DV, 0:MLSTM_DK], c_ext[None, :, :, MLSTM_DV, 0:MLSTM_DK], m_ext[None, :, 0:MLSTM_HEADS, 0],
        conv_new[None, :, SUBLANES - (CONV_W - 1):, :],
    )
    return per_b(y), states


def kernel(x_prompt, x_sample, mem_prompt, cache_fox_k, cache_fox_v, cache_fox_logf, state_mlstm_c, state_mlstm_n, state_mlstm_m, state_conv, cache_mem_k, cache_mem_v, norm_mix_g, norm_mem_g, norm_cross_g, norm_ffn_g, final_norm_g, even_w_in, fox_b_f, mlstm_b_i, mlstm_b_f, mlstm_norm_g, even_w_out, odd_w_in, conv_w, odd_w_out, mem_wq, mem_wk, mem_wv, mem_wo, ffn_w_gate, ffn_w_up, ffn_w_down, router_w, moe_w_gate, moe_w_up, moe_w_down):
    w = _pack_params(dict(
        norm_mix_g=norm_mix_g, norm_cross_g=norm_cross_g, norm_ffn_g=norm_ffn_g, final_norm_g=final_norm_g,
        even_w_in=even_w_in, fox_b_f=fox_b_f, mlstm_b_i=mlstm_b_i, mlstm_b_f=mlstm_b_f, mlstm_norm_g=mlstm_norm_g,
        even_w_out=even_w_out, odd_w_in=odd_w_in, conv_w=conv_w, odd_w_out=odd_w_out, mem_wq=mem_wq, mem_wo=mem_wo,
        ffn_w_gate=ffn_w_gate, ffn_w_up=ffn_w_up, ffn_w_down=ffn_w_down, router_w=router_w,
        moe_w_gate=moe_w_gate, moe_w_up=moe_w_up, moe_w_down=moe_w_down))

    bp, tp, _ = x_prompt.shape
    bs, ts, _ = x_sample.shape
    depth = norm_mem_g.shape[0]

    mem_k_p, mem_v_p, mem_kb, mem_vb = _memkv(mem_prompt.reshape(bp * MEM_TOKENS, D_MODEL),
                                              norm_mem_g.reshape(depth, 1, D_MODEL).astype(F32),
                                              mem_wk.astype(BF16), mem_wv.astype(BF16), tm=512)
    per_layer = lambda a, nb: a.reshape(depth, nb, MEM_TOKENS, MEM_W)
    y_prompt, st_p = _trunk(x_prompt, per_layer(mem_kb, bp), per_layer(mem_vb, bp), None, w,
                            tm=512, tm_wide=1024, chunk=MLSTM_CHUNK)

    caches = (cache_fox_k[0], cache_fox_v[0], cache_fox_logf[0], state_mlstm_c[0], state_mlstm_n[0],
              state_mlstm_m[0], state_conv[0])
    y_sample, st_s = _trunk(x_sample, per_layer(cache_mem_k.astype(BF16), bs), per_layer(cache_mem_v.astype(BF16), bs),
                            caches, w, tm=ts, tm_wide=bs * ts, chunk=ts)

    mem_shape = (depth, bp, MEM_TOKENS, MEM_HEADS, MEM_DH)
    return (y_prompt, y_sample) + st_p + (mem_k_p.reshape(mem_shape), mem_v_p.reshape(mem_shape)) + st_s
```

```python
import functools

import jax
import jax.numpy as jnp
from jax import lax
from jax.experimental import pallas as pl
from jax.experimental.pallas import tpu as pltpu

F32 = jnp.float32
BF16 = jnp.bfloat16

D_MODEL = 1024
EPS = 1e-6
FOX_HEADS = 8
FOX_DH = 64
FOX_W = FOX_HEADS * FOX_DH
FOX_SCALE = FOX_DH ** -0.5
MLSTM_HEADS = 4
MLSTM_DV = 128
MLSTM_DK = 64
MLSTM_W = MLSTM_HEADS * MLSTM_DV
MLSTM_QK_W = MLSTM_HEADS * MLSTM_DK
MLSTM_SCALE = MLSTM_DK ** -0.5
MLSTM_CHUNK = 256
MLSTM_EXT = 256
MEM_TOKENS = 256
MEM_HEADS = 4
MEM_DH = 128
MEM_W = MEM_HEADS * MEM_DH
MEM_SCALE = MEM_DH ** -0.5
D_FF = 3584
N_EXPERTS = 8
TOP_K = 2
MOE_EXPERT_TILE = 1024
MOE_TOKEN_TILE = 512
MOE_DMA_UNROLL = 8
CONV_W = 3
EVEN_SIZES = (FOX_W, FOX_W, FOX_W, FOX_HEADS, MLSTM_QK_W, MLSTM_QK_W, MLSTM_W, MLSTM_W, MLSTM_HEADS, MLSTM_HEADS)
EVEN_SPLITS = tuple(sum(EVEN_SIZES[:i + 1]) for i in range(len(EVEN_SIZES) - 1))

LANES = 128
SUBLANES = 8
FOX_AUG_W = FOX_HEADS * LANES
FOX_TILE = 256
FOX_LOOKAHEAD = 3
FOX_PV_DELAY = 2
FF_TILE = 512
FF_SUB = 256
_C_Q, _C_K, _C_V, _C_MQ, _C_MK, _C_MV, _C_MO, _C_G, _C_END = 0, 512, 1024, 1536, 1792, 2048, 2560, 3072, 3200
_G_FOX, _G_IG, _G_LF, _G_END = 0, 8, 12, 16
NEG = -1e30
LOG2E = 1.4426950408889634


def _cp(sem, vmem_mb):
    return pltpu.CompilerParams(dimension_semantics=sem, vmem_limit_bytes=vmem_mb * 1024 * 1024)


def _rms(x, g):
    return x * lax.rsqrt(jnp.mean(x * x, axis=-1, keepdims=True) + EPS) * g


def _log_sigmoid(x):
    return jnp.minimum(x, 0.0) - jnp.log1p(jnp.exp(-jnp.abs(x)))


def _sigmoid(x):
    return 1.0 / (1.0 + jnp.exp(-x))


def _cumsum_rows(x):
    n = x.shape[0]
    row = lax.broadcasted_iota(jnp.int32, x.shape, 0)
    s = 1
    while s < n:
        x = x + jnp.where(row >= s, pltpu.roll(x, s, axis=0), 0.0)
        s *= 2
    return x


def _split3(f):
    hi = f.astype(BF16).astype(F32)
    r = f - hi
    mid = r.astype(BF16).astype(F32)
    return hi, mid, r - mid


def _head_block(src, h, lane):
    p, odd = divmod(h, 2)
    blk = src[:, LANES * p:LANES * (p + 1)]
    return pltpu.roll(blk, FOX_DH, axis=1) if odd else blk


_AUG_Q = (64, 72, 80)
_AUG_K = (88, 96, 104)


def _fox_bias_lanes(cum_f, lane):
    parts = _split3(jnp.where(lane < FOX_HEADS, cum_f * LOG2E, 0.0))
    q_side = sum(pltpu.roll(p, off, axis=1) for p, off in zip(parts, _AUG_Q))
    k_side = sum(pltpu.roll(-p, off, axis=1) for p, off in zip(parts, _AUG_K))
    return q_side, k_side


def _fox_aug(src, bias, h, is_query):
    lane = lax.broadcasted_iota(jnp.int32, (1, LANES), 1)
    blk = _head_block(src, h, lane)
    q_lanes = (lane == _AUG_Q[0] + h) | (lane == _AUG_Q[1] + h) | (lane == _AUG_Q[2] + h)
    k_lanes = (lane == _AUG_K[0] + h) | (lane == _AUG_K[1] + h) | (lane == _AUG_K[2] + h)
    own, ones = (q_lanes, k_lanes) if is_query else (k_lanes, q_lanes)
    aug = jnp.where(own, bias, jnp.where(ones, 1.0, 0.0))
    return jnp.where(lane < FOX_DH, blk, aug).astype(BF16)


def _even_in_kernel(x_ref, g_ref, w_ref, b_ref, f0_ref,
                    qa_ref, ka_ref, fk_ref, fv_ref, vb_ref, mq_ref, mk_ref, mv_ref, mo_ref, gt_ref,
                    carry_ref, *, tiles_per_batch):
    @pl.when(pl.program_id(0) % tiles_per_batch == 0)
    def _():
        carry_ref[...] = f0_ref[...]

    xn = _rms(x_ref[...], g_ref[...]).astype(BF16)
    acc = jnp.dot(xn, w_ref[...], preferred_element_type=F32)
    tm = acc.shape[0]
    lane = lax.broadcasted_iota(jnp.int32, (tm, LANES), 1)

    gates = acc[:, _C_G:_C_END] + b_ref[...]
    ls = _log_sigmoid(gates)
    is_ig = (lane >= _G_IG) & (lane < _G_LF)
    gt_ref[...] = jnp.where(is_ig, gates, jnp.where(lane < _G_END, ls, 0.0))
    cum_f = _cumsum_rows(jnp.where(lane < _G_IG, ls, 0.0)) + carry_ref[...]
    carry_ref[...] = cum_f[tm - 1:tm, :]

    q_all = acc[:, _C_Q:_C_K] * (FOX_SCALE * LOG2E)
    k_all = acc[:, _C_K:_C_V]
    v_all = acc[:, _C_V:_C_MQ]
    q_bias, k_bias = _fox_bias_lanes(cum_f, lane)
    for h in range(FOX_HEADS):
        qa_ref[:, LANES * h:LANES * (h + 1)] = _fox_aug(q_all, q_bias, h, True)
        ka_ref[:, LANES * h:LANES * (h + 1)] = _fox_aug(k_all, k_bias, h, False)
    for h in range(FOX_HEADS):
        fk_ref[:, h, :] = _head_block(k_all, h, lane)[:, 0:FOX_DH]
        fv_ref[:, h, :] = _head_block(v_all, h, lane)[:, 0:FOX_DH]
    vb_ref[...] = v_all.astype(BF16)

    mq_all = acc[:, _C_MQ:_C_MK] * MLSTM_SCALE
    mk_all = acc[:, _C_MK:_C_MV]
    for h in range(MLSTM_HEADS):
        mq_ref[:, LANES * h:LANES * (h + 1)] = jnp.where(lane < MLSTM_DK, _head_block(mq_all, h, lane), 0.0).astype(BF16)
        mk_ref[:, LANES * h:LANES * (h + 1)] = jnp.where(lane < MLSTM_DK, _head_block(mk_all, h, lane), 0.0).astype(BF16)
    mv_ref[...] = acc[:, _C_MV:_C_MO].astype(BF16)
    mo_ref[...] = acc[:, _C_MO:_C_G].astype(BF16)


def _even_in(x, g, w, bias, f0, *, tm, tiles_per_batch):
    n = x.shape[0]
    row = lambda i: (i, 0)
    fixed = lambda i: (0, 0)
    head_major = (FOX_HEADS, FOX_DH)
    widths = (FOX_AUG_W, FOX_AUG_W, head_major, head_major, FOX_W, MLSTM_W, MLSTM_W, MLSTM_W, MLSTM_W, LANES)
    dtypes = (BF16, BF16, F32, F32, BF16, BF16, BF16, BF16, BF16, F32)
    spec = lambda wd: (pl.BlockSpec((tm,) + wd, lambda i: (i, 0, 0)) if isinstance(wd, tuple) else pl.BlockSpec((tm, wd), row))
    shape = lambda wd: (n,) + wd if isinstance(wd, tuple) else (n, wd)
    return pl.pallas_call(
        functools.partial(_even_in_kernel, tiles_per_batch=tiles_per_batch),
        grid=(n // tm,),
        in_specs=[pl.BlockSpec((tm, D_MODEL), row), pl.BlockSpec((1, D_MODEL), fixed),
                  pl.BlockSpec((D_MODEL, _C_END), fixed), pl.BlockSpec((1, LANES), fixed),
                  pl.BlockSpec((None, 1, LANES), lambda i: (i // tiles_per_batch, 0, 0))],
        out_specs=[spec(wd) for wd in widths],
        out_shape=[jax.ShapeDtypeStruct(shape(wd), dt) for wd, dt in zip(widths, dtypes)],
        scratch_shapes=[pltpu.VMEM((1, LANES), F32)],
        compiler_params=_cp(("arbitrary",), 48),
        name="even_in",
    )(x, g, w, bias, f0)


def _cache_prep_kernel(k_ref, lf_ref, ka_ref, fend_ref, carry_ref):
    @pl.when(pl.program_id(1) == 0)
    def _():
        carry_ref[...] = jnp.zeros_like(carry_ref)

    k_all = k_ref[...]
    tm = k_all.shape[0]
    lane = lax.broadcasted_iota(jnp.int32, (tm, LANES), 1)
    cum_f = _cumsum_rows(lf_ref[...]) + carry_ref[...]
    carry_ref[...] = cum_f[tm - 1:tm, :]
    fend_ref[...] = cum_f[tm - 1:tm, :]
    _, k_bias = _fox_bias_lanes(cum_f, lane)
    for h in range(FOX_HEADS):
        ka_ref[:, LANES * h:LANES * (h + 1)] = _fox_aug(k_all, k_bias, h, False)


def _cache_prep(cache_k, cache_lf, *, tm):
    b, p, _ = cache_k.shape
    return pl.pallas_call(
        _cache_prep_kernel,
        grid=(b, p // tm),
        in_specs=[pl.BlockSpec((None, tm, FOX_W), lambda i, j: (i, j, 0)),
                  pl.BlockSpec((None, tm, LANES), lambda i, j: (i, j, 0))],
        out_specs=[pl.BlockSpec((None, tm, FOX_AUG_W), lambda i, j: (i, j, 0)),
                   pl.BlockSpec((None, 1, LANES), lambda i, j: (i, 0, 0))],
        out_shape=[jax.ShapeDtypeStruct((b, p, FOX_AUG_W), BF16), jax.ShapeDtypeStruct((b, 1, LANES), F32)],
        scratch_shapes=[pltpu.VMEM((1, LANES), F32)],
        compiler_params=_cp(("arbitrary", "arbitrary"), 32),
        name="fox_cache_prep",
    )(cache_k, cache_lf)


def _fox_kernel(q_ref, k_ref, v_ref, o_ref, m_sc, l_sc, acc_sc, s_sc, *, q0):
    t = FOX_TILE
    n_full = q0 + pl.program_id(1)
    visible = (lax.broadcasted_iota(jnp.int32, (t, t), 0) <= lax.broadcasted_iota(jnp.int32, (t, t), 1))
    m_sc[...] = jnp.full_like(m_sc, NEG)
    l_sc[...] = jnp.zeros_like(l_sc)
    acc_sc[...] = jnp.zeros_like(acc_sc)

    def scores(j, h):
        hl = slice(LANES * h, LANES * (h + 1))
        return lax.dot_general(k_ref[pl.ds(pl.multiple_of(j * t, t), t), hl], q_ref[:, hl], (((1,), (1,)), ((), ())),
                               preferred_element_type=F32)

    def step(j, diagonal):
        off = pl.multiple_of(j * t, t)

        def weighted_values(h, p, alpha):
            pair, hh = divmod(h, 2)
            v_t = v_ref[pl.ds(off, t), LANES * pair:LANES * (pair + 1)].astype(BF16).T
            v_ext = jnp.concatenate([v_t[FOX_DH * hh:FOX_DH * (hh + 1), :], jnp.ones((2 * SUBLANES, t), BF16)], axis=0)
            pv = jnp.dot(v_ext, p, preferred_element_type=F32)
            rows = slice(FOX_DH * h, FOX_DH * (h + 1))
            acc_sc[rows, :] = alpha * acc_sc[rows, :] + pv[0:FOX_DH, :]
            l_sc[h:h + 1, :] = alpha * l_sc[h:h + 1, :] + pv[FOX_DH:FOX_DH + 1, :]

        pending = {h: s_sc[h] for h in range(FOX_LOOKAHEAD)}
        ready = {}
        for h in range(FOX_HEADS):
            s = pending.pop(h)
            if diagonal:
                s = jnp.where(visible, s, NEG)
            m_old = m_sc[h:h + 1, :]
            m_new = jnp.maximum(m_old, jnp.max(s, axis=0, keepdims=True))
            alpha = jnp.exp2(m_old - m_new)
            p = jnp.exp2(s - m_new)
            m_sc[h:h + 1, :] = m_new
            ready[h] = (p.astype(BF16), alpha)
            ahead = h + FOX_LOOKAHEAD
            if ahead < FOX_HEADS:
                pending[ahead] = scores(j, ahead)
            elif not diagonal:
                s_sc[ahead - FOX_HEADS] = scores(j + 1, ahead - FOX_HEADS)
            if h - FOX_PV_DELAY in ready:
                weighted_values(h - FOX_PV_DELAY, *ready.pop(h - FOX_PV_DELAY))
        for h in sorted(ready):
            weighted_values(h, *ready[h])

    def body(j, c):
        step(j, False)
        return c

    for h in range(FOX_LOOKAHEAD):
        s_sc[h] = scores(0, h)
    lax.fori_loop(0, n_full, body, 0)
    step(n_full, True)
    for h in range(FOX_HEADS):
        rows = slice(FOX_DH * h, FOX_DH * (h + 1))
        o_ref[rows, :] = (acc_sc[rows, :] / l_sc[h:h + 1, :]).astype(BF16)


def _fox_attention(q_aug, k_aug, v, *, q0):
    b, tq_total, _ = q_aug.shape
    t_kv = k_aug.shape[1]
    nq = tq_total // FOX_TILE
    return pl.pallas_call(
        functools.partial(_fox_kernel, q0=q0),
        grid=(b, nq),
        in_specs=[pl.BlockSpec((None, FOX_TILE, FOX_AUG_W), lambda bi, i: (bi, i, 0)),
                  pl.BlockSpec((None, t_kv, FOX_AUG_W), lambda bi, i: (bi, 0, 0)),
                  pl.BlockSpec((None, t_kv, FOX_W), lambda bi, i: (bi, 0, 0))],
        out_specs=pl.BlockSpec((None, FOX_W, FOX_TILE), lambda bi, i: (bi, 0, i)),
        out_shape=jax.ShapeDtypeStruct((b, FOX_W, tq_total), BF16),
        scratch_shapes=[pltpu.VMEM((FOX_HEADS, FOX_TILE), F32), pltpu.VMEM((FOX_HEADS, FOX_TILE), F32),
                        pltpu.VMEM((FOX_W, FOX_TILE), F32), pltpu.VMEM((FOX_LOOKAHEAD, FOX_TILE, FOX_TILE), F32)],
        compiler_params=_cp(("arbitrary", "arbitrary"), 48),
        name="fox_attention",
    )(q_aug, k_aug, v)


def _mlstm_kernel(mq_ref, mk_ref, mv_ref, mo_ref, gt_ref, c0_ref, m0_ref, ng_ref,
                  hm_ref, co_ref, mout_ref, c_sc, m_sc, gt_t, cs_t, *, chunk):
    step = pl.program_id(1)

    @pl.when(step == 0)
    def _():
        c_sc[...] = c0_ref[...]
        m_sc[...] = m0_ref[...]

    ln = chunk
    nt = (((1,), (1,)), ((), ()))
    causal = lax.broadcasted_iota(jnp.int32, (ln, ln), 0) >= lax.broadcasted_iota(jnp.int32, (ln, ln), 1)
    heads = range(MLSTM_HEADS)
    hl = [slice(LANES * h, LANES * (h + 1)) for h in heads]

    g = gt_ref[...]
    cs = _cumsum_rows(g)
    if ln < LANES:
        pad = jnp.zeros((LANES - ln, LANES), F32)
        gt_t[...] = jnp.concatenate([g, pad], axis=0).T
        cs_t[...] = jnp.concatenate([cs, pad], axis=0).T
    else:
        gt_t[...] = g.T
        cs_t[...] = cs.T
    rep = lambda col: jnp.broadcast_to(col, (ln, LANES))
    wide = (lambda r: jnp.concatenate([r] * (ln // LANES), axis=1)) if ln >= LANES else (lambda r: r[:, 0:ln])
    ig = [rep(g[:, _G_IG + h:_G_IG + h + 1]) for h in heads]
    b = [rep(cs[:, _G_LF + h:_G_LF + h + 1]) for h in heads]
    b_last = [x[ln - 1:ln, :] for x in b]

    qk = [lax.dot_general(mq_ref[:, hl[h]], mk_ref[:, hl[h]], nt, preferred_element_type=F32) for h in heads]
    m_loc, a_sum, av, g_max, upd = [], [], [], [], []
    for h in heads:
        d = jnp.where(causal, wide(b[h]) - cs_t[_G_LF + h:_G_LF + h + 1, 0:ln] + gt_t[_G_IG + h:_G_IG + h + 1, 0:ln],
                      -jnp.inf)
        m_loc.append(rep(jnp.max(d, axis=1, keepdims=True)))
        a = qk[h] * jnp.exp(d - wide(m_loc[h]))
        a_sum.append(rep(jnp.sum(a, axis=1, keepdims=True)))
        av.append(jnp.dot(a.astype(BF16), mv_ref[:, hl[h]], preferred_element_type=F32))
    for h in heads:
        g_tok = b_last[h] - b[h] + ig[h]
        g_max.append(jnp.max(g_tok, axis=0, keepdims=True))
        w_loc = jnp.exp(g_tok - g_max[h])
        vw = jnp.concatenate([mv_ref[:, hl[h]].astype(F32) * w_loc, w_loc], axis=1)
        upd.append(lax.dot_general(vw.astype(BF16), mk_ref[:, hl[h]], (((0,), (0,)), ((), ())),
                                   preferred_element_type=F32))

    for h in heads:
        m_prev = m_sc[h:h + 1, :]
        c_prev = c_sc[h]
        cq = lax.dot_general(mq_ref[:, hl[h]], c_prev.astype(BF16), nt, preferred_element_type=F32)
        inter = b[h] + m_prev
        m_t = jnp.maximum(inter, m_loc[h])
        w_inter = jnp.exp(inter - m_t)
        w_intra = jnp.exp(m_loc[h] - m_t)
        num = w_inter * cq[:, 0:MLSTM_DV] + w_intra * av[h]
        den = w_inter * cq[:, MLSTM_DV:] + w_intra * a_sum[h]
        h_cell = num / jnp.maximum(jnp.abs(den), jnp.exp(-m_t))

        m_new = jnp.maximum(b_last[h] + m_prev, g_max[h])
        c_sc[h] = jnp.exp(b_last[h] + m_prev - m_new) * c_prev + jnp.exp(g_max[h] - m_new) * upd[h]
        m_sc[h:h + 1, :] = m_new

        hn = h_cell * lax.rsqrt(jnp.mean(h_cell * h_cell, axis=1, keepdims=True) + EPS)
        hm_ref[:, hl[h]] = (hn * ng_ref[:, hl[h]] * _sigmoid(mo_ref[:, hl[h]].astype(F32))).astype(BF16)

    @pl.when(step == pl.num_programs(1) - 1)
    def _():
        co_ref[...] = c_sc[...]
        mout_ref[...] = m_sc[...]


def _mlstm(mq, mk, mv, mo, gt, c0, m0, ng, *, chunk):
    b, t, _ = mq.shape
    tok = lambda wd: pl.BlockSpec((None, chunk, wd), lambda i, j: (i, j, 0))
    c_spec = pl.BlockSpec((None, MLSTM_HEADS, MLSTM_EXT, LANES), lambda i, j: (i, 0, 0, 0))
    m_spec = pl.BlockSpec((None, SUBLANES, LANES), lambda i, j: (i, 0, 0))
    t_cols = max(chunk, LANES)
    return pl.pallas_call(
        functools.partial(_mlstm_kernel, chunk=chunk),
        grid=(b, t // chunk),
        in_specs=[tok(MLSTM_W), tok(MLSTM_W), tok(MLSTM_W), tok(MLSTM_W), tok(LANES), c_spec, m_spec,
                  pl.BlockSpec((1, MLSTM_W), lambda i, j: (0, 0))],
        out_specs=[tok(MLSTM_W), c_spec, m_spec],
        out_shape=[jax.ShapeDtypeStruct((b, t, MLSTM_W), BF16),
                   jax.ShapeDtypeStruct((b, MLSTM_HEADS, MLSTM_EXT, LANES), F32),
                   jax.ShapeDtypeStruct((b, SUBLANES, LANES), F32)],
        scratch_shapes=[pltpu.VMEM((MLSTM_HEADS, MLSTM_EXT, LANES), F32), pltpu.VMEM((SUBLANES, LANES), F32),
                        pltpu.VMEM((LANES, t_cols), F32), pltpu.VMEM((LANES, t_cols), F32)],
        compiler_params=_cp(("arbitrary", "arbitrary"), 32),
        name="mlstm",
    )(mq, mk, mv, mo, gt, c0, m0, ng)


def _even_out_kernel(x_ref, fox_ref, hm_ref, w_ref, o_ref, *, fox_transposed):
    dims = (((0,), (0,)), ((), ())) if fox_transposed else (((1,), (0,)), ((), ()))
    y = lax.dot_general(fox_ref[...], w_ref[0:FOX_W, :], dims, preferred_element_type=F32)
    y = y + jnp.dot(hm_ref[...], w_ref[FOX_W:, :], preferred_element_type=F32)
    o_ref[...] = x_ref[...] + y


def _even_out(x, fox, hm, w, *, tm, fox_transposed):
    b, t, _ = x.shape
    fox_spec = (pl.BlockSpec((None, FOX_W, tm), lambda i, j: (i, 0, j)) if fox_transposed
                else pl.BlockSpec((None, tm, FOX_W), lambda i, j: (i, j, 0)))
    return pl.pallas_call(
        functools.partial(_even_out_kernel, fox_transposed=fox_transposed),
        grid=(b, t // tm),
        in_specs=[pl.BlockSpec((None, tm, D_MODEL), lambda i, j: (i, j, 0)), fox_spec,
                  pl.BlockSpec((None, tm, MLSTM_W), lambda i, j: (i, j, 0)),
                  pl.BlockSpec((D_MODEL, D_MODEL), lambda i, j: (0, 0))],
        out_specs=pl.BlockSpec((None, tm, D_MODEL), lambda i, j: (i, j, 0)),
        out_shape=jax.ShapeDtypeStruct((b, t, D_MODEL), F32),
        compiler_params=_cp(("arbitrary", "arbitrary"), 32),
        name="even_out",
    )(x, fox, hm, w)


def _cross_body(h_ref, g_ref, wq_ref, wo_ref, mk_ref, mv_ref):
    x = h_ref[...]
    xn = _rms(x, g_ref[...]).astype(BF16)
    q = jnp.dot(xn, wq_ref[...], preferred_element_type=F32).astype(BF16)
    hls = [slice(MEM_DH * h, MEM_DH * (h + 1)) for h in range(MEM_HEADS)]
    scores = [lax.dot_general(q[:, hl], mk_ref[:, hl], (((1,), (1,)), ((), ())), preferred_element_type=F32) * MEM_SCALE
              for hl in hls]
    outs = []
    for s, hl in zip(scores, hls):
        e = jnp.exp(s - jnp.max(s, axis=1, keepdims=True))
        p = e * (1.0 / jnp.sum(e, axis=1, keepdims=True))
        outs.append(jnp.dot(p.astype(BF16), mv_ref[:, hl], preferred_element_type=F32).astype(BF16))
    o = jnp.concatenate(outs, axis=1)
    return x + jnp.dot(o, wo_ref[...], preferred_element_type=F32)


def _cross_kernel(h_ref, g_ref, wq_ref, wo_ref, mk_ref, mv_ref, o_ref):
    o_ref[...] = _cross_body(h_ref, g_ref, wq_ref, wo_ref, mk_ref, mv_ref)


def _cross_route_kernel(h_ref, g_ref, wq_ref, wo_ref, mk_ref, mv_ref, gr_ref, rw_ref, o_ref, rt_ref, cnt_ref, carry_ref):
    @pl.when((pl.program_id(0) == 0) & (pl.program_id(1) == 0))
    def _():
        carry_ref[...] = jnp.zeros_like(carry_ref)

    out = _cross_body(h_ref, g_ref, wq_ref, wo_ref, mk_ref, mv_ref)
    o_ref[...] = out
    rt_ref[...] = _route_records(_rms(out, gr_ref[...]), rw_ref, carry_ref)
    cnt_ref[...] = carry_ref[...]


def _cross_route(h, g, wq, wo, mem_k, mem_v, g_ffn, rw, *, tm):
    b, t, _ = h.shape
    tok = pl.BlockSpec((None, tm, D_MODEL), lambda i, j: (i, j, 0))
    mem = pl.BlockSpec((None, MEM_TOKENS, MEM_W), lambda i, j: (i, 0, 0))
    fixed = lambda r, c: pl.BlockSpec((r, c), lambda i, j: (0, 0))
    return pl.pallas_call(
        _cross_route_kernel,
        grid=(b, t // tm),
        in_specs=[tok, fixed(1, D_MODEL), fixed(D_MODEL, MEM_W), fixed(MEM_W, D_MODEL), mem, mem,
                  fixed(1, D_MODEL), fixed(D_MODEL, LANES)],
        out_specs=[tok, pl.BlockSpec((None, tm, LANES), lambda i, j: (i, j, 0)), fixed(1, LANES)],
        out_shape=[jax.ShapeDtypeStruct((b, t, D_MODEL), F32), jax.ShapeDtypeStruct((b, t, LANES), F32),
                   jax.ShapeDtypeStruct((1, LANES), F32)],
        scratch_shapes=[pltpu.VMEM((1, LANES), F32)],
        compiler_params=_cp(("arbitrary", "arbitrary"), 32),
        name="cross_attention_router",
    )(h, g, wq, wo, mem_k, mem_v, g_ffn, rw)


def _cross(h, g, wq, wo, mem_k, mem_v, *, tm):
    b, t, _ = h.shape
    tok = pl.BlockSpec((None, tm, D_MODEL), lambda i, j: (i, j, 0))
    mem = pl.BlockSpec((None, MEM_TOKENS, MEM_W), lambda i, j: (i, 0, 0))
    return pl.pallas_call(
        _cross_kernel,
        grid=(b, t // tm),
        in_specs=[tok, pl.BlockSpec((1, D_MODEL), lambda i, j: (0, 0)),
                  pl.BlockSpec((D_MODEL, MEM_W), lambda i, j: (0, 0)),
                  pl.BlockSpec((MEM_W, D_MODEL), lambda i, j: (0, 0)), mem, mem],
        out_specs=tok,
        out_shape=jax.ShapeDtypeStruct((b, t, D_MODEL), F32),
        compiler_params=_cp(("arbitrary", "arbitrary"), 32),
        name="cross_attention",
    )(h, g, wq, wo, mem_k, mem_v)


def _memkv_kernel(mem_ref, g_ref, wk_ref, wv_ref, k_ref, v_ref, kb_ref, vb_ref):
    mn = _rms(mem_ref[...], g_ref[...]).astype(BF16)
    k = jnp.dot(mn, wk_ref[...], preferred_element_type=F32)
    v = jnp.dot(mn, wv_ref[...], preferred_element_type=F32)
    k_ref[...] = k
    v_ref[...] = v
    kb_ref[...] = k.astype(BF16)
    vb_ref[...] = v.astype(BF16)


def _memkv(mem, g, wk, wv, *, tm):
    n = mem.shape[0]
    depth = g.shape[0]
    w_spec = pl.BlockSpec((None, D_MODEL, MEM_W), lambda l, i: (l, 0, 0))
    o_spec = pl.BlockSpec((None, tm, MEM_W), lambda l, i: (l, i, 0))
    return pl.pallas_call(
        _memkv_kernel,
        grid=(depth, n // tm),
        in_specs=[pl.BlockSpec((tm, D_MODEL), lambda l, i: (i, 0)),
                  pl.BlockSpec((None, 1, D_MODEL), lambda l, i: (l, 0, 0)), w_spec, w_spec],
        out_specs=[o_spec] * 4,
        out_shape=[jax.ShapeDtypeStruct((depth, n, MEM_W), dt) for dt in (F32, F32, BF16, BF16)],
        compiler_params=_cp(("arbitrary", "arbitrary"), 32),
        name="memory_kv",
    )(mem, g, wk, wv)


def _silu(x):
    return x * _sigmoid(x)


def _swiglu_chunk(x, wg_ref, wu_ref, wd_ref):
    gate_up = []
    for s in range(FF_TILE // FF_SUB):
        cols = slice(s * FF_SUB, (s + 1) * FF_SUB)
        gate_up.append((jnp.dot(x, wg_ref[:, cols], preferred_element_type=F32),
                        jnp.dot(x, wu_ref[:, cols], preferred_element_type=F32)))
    y = None
    for s, (g, u) in enumerate(gate_up):
        part = jnp.dot((_silu(g) * u).astype(BF16), wd_ref[s * FF_SUB:(s + 1) * FF_SUB, :], preferred_element_type=F32)
        y = part if y is None else y + part
    return y


def _ffn_kernel(h_ref, g_ref, wg_ref, wu_ref, wd_ref, o_ref, xn_sc):
    @pl.when(pl.program_id(1) == 0)
    def _():
        x = h_ref[...]
        xn_sc[...] = _rms(x, g_ref[...]).astype(BF16)
        o_ref[...] = x

    o_ref[...] += _swiglu_chunk(xn_sc[...], wg_ref, wu_ref, wd_ref)


def _ffn(h, g, wg, wu, wd, *, tm):
    n = h.shape[0]
    tok = pl.BlockSpec((tm, D_MODEL), lambda i, j: (i, 0))
    return pl.pallas_call(
        _ffn_kernel,
        grid=(n // tm, D_FF // FF_TILE),
        in_specs=[tok, pl.BlockSpec((1, D_MODEL), lambda i, j: (0, 0)),
                  pl.BlockSpec((D_MODEL, FF_TILE), lambda i, j: (0, j)),
                  pl.BlockSpec((D_MODEL, FF_TILE), lambda i, j: (0, j)),
                  pl.BlockSpec((FF_TILE, D_MODEL), lambda i, j: (j, 0))],
        out_specs=tok,
        out_shape=jax.ShapeDtypeStruct((n, D_MODEL), F32),
        scratch_shapes=[pltpu.VMEM((tm, D_MODEL), BF16)],
        compiler_params=_cp(("arbitrary", "arbitrary"), 48),
        name="dense_swiglu",
    )(h, g, wg, wu, wd)


def _odd_kernel(h_ref, g_ref, win_ref, cw_ref, wout_ref, past_ref, o_ref, st_ref, u_sc):
    j = pl.program_id(1)
    tm = h_ref.shape[0]

    @pl.when(j == 0)
    def _():
        u_sc[0:SUBLANES, :] = past_ref[...]

    @pl.when(j > 0)
    def _():
        u_sc[0:SUBLANES, :] = u_sc[tm:tm + SUBLANES, :]

    x = h_ref[...]
    xn = _rms(x, g_ref[...]).astype(BF16)
    z = jnp.dot(xn, win_ref[...], preferred_element_type=F32)
    gate_b = z[:, 0:D_MODEL]
    u = z[:, D_MODEL:2 * D_MODEL] * z[:, 2 * D_MODEL:]
    u_sc[SUBLANES:, :] = u
    conv = (cw_ref[0:1, :] * u_sc[SUBLANES - 2:SUBLANES - 2 + tm, :]
            + cw_ref[1:2, :] * u_sc[SUBLANES - 1:SUBLANES - 1 + tm, :]
            + cw_ref[2:3, :] * u)
    o_ref[...] = x + jnp.dot((gate_b * conv).astype(BF16), wout_ref[...], preferred_element_type=F32)

    @pl.when(j == pl.num_programs(1) - 1)
    def _():
        st_ref[...] = u_sc[tm:tm + SUBLANES, :]


def _odd(h, g, w_in, cw, w_out, past, *, tm):
    b, t, _ = h.shape
    tok = pl.BlockSpec((None, tm, D_MODEL), lambda i, j: (i, j, 0))
    st = pl.BlockSpec((None, SUBLANES, D_MODEL), lambda i, j: (i, 0, 0))
    return pl.pallas_call(
        _odd_kernel,
        grid=(b, t // tm),
        in_specs=[tok, pl.BlockSpec((1, D_MODEL), lambda i, j: (0, 0)),
                  pl.BlockSpec((D_MODEL, 3 * D_MODEL), lambda i, j: (0, 0)),
                  pl.BlockSpec((SUBLANES, D_MODEL), lambda i, j: (0, 0)),
                  pl.BlockSpec((D_MODEL, D_MODEL), lambda i, j: (0, 0)), st],
        out_specs=[tok, st],
        out_shape=[jax.ShapeDtypeStruct((b, t, D_MODEL), F32), jax.ShapeDtypeStruct((b, SUBLANES, D_MODEL), F32)],
        scratch_shapes=[pltpu.VMEM((tm + SUBLANES, D_MODEL), F32)],
        compiler_params=_cp(("arbitrary", "arbitrary"), 48),
        name="short_conv_mixer",
    )(h, g, w_in, cw, w_out, past)


def _top2(logits, lane_f):
    lg = jnp.where(lane_f < N_EXPERTS, logits, -jnp.inf)
    m1 = jnp.max(lg, axis=1, keepdims=True)
    i1 = jnp.min(jnp.where(lg == m1, lane_f, float(LANES)), axis=1, keepdims=True)
    lg2 = jnp.where(lane_f == i1, -jnp.inf, lg)
    m2 = jnp.max(lg2, axis=1, keepdims=True)
    i2 = jnp.min(jnp.where(lg2 == m2, lane_f, float(LANES)), axis=1, keepdims=True)
    e2 = jnp.exp(m2 - m1)
    inv = 1.0 / (1.0 + e2)
    return i1, i2, inv, e2 * inv


def _route(logits, lane_f):
    i1, i2, g1, g2 = _top2(logits, lane_f)
    return jnp.where(lane_f == i1, g1, 0.0) + jnp.where(lane_f == i2, g2, 0.0)


def _moe_kernel(h_ref, g_ref, rw_ref, wg_ref, wu_ref, wd_ref, fg_ref, o_ref, xn_sc, comb_sc):
    e = pl.program_id(1)
    j = pl.program_id(2)
    tm = h_ref.shape[0]
    lane = lax.broadcasted_iota(jnp.int32, (tm, LANES), 1)

    @pl.when((e == 0) & (j == 0))
    def _():
        x = h_ref[...]
        xn = _rms(x, g_ref[...]).astype(BF16)
        xn_sc[...] = xn
        logits = jnp.dot(xn, rw_ref[...], preferred_element_type=F32)
        comb_sc[...] = _route(logits, lane.astype(F32))
        o_ref[...] = x

    y = _swiglu_chunk(xn_sc[...], wg_ref, wu_ref, wd_ref)
    w_e = jnp.sum(jnp.where(lane == e, comb_sc[...], 0.0), axis=1, keepdims=True)
    o_ref[...] += y * w_e

    @pl.when((e == pl.num_programs(1) - 1) & (j == pl.num_programs(2) - 1))
    def _():
        o_ref[...] = _rms(o_ref[...], fg_ref[...])


def _moe(h, g, rw, wg, wu, wd, fg, *, tm):
    n = h.shape[0]
    tok = pl.BlockSpec((tm, D_MODEL), lambda i, e, j: (i, 0))
    vec = pl.BlockSpec((1, D_MODEL), lambda i, e, j: (0, 0))
    return pl.pallas_call(
        _moe_kernel,
        grid=(n // tm, N_EXPERTS, D_FF // FF_TILE),
        in_specs=[tok, vec, pl.BlockSpec((D_MODEL, LANES), lambda i, e, j: (0, 0)),
                  pl.BlockSpec((None, D_MODEL, FF_TILE), lambda i, e, j: (e, 0, j)),
                  pl.BlockSpec((None, D_MODEL, FF_TILE), lambda i, e, j: (e, 0, j)),
                  pl.BlockSpec((None, FF_TILE, D_MODEL), lambda i, e, j: (e, j, 0)), vec],
        out_specs=tok,
        out_shape=jax.ShapeDtypeStruct((n, D_MODEL), F32),
        scratch_shapes=[pltpu.VMEM((tm, D_MODEL), BF16), pltpu.VMEM((tm, LANES), F32)],
        compiler_params=_cp(("arbitrary", "arbitrary", "arbitrary"), 48),
        name="moe_swiglu",
    )(h, g, rw, wg, wu, wd, fg)


_R_E1, _R_E2, _R_G1, _R_G2, _R_P1, _R_P2 = range(6)


def _route_records(xn, rw_ref, carry_ref):
    tm = xn.shape[0]
    lane = lax.broadcasted_iota(jnp.int32, (tm, LANES), 1)
    lane_f = lane.astype(F32)
    logits = jnp.dot(xn.astype(BF16), rw_ref[...], preferred_element_type=F32)
    i1, i2, g1, g2 = _top2(logits, lane_f)
    sel = jnp.where((lane_f == i1) | (lane_f == i2), 1.0, 0.0)
    incl = _cumsum_rows(sel)
    rank = incl - sel + carry_ref[...]
    p1 = jnp.sum(jnp.where(lane_f == i1, rank, 0.0), axis=1, keepdims=True)
    p2 = jnp.sum(jnp.where(lane_f == i2, rank, 0.0), axis=1, keepdims=True)
    carry_ref[...] = carry_ref[...] + incl[tm - 1:tm, :]
    rec = jnp.zeros((tm, LANES), F32)
    for ln, val in ((_R_E1, i1), (_R_E2, i2), (_R_G1, g1), (_R_G2, g2), (_R_P1, p1), (_R_P2, p2)):
        rec = jnp.where(lane == ln, val, rec)
    return rec


def _row_copy(src, src_row, dst, dst_row, sem):
    return pltpu.make_async_copy(src.at[pl.ds(src_row, 1), :], dst.at[pl.ds(dst_row, 1), :], sem)


def _dispatch_kernel(ends_ref, dest_ref, h_ref, g_ref, xs_ref, zero_sc, x_ref, sem, *, tm_expert):
    tm = h_ref.shape[0]
    x_ref[...] = _rms(h_ref[...], g_ref[...])

    @pl.when(pl.program_id(0) == 0)
    def _():
        zero_sc[...] = jnp.zeros_like(zero_sc)

        def zero_tile(first_row, wanted):
            @pl.when(wanted)
            def _():
                cp = pltpu.make_async_copy(zero_sc, xs_ref.at[pl.ds(pl.multiple_of(first_row, tm_expert), tm_expert), :], sem)
                cp.start()
                cp.wait()

        for e in range(N_EXPERTS):
            begin = ends_ref[e - 1] if e else 0
            zero_tile(ends_ref[e] - tm_expert, ends_ref[e] > begin)
        for k in range(N_EXPERTS):
            first_row = ends_ref[N_EXPERTS - 1] + k * tm_expert
            zero_tile(first_row, first_row < xs_ref.shape[0])

    def start(t, c):
        _row_copy(x_ref, t, xs_ref, dest_ref[0, 2 * t], sem).start()
        _row_copy(x_ref, t, xs_ref, dest_ref[0, 2 * t + 1], sem).start()
        return c

    def wait(t, c):
        _row_copy(x_ref, t, xs_ref, dest_ref[0, 2 * t], sem).wait()
        _row_copy(x_ref, t, xs_ref, dest_ref[0, 2 * t + 1], sem).wait()
        return c

    lax.fori_loop(0, tm, start, 0, unroll=MOE_DMA_UNROLL)
    lax.fori_loop(0, tm, wait, 0, unroll=MOE_DMA_UNROLL)


def _dispatch(ends, dest, h, g, *, rows, tm, tm_expert):
    n = h.shape[0]
    return pl.pallas_call(
        functools.partial(_dispatch_kernel, tm_expert=tm_expert),
        grid_spec=pltpu.PrefetchScalarGridSpec(
            num_scalar_prefetch=1,
            grid=(n // tm,),
            in_specs=[pl.BlockSpec((None, 1, 2 * tm), lambda i, ends: (i, 0, 0), memory_space=pltpu.SMEM),
                      pl.BlockSpec((tm, D_MODEL), lambda i, ends: (i, 0)),
                      pl.BlockSpec((1, D_MODEL), lambda i, ends: (0, 0))],
            out_specs=pl.BlockSpec(memory_space=pl.ANY),
            scratch_shapes=[pltpu.VMEM((tm_expert, D_MODEL), F32), pltpu.VMEM((tm, D_MODEL), F32),
                            pltpu.SemaphoreType.DMA(())]),
        out_shape=jax.ShapeDtypeStruct((rows, D_MODEL), F32),
        compiler_params=_cp(("arbitrary",), 32),
        name="moe_dispatch",
    )(ends, dest.reshape(n // tm, 1, 2 * tm), h, g)


def _expert_kernel(te_ref, nv_ref, x_ref, wg_ref, wu_ref, wd_ref, o_ref, xb_sc):
    n_valid = nv_ref[pl.program_id(0)]

    @pl.when(pl.program_id(1) == 0)
    def _():
        xb_sc[...] = x_ref[...].astype(BF16)
        o_ref[...] = jnp.zeros_like(o_ref)

    @pl.when(n_valid > 0)
    def _():
        o_ref[...] += _swiglu_chunk(xb_sc[...], wg_ref, wu_ref, wd_ref)


def _experts(tile_expert, tile_valid, xs, wg, wu, wd, *, tm):
    rows = xs.shape[0]
    tok = pl.BlockSpec((tm, D_MODEL), lambda i, j, te, nv: (i, 0))
    return pl.pallas_call(
        _expert_kernel,
        grid_spec=pltpu.PrefetchScalarGridSpec(
            num_scalar_prefetch=2,
            grid=(rows // tm, D_FF // FF_TILE),
            in_specs=[tok,
                      pl.BlockSpec((None, D_MODEL, FF_TILE), lambda i, j, te, nv: (te[i], 0, j)),
                      pl.BlockSpec((None, D_MODEL, FF_TILE), lambda i, j, te, nv: (te[i], 0, j)),
                      pl.BlockSpec((None, FF_TILE, D_MODEL), lambda i, j, te, nv: (te[i], j, 0))],
            out_specs=tok,
            scratch_shapes=[pltpu.VMEM((tm, D_MODEL), BF16)]),
        out_shape=jax.ShapeDtypeStruct((rows, D_MODEL), F32),
        compiler_params=_cp(("arbitrary", "arbitrary"), 48),
        name="moe_experts",
    )(tile_expert, tile_valid, xs, wg, wu, wd)


def _combine_kernel(dest_ref, h_ref, rt_ref, fg_ref, ys_ref, o_ref, y1_sc, y2_sc, sem):
    tm = h_ref.shape[0]

    def start(t, c):
        _row_copy(ys_ref, dest_ref[0, 2 * t], y1_sc, t, sem).start()
        _row_copy(ys_ref, dest_ref[0, 2 * t + 1], y2_sc, t, sem).start()
        return c

    def wait(t, c):
        _row_copy(ys_ref, dest_ref[0, 2 * t], y1_sc, t, sem).wait()
        _row_copy(ys_ref, dest_ref[0, 2 * t + 1], y2_sc, t, sem).wait()
        return c

    lax.fori_loop(0, tm, start, 0, unroll=MOE_DMA_UNROLL)
    lax.fori_loop(0, tm, wait, 0, unroll=MOE_DMA_UNROLL)
    rt = rt_ref[...]
    moe = rt[:, _R_G1:_R_G1 + 1] * y1_sc[...] + rt[:, _R_G2:_R_G2 + 1] * y2_sc[...]
    o_ref[...] = _rms(h_ref[...] + moe, fg_ref[...])


def _combine(dest, h, rt, fg, ys, *, tm):
    n = h.shape[0]
    tok = pl.BlockSpec((tm, D_MODEL), lambda i: (i, 0))
    return pl.pallas_call(
        _combine_kernel,
        grid=(n // tm,),
        in_specs=[pl.BlockSpec((None, 1, 2 * tm), lambda i: (i, 0, 0), memory_space=pltpu.SMEM),
                  tok, pl.BlockSpec((tm, LANES), lambda i: (i, 0)), pl.BlockSpec((1, D_MODEL), lambda i: (0, 0)),
                  pl.BlockSpec(memory_space=pl.ANY)],
        out_specs=tok,
        out_shape=jax.ShapeDtypeStruct((n, D_MODEL), F32),
        scratch_shapes=[pltpu.VMEM((tm, D_MODEL), F32), pltpu.VMEM((tm, D_MODEL), F32), pltpu.SemaphoreType.DMA(())],
        compiler_params=_cp(("arbitrary",), 32),
        name="moe_combine",
    )(dest.reshape(n // tm, 1, 2 * tm), h, rt, fg, ys)


def _moe_routed(h, rt, counts, g, wg, wu, wd, fg, *, tm, tm_expert):
    n = h.shape[0]
    counts = counts[0, :N_EXPERTS].astype(jnp.int32)
    group = (counts + tm_expert - 1) // tm_expert * tm_expert
    ends = jnp.cumsum(group)
    starts = ends - group
    e = rt[:, _R_E1:_R_E2 + 1].astype(jnp.int32)
    dest = (starts[e] + rt[:, _R_P1:_R_P2 + 1].astype(jnp.int32)).reshape(2 * n)
    n_tiles = (TOP_K * n) // tm_expert + N_EXPERTS
    tile_start = jnp.arange(n_tiles, dtype=jnp.int32) * tm_expert
    active = tile_start < ends[-1]
    last_active = ends[-1] // tm_expert - 1
    probe = jnp.minimum(tile_start, last_active * tm_expert)
    tile_expert = jnp.sum((probe[:, None] >= ends[None, :]).astype(jnp.int32), axis=1)
    tile_valid = jnp.where(active, jnp.clip(starts[tile_expert] + counts[tile_expert] - tile_start, 0, tm_expert), 0)
    xs = _dispatch(ends.astype(jnp.int32), dest, h, g, rows=n_tiles * tm_expert, tm=tm, tm_expert=tm_expert)
    ys = _experts(tile_expert, tile_valid.astype(jnp.int32), xs, wg, wu, wd, tm=tm_expert)
    return _combine(dest, h, rt, fg, ys, tm=tm)


def _pack_params(p):
    fq, fk, fv, ff, mq, mk, mv, mo, mi, mf = jnp.split(p['even_w_in'][0], list(EVEN_SPLITS), axis=1)
    gate_w = jnp.concatenate([ff, mi, mf, jnp.zeros((D_MODEL, LANES - _G_END), F32)], axis=1)
    gate_b = jnp.concatenate([p['fox_b_f'][0], p['mlstm_b_i'][0], p['mlstm_b_f'][0], jnp.zeros((LANES - _G_END,), F32)])
    row = lambda a: a.reshape(1, -1).astype(F32)
    return dict(
        even_w=jnp.concatenate([fq, fk, fv, mq, mk, mv, mo, gate_w], axis=1).astype(BF16),
        even_b=gate_b.reshape(1, LANES),
        even_w_out=p['even_w_out'][0].astype(BF16),
        mlstm_norm_g=row(p['mlstm_norm_g'][0]),
        norm_mix_g=[row(p['norm_mix_g'][l]) for l in range(2)],
        norm_cross_g=[row(p['norm_cross_g'][l]) for l in range(2)],
        norm_ffn_g=[row(p['norm_ffn_g'][l]) for l in range(2)],
        final_norm_g=row(p['final_norm_g']),
        mem_wq=p['mem_wq'].astype(BF16), mem_wo=p['mem_wo'].astype(BF16),
        ffn_wg=p['ffn_w_gate'][0].astype(BF16), ffn_wu=p['ffn_w_up'][0].astype(BF16), ffn_wd=p['ffn_w_down'][0].astype(BF16),
        odd_w_in=p['odd_w_in'][0].astype(BF16), odd_w_out=p['odd_w_out'][0].astype(BF16),
        conv_w=jnp.concatenate([p['conv_w'][0], jnp.zeros((SUBLANES - CONV_W, D_MODEL), F32)], axis=0),
        router_w=jnp.concatenate([p['router_w'][0], jnp.zeros((D_MODEL, LANES - N_EXPERTS), F32)], axis=1).astype(BF16),
        moe_wg=p['moe_w_gate'][0].astype(BF16), moe_wu=p['moe_w_up'][0].astype(BF16), moe_wd=p['moe_w_down'][0].astype(BF16),
    )


def _trunk(x, mem_k, mem_v, caches, w, *, tm, tm_wide, chunk):
    b, t, _ = x.shape
    n = b * t
    flat = lambda a: a.reshape(n, a.shape[-1])
    per_b = lambda a: a.reshape(b, t, a.shape[-1])

    if caches is None:
        f0 = jnp.zeros((b, 1, LANES), F32)
    else:
        cache_k, cache_v, cache_lf, c0, n0, m0, conv_st = caches
        past_len = cache_k.shape[1]
        lf_pad = jnp.pad(cache_lf.astype(F32), ((0, 0), (0, 0), (0, LANES - FOX_HEADS)))
        ka_cache, f0 = _cache_prep(cache_k.reshape(b, past_len, FOX_W).astype(F32), lf_pad, tm=FOX_TILE)
    qa, ka, fk, fv, vb, mq, mk, mv, mo, gt = _even_in(flat(x), w['norm_mix_g'][0], w['even_w'], w['even_b'], f0,
                                                      tm=tm, tiles_per_batch=t // tm)
    if caches is None:
        fox_t = _fox_attention(per_b(qa), per_b(ka), per_b(vb), q0=0)
        c_ext0 = jnp.zeros((b, MLSTM_HEADS, MLSTM_EXT, LANES), F32)
        m_ext0 = jnp.zeros((b, SUBLANES, LANES), F32)
        past = jnp.zeros((b, SUBLANES, D_MODEL), F32)
    else:
        pad_t = lambda a: jnp.pad(per_b(a), ((0, 0), (0, FOX_TILE - t), (0, 0)))
        k_all = jnp.concatenate([ka_cache, pad_t(ka)], axis=1)
        v_all = jnp.concatenate([cache_v.reshape(b, past_len, FOX_W).astype(F32), pad_t(fv.reshape(n, FOX_W))], axis=1)
        fox_full = _fox_attention(pad_t(qa), k_all, v_all, q0=past_len // FOX_TILE)
        fox_t = jnp.swapaxes(fox_full[:, :, :t], 1, 2)
        c_ext0 = jnp.concatenate([
            jnp.pad(c0.astype(F32), ((0, 0), (0, 0), (0, 0), (0, LANES - MLSTM_DK))),
            jnp.broadcast_to(jnp.pad(n0.astype(F32), ((0, 0), (0, 0), (0, LANES - MLSTM_DK)))[:, :, None, :],
                             (b, MLSTM_HEADS, MLSTM_EXT - MLSTM_DV, LANES))],
            axis=2)
        m_ext0 = jnp.broadcast_to(jnp.pad(m0.astype(F32), ((0, 0), (0, SUBLANES - MLSTM_HEADS)))[:, :, None],
                                  (b, SUBLANES, LANES))
        past = jnp.pad(conv_st.astype(F32), ((0, 0), (SUBLANES - (CONV_W - 1), 0), (0, 0)))
    hm, c_ext, m_ext = _mlstm(per_b(mq), per_b(mk), per_b(mv), per_b(mo), per_b(gt), c_ext0, m_ext0,
                              w['mlstm_norm_g'], chunk=chunk)
    h = _even_out(x, fox_t, hm, w['even_w_out'], tm=tm, fox_transposed=caches is None)
    h = _cross(h, w['norm_cross_g'][0], w['mem_wq'][0], w['mem_wo'][0], mem_k[0], mem_v[0], tm=tm)
    h = _ffn(flat(h), w['norm_ffn_g'][0], w['ffn_wg'], w['ffn_wu'], w['ffn_wd'], tm=tm_wide)

    h, conv_new = _odd(per_b(h), w['norm_mix_g'][1], w['odd_w_in'], w['conv_w'], w['odd_w_out'], past, tm=tm)
    cross_args = (h, w['norm_cross_g'][1], w['mem_wq'][1], w['mem_wo'][1], mem_k[1], mem_v[1])
    experts = (w['moe_wg'], w['moe_wu'], w['moe_wd'], w['final_norm_g'])
    if n * TOP_K >= N_EXPERTS * MOE_EXPERT_TILE:
        h, rt, counts = _cross_route(*cross_args, w['norm_ffn_g'][1], w['router_w'], tm=tm)
        y = _moe_routed(flat(h), flat(rt), counts, w['norm_ffn_g'][1], *experts, tm=MOE_TOKEN_TILE, tm_expert=MOE_EXPERT_TILE)
    else:
        h = _cross(*cross_args, tm=tm)
        y = _moe(flat(h), w['norm_ffn_g'][1], w['router_w'], *experts, tm=tm_wide)

    states = (
        fk.reshape(1, b, t, FOX_HEADS, FOX_DH), fv.reshape(1, b, t, FOX_HEADS, FOX_DH),
        per_b(gt)[None, :, :, _G_FOX:_G_IG],
        c_ext[None, :, :, 0:MLSTM_DV, 0:MLSTM_DK], c_ext[None, :, :, MLSTM_DV, 0:MLSTM_DK], m_ext[None, :, 0:MLSTM_HEADS, 0],
        conv_new[None, :, SUBLANES - (CONV_W - 1):, :],
    )
    return per_b(y), states


def kernel(x_prompt, x_sample, mem_prompt, cache_fox_k, cache_fox_v, cache_fox_logf, state_mlstm_c, state_mlstm_n, state_mlstm_m, state_conv, cache_mem_k, cache_mem_v, norm_mix_g, norm_mem_g, norm_cross_g, norm_ffn_g, final_norm_g, even_w_in, fox_b_f, mlstm_b_i, mlstm_b_f, mlstm_norm_g, even_w_out, odd_w_in, conv_w, odd_w_out, mem_wq, mem_wk, mem_wv, mem_wo, ffn_w_gate, ffn_w_up, ffn_w_down, router_w, moe_w_gate, moe_w_up, moe_w_down):
    w = _pack_params(dict(
        norm_mix_g=norm_mix_g, norm_cross_g=norm_cross_g, norm_ffn_g=norm_ffn_g, final_norm_g=final_norm_g,
        even_w_in=even_w_in, fox_b_f=fox_b_f, mlstm_b_i=mlstm_b_i, mlstm_b_f=mlstm_b_f, mlstm_norm_g=mlstm_norm_g,
        even_w_out=even_w_out, odd_w_in=odd_w_in, conv_w=conv_w, odd_w_out=odd_w_out, mem_wq=mem_wq, mem_wo=mem_wo,
        ffn_w_gate=ffn_w_gate, ffn_w_up=ffn_w_up, ffn_w_down=ffn_w_down, router_w=router_w,
        moe_w_gate=moe_w_gate, moe_w_up=moe_w_up, moe_w_down=moe_w_down))

    bp, tp, _ = x_prompt.shape
    bs, ts, _ = x_sample.shape
    depth = norm_mem_g.shape[0]

    mem_k_p, mem_v_p, mem_kb, mem_vb = _memkv(mem_prompt.reshape(bp * MEM_TOKENS, D_MODEL),
                                              norm_mem_g.reshape(depth, 1, D_MODEL).astype(F32),
                                              mem_wk.astype(BF16), mem_wv.astype(BF16), tm=512)
    per_layer = lambda a, nb: a.reshape(depth, nb, MEM_TOKENS, MEM_W)
    y_prompt, st_p = _trunk(x_prompt, per_layer(mem_kb, bp), per_layer(mem_vb, bp), None, w,
                            tm=512, tm_wide=1024, chunk=MLSTM_CHUNK)

    caches = (cache_fox_k[0], cache_fox_v[0], cache_fox_logf[0], state_mlstm_c[0], state_mlstm_n[0],
              state_mlstm_m[0], state_conv[0])
    y_sample, st_s = _trunk(x_sample, per_layer(cache_mem_k.astype(BF16), bs), per_layer(cache_mem_v.astype(BF16), bs),
                            caches, w, tm=ts, tm_wide=bs * ts, chunk=ts)

    mem_shape = (depth, bp, MEM_TOKENS, MEM_HEADS, MEM_DH)
    return (y_prompt, y_sample) + st_p + (mem_k_p.reshape(mem_shape), mem_v_p.reshape(mem_shape)) + st_s
```

```python
import functools

import jax
import jax.numpy as jnp
from jax import lax
from jax.experimental import pallas as pl
from jax.experimental.pallas import tpu as pltpu

F32 = jnp.float32
BF16 = jnp.bfloat16

D_MODEL = 1024
EPS = 1e-6
FOX_HEADS = 8
FOX_DH = 64
FOX_W = FOX_HEADS * FOX_DH
FOX_SCALE = FOX_DH ** -0.5
MLSTM_HEADS = 4
MLSTM_DV = 128
MLSTM_DK = 64
MLSTM_W = MLSTM_HEADS * MLSTM_DV
MLSTM_QK_W = MLSTM_HEADS * MLSTM_DK
MLSTM_SCALE = MLSTM_DK ** -0.5
MLSTM_CHUNK = 256
MLSTM_EXT = 256
MEM_TOKENS = 256
MEM_HEADS = 4
MEM_DH = 128
MEM_W = MEM_HEADS * MEM_DH
MEM_SCALE = MEM_DH ** -0.5
D_FF = 3584
N_EXPERTS = 8
TOP_K = 2
MOE_EXPERT_TILE = 1024
MOE_TOKEN_TILE = 512
MOE_DMA_UNROLL = 8
CONV_W = 3
EVEN_SIZES = (FOX_W, FOX_W, FOX_W, FOX_HEADS, MLSTM_QK_W, MLSTM_QK_W, MLSTM_W, MLSTM_W, MLSTM_HEADS, MLSTM_HEADS)
EVEN_SPLITS = tuple(sum(EVEN_SIZES[:i + 1]) for i in range(len(EVEN_SIZES) - 1))

LANES = 128
SUBLANES = 8
FOX_AUG_W = FOX_HEADS * LANES
FOX_TILE = 256
FOX_LOOKAHEAD = 3
FOX_PV_DELAY = 2
FF_TILE = 512
FF_SUB = 256
_C_Q, _C_K, _C_V, _C_MQ, _C_MK, _C_MV, _C_MO, _C_G, _C_END = 0, 512, 1024, 1536, 1792, 2048, 2560, 3072, 3200
_G_FOX, _G_IG, _G_LF, _G_END = 0, 8, 12, 16
NEG = -1e30
LOG2E = 1.4426950408889634


def _cp(sem, vmem_mb):
    return pltpu.CompilerParams(dimension_semantics=sem, vmem_limit_bytes=vmem_mb * 1024 * 1024)


def _rms(x, g):
    return x * lax.rsqrt(jnp.mean(x * x, axis=-1, keepdims=True) + EPS) * g


def _log_sigmoid(x):
    return jnp.minimum(x, 0.0) - jnp.log1p(jnp.exp(-jnp.abs(x)))


def _sigmoid(x):
    return 1.0 / (1.0 + jnp.exp(-x))


def _cumsum_rows(x):
    n = x.shape[0]
    row = lax.broadcasted_iota(jnp.int32, x.shape, 0)
    s = 1
    while s < n:
        x = x + jnp.where(row >= s, pltpu.roll(x, s, axis=0), 0.0)
        s *= 2
    return x


def _split3(f):
    hi = f.astype(BF16).astype(F32)
    r = f - hi
    mid = r.astype(BF16).astype(F32)
    return hi, mid, r - mid


def _head_block(src, h, lane):
    p, odd = divmod(h, 2)
    blk = src[:, LANES * p:LANES * (p + 1)]
    return pltpu.roll(blk, FOX_DH, axis=1) if odd else blk


_AUG_Q = (64, 72, 80)
_AUG_K = (88, 96, 104)


def _fox_bias_lanes(cum_f, lane):
    parts = _split3(jnp.where(lane < FOX_HEADS, cum_f * LOG2E, 0.0))
    q_side = sum(pltpu.roll(p, off, axis=1) for p, off in zip(parts, _AUG_Q))
    k_side = sum(pltpu.roll(-p, off, axis=1) for p, off in zip(parts, _AUG_K))
    return q_side, k_side


def _fox_aug(src, bias, h, is_query):
    lane = lax.broadcasted_iota(jnp.int32, (1, LANES), 1)
    blk = _head_block(src, h, lane)
    q_lanes = (lane == _AUG_Q[0] + h) | (lane == _AUG_Q[1] + h) | (lane == _AUG_Q[2] + h)
    k_lanes = (lane == _AUG_K[0] + h) | (lane == _AUG_K[1] + h) | (lane == _AUG_K[2] + h)
    own, ones = (q_lanes, k_lanes) if is_query else (k_lanes, q_lanes)
    aug = jnp.where(own, bias, jnp.where(ones, 1.0, 0.0))
    return jnp.where(lane < FOX_DH, blk, aug).astype(BF16)


def _even_in_kernel(x_ref, g_ref, w_ref, b_ref, f0_ref,
                    qa_ref, ka_ref, fk_ref, fv_ref, vb_ref, mq_ref, mk_ref, mv_ref, mo_ref, gt_ref,
                    carry_ref, *, tiles_per_batch):
    @pl.when(pl.program_id(0) % tiles_per_batch == 0)
    def _():
        carry_ref[...] = f0_ref[...]

    xn = _rms(x_ref[...], g_ref[...]).astype(BF16)
    acc = jnp.dot(xn, w_ref[...], preferred_element_type=F32)
    tm = acc.shape[0]
    lane = lax.broadcasted_iota(jnp.int32, (tm, LANES), 1)

    gates = acc[:, _C_G:_C_END] + b_ref[...]
    ls = _log_sigmoid(gates)
    is_ig = (lane >= _G_IG) & (lane < _G_LF)
    gt_ref[...] = jnp.where(is_ig, gates, jnp.where(lane < _G_END, ls, 0.0))
    cum_f = _cumsum_rows(jnp.where(lane < _G_IG, ls, 0.0)) + carry_ref[...]
    carry_ref[...] = cum_f[tm - 1:tm, :]

    q_all = acc[:, _C_Q:_C_K] * (FOX_SCALE * LOG2E)
    k_all = acc[:, _C_K:_C_V]
    v_all = acc[:, _C_V:_C_MQ]
    q_bias, k_bias = _fox_bias_lanes(cum_f, lane)
    for h in range(FOX_HEADS):
        qa_ref[:, LANES * h:LANES * (h + 1)] = _fox_aug(q_all, q_bias, h, True)
        ka_ref[:, LANES * h:LANES * (h + 1)] = _fox_aug(k_all, k_bias, h, False)
    for h in range(FOX_HEADS):
        fk_ref[:, h, :] = _head_block(k_all, h, lane)[:, 0:FOX_DH]
        fv_ref[:, h, :] = _head_block(v_all, h, lane)[:, 0:FOX_DH]
    vb_ref[...] = v_all.astype(BF16)

    mq_all = acc[:, _C_MQ:_C_MK] * MLSTM_SCALE
    mk_all = acc[:, _C_MK:_C_MV]
    for h in range(MLSTM_HEADS):
        mq_ref[:, LANES * h:LANES * (h + 1)] = jnp.where(lane < MLSTM_DK, _head_block(mq_all, h, lane), 0.0).astype(BF16)
        mk_ref[:, LANES * h:LANES * (h + 1)] = jnp.where(lane < MLSTM_DK, _head_block(mk_all, h, lane), 0.0).astype(BF16)
    mv_ref[...] = acc[:, _C_MV:_C_MO].astype(BF16)
    mo_ref[...] = acc[:, _C_MO:_C_G].astype(BF16)


def _even_in(x, g, w, bias, f0, *, tm, tiles_per_batch):
    n = x.shape[0]
    row = lambda i: (i, 0)
    fixed = lambda i: (0, 0)
    head_major = (FOX_HEADS, FOX_DH)
    widths = (FOX_AUG_W, FOX_AUG_W, head_major, head_major, FOX_W, MLSTM_W, MLSTM_W, MLSTM_W, MLSTM_W, LANES)
    dtypes = (BF16, BF16, F32, F32, BF16, BF16, BF16, BF16, BF16, F32)
    spec = lambda wd: (pl.BlockSpec((tm,) + wd, lambda i: (i, 0, 0)) if isinstance(wd, tuple) else pl.BlockSpec((tm, wd), row))
    shape = lambda wd: (n,) + wd if isinstance(wd, tuple) else (n, wd)
    return pl.pallas_call(
        functools.partial(_even_in_kernel, tiles_per_batch=tiles_per_batch),
        grid=(n // tm,),
        in_specs=[pl.BlockSpec((tm, D_MODEL), row), pl.BlockSpec((1, D_MODEL), fixed),
                  pl.BlockSpec((D_MODEL, _C_END), fixed), pl.BlockSpec((1, LANES), fixed),
                  pl.BlockSpec((None, 1, LANES), lambda i: (i // tiles_per_batch, 0, 0))],
        out_specs=[spec(wd) for wd in widths],
        out_shape=[jax.ShapeDtypeStruct(shape(wd), dt) for wd, dt in zip(widths, dtypes)],
        scratch_shapes=[pltpu.VMEM((1, LANES), F32)],
        compiler_params=_cp(("arbitrary",), 48),
        name="even_in",
    )(x, g, w, bias, f0)


def _cache_prep_kernel(k_ref, lf_ref, ka_ref, fend_ref, carry_ref):
    @pl.when(pl.program_id(1) == 0)
    def _():
        carry_ref[...] = jnp.zeros_like(carry_ref)

    k_all = k_ref[...]
    tm = k_all.shape[0]
    lane = lax.broadcasted_iota(jnp.int32, (tm, LANES), 1)
    cum_f = _cumsum_rows(lf_ref[...]) + carry_ref[...]
    carry_ref[...] = cum_f[tm - 1:tm, :]
    fend_ref[...] = cum_f[tm - 1:tm, :]
    _, k_bias = _fox_bias_lanes(cum_f, lane)
    for h in range(FOX_HEADS):
        ka_ref[:, LANES * h:LANES * (h + 1)] = _fox_aug(k_all, k_bias, h, False)


def _cache_prep(cache_k, cache_lf, *, tm):
    b, p, _ = cache_k.shape
    return pl.pallas_call(
        _cache_prep_kernel,
        grid=(b, p // tm),
        in_specs=[pl.BlockSpec((None, tm, FOX_W), lambda i, j: (i, j, 0)),
                  pl.BlockSpec((None, tm, LANES), lambda i, j: (i, j, 0))],
        out_specs=[pl.BlockSpec((None, tm, FOX_AUG_W), lambda i, j: (i, j, 0)),
                   pl.BlockSpec((None, 1, LANES), lambda i, j: (i, 0, 0))],
        out_shape=[jax.ShapeDtypeStruct((b, p, FOX_AUG_W), BF16), jax.ShapeDtypeStruct((b, 1, LANES), F32)],
        scratch_shapes=[pltpu.VMEM((1, LANES), F32)],
        compiler_params=_cp(("arbitrary", "arbitrary"), 32),
        name="fox_cache_prep",
    )(cache_k, cache_lf)


def _fox_kernel(q_ref, k_ref, v_ref, o_ref, m_sc, l_sc, acc_sc, s_sc, *, q0):
    t = FOX_TILE
    n_full = q0 + pl.program_id(1)
    visible = (lax.broadcasted_iota(jnp.int32, (t, t), 0) <= lax.broadcasted_iota(jnp.int32, (t, t), 1))
    m_sc[...] = jnp.full_like(m_sc, NEG)
    l_sc[...] = jnp.zeros_like(l_sc)
    acc_sc[...] = jnp.zeros_like(acc_sc)

    def scores(j, h):
        hl = slice(LANES * h, LANES * (h + 1))
        return lax.dot_general(k_ref[pl.ds(pl.multiple_of(j * t, t), t), hl], q_ref[:, hl], (((1,), (1,)), ((), ())),
                               preferred_element_type=F32)

    def step(j, diagonal):
        off = pl.multiple_of(j * t, t)

        def weighted_values(h, p, alpha):
            pair, hh = divmod(h, 2)
            v_t = v_ref[pl.ds(off, t), LANES * pair:LANES * (pair + 1)].astype(BF16).T
            v_ext = jnp.concatenate([v_t[FOX_DH * hh:FOX_DH * (hh + 1), :], jnp.ones((2 * SUBLANES, t), BF16)], axis=0)
            pv = jnp.dot(v_ext, p, preferred_element_type=F32)
            rows = slice(FOX_DH * h, FOX_DH * (h + 1))
            acc_sc[rows, :] = alpha * acc_sc[rows, :] + pv[0:FOX_DH, :]
            l_sc[h:h + 1, :] = alpha * l_sc[h:h + 1, :] + pv[FOX_DH:FOX_DH + 1, :]

        pending = {h: s_sc[h] for h in range(FOX_LOOKAHEAD)}
        ready = {}
        for h in range(FOX_HEADS):
            s = pending.pop(h)
            if diagonal:
                s = jnp.where(visible, s, NEG)
            m_old = m_sc[h:h + 1, :]
            m_new = jnp.maximum(m_old, jnp.max(s, axis=0, keepdims=True))
            alpha = jnp.exp2(m_old - m_new)
            p = jnp.exp2(s - m_new)
            m_sc[h:h + 1, :] = m_new
            ready[h] = (p.astype(BF16), alpha)
            ahead = h + FOX_LOOKAHEAD
            if ahead < FOX_HEADS:
                pending[ahead] = scores(j, ahead)
            elif not diagonal:
                s_sc[ahead - FOX_HEADS] = scores(j + 1, ahead - FOX_HEADS)
            if h - FOX_PV_DELAY in ready:
                weighted_values(h - FOX_PV_DELAY, *ready.pop(h - FOX_PV_DELAY))
        for h in sorted(ready):
            weighted_values(h, *ready[h])

    def body(j, c):
        step(j, False)
        return c

    for h in range(FOX_LOOKAHEAD):
        s_sc[h] = scores(0, h)
    lax.fori_loop(0, n_full, body, 0)
    step(n_full, True)
    for h in range(FOX_HEADS):
        rows = slice(FOX_DH * h, FOX_DH * (h + 1))
        o_ref[rows, :] = (acc_sc[rows, :] / l_sc[h:h + 1, :]).astype(BF16)


def _fox_attention(q_aug, k_aug, v, *, q0):
    b, tq_total, _ = q_aug.shape
    t_kv = k_aug.shape[1]
    nq = tq_total // FOX_TILE
    return pl.pallas_call(
        functools.partial(_fox_kernel, q0=q0),
        grid=(b, nq),
        in_specs=[pl.BlockSpec((None, FOX_TILE, FOX_AUG_W), lambda bi, i: (bi, i, 0)),
                  pl.BlockSpec((None, t_kv, FOX_AUG_W), lambda bi, i: (bi, 0, 0)),
                  pl.BlockSpec((None, t_kv, FOX_W), lambda bi, i: (bi, 0, 0))],
        out_specs=pl.BlockSpec((None, FOX_W, FOX_TILE), lambda bi, i: (bi, 0, i)),
        out_shape=jax.ShapeDtypeStruct((b, FOX_W, tq_total), BF16),
        scratch_shapes=[pltpu.VMEM((FOX_HEADS, FOX_TILE), F32), pltpu.VMEM((FOX_HEADS, FOX_TILE), F32),
                        pltpu.VMEM((FOX_W, FOX_TILE), F32), pltpu.VMEM((FOX_LOOKAHEAD, FOX_TILE, FOX_TILE), F32)],
        compiler_params=_cp(("arbitrary", "arbitrary"), 48),
        name="fox_attention",
    )(q_aug, k_aug, v)


def _mlstm_kernel(mq_ref, mk_ref, mv_ref, mo_ref, gt_ref, c0_ref, m0_ref, ng_ref,
                  hm_ref, co_ref, mout_ref, c_sc, m_sc, gt_t, cs_t, *, chunk):
    step = pl.program_id(1)

    @pl.when(step == 0)
    def _():
        c_sc[...] = c0_ref[...]
        m_sc[...] = m0_ref[...]

    ln = chunk
    nt = (((1,), (1,)), ((), ()))
    causal = lax.broadcasted_iota(jnp.int32, (ln, ln), 0) >= lax.broadcasted_iota(jnp.int32, (ln, ln), 1)
    heads = range(MLSTM_HEADS)
    hl = [slice(LANES * h, LANES * (h + 1)) for h in heads]

    g = gt_ref[...]
    cs = _cumsum_rows(g)
    if ln < LANES:
        pad = jnp.zeros((LANES - ln, LANES), F32)
        gt_t[...] = jnp.concatenate([g, pad], axis=0).T
        cs_t[...] = jnp.concatenate([cs, pad], axis=0).T
    else:
        gt_t[...] = g.T
        cs_t[...] = cs.T
    rep = lambda col: jnp.broadcast_to(col, (ln, LANES))
    wide = (lambda r: jnp.concatenate([r] * (ln // LANES), axis=1)) if ln >= LANES else (lambda r: r[:, 0:ln])
    ig = [rep(g[:, _G_IG + h:_G_IG + h + 1]) for h in heads]
    b = [rep(cs[:, _G_LF + h:_G_LF + h + 1]) for h in heads]
    b_last = [x[ln - 1:ln, :] for x in b]

    qk = [lax.dot_general(mq_ref[:, hl[h]], mk_ref[:, hl[h]], nt, preferred_element_type=F32) for h in heads]
    m_loc, a_sum, av, g_max, upd = [], [], [], [], []
    for h in heads:
        d = jnp.where(causal, wide(b[h]) - cs_t[_G_LF + h:_G_LF + h + 1, 0:ln] + gt_t[_G_IG + h:_G_IG + h + 1, 0:ln],
                      -jnp.inf)
        m_loc.append(rep(jnp.max(d, axis=1, keepdims=True)))
        a = qk[h] * jnp.exp(d - wide(m_loc[h]))
        a_sum.append(rep(jnp.sum(a, axis=1, keepdims=True)))
        av.append(jnp.dot(a.astype(BF16), mv_ref[:, hl[h]], preferred_element_type=F32))
    for h in heads:
        g_tok = b_last[h] - b[h] + ig[h]
        g_max.append(jnp.max(g_tok, axis=0, keepdims=True))
        w_loc = jnp.exp(g_tok - g_max[h])
        vw = jnp.concatenate([mv_ref[:, hl[h]].astype(F32) * w_loc, w_loc], axis=1)
        upd.append(lax.dot_general(vw.astype(BF16), mk_ref[:, hl[h]], (((0,), (0,)), ((), ())),
                                   preferred_element_type=F32))

    for h in heads:
        m_prev = m_sc[h:h + 1, :]
        c_prev = c_sc[h]
        cq = lax.dot_general(mq_ref[:, hl[h]], c_prev.astype(BF16), nt, preferred_element_type=F32)
        inter = b[h] + m_prev
        m_t = jnp.maximum(inter, m_loc[h])
        w_inter = jnp.exp(inter - m_t)
        w_intra = jnp.exp(m_loc[h] - m_t)
        num = w_inter * cq[:, 0:MLSTM_DV] + w_intra * av[h]
        den = w_inter * cq[:, MLSTM_DV:] + w_intra * a_sum[h]
        h_cell = num / jnp.maximum(jnp.abs(den), jnp.exp(-m_t))

        m_new = jnp.maximum(b_last[h] + m_prev, g_max[h])
        c_sc[h] = jnp.exp(b_last[h] + m_prev - m_new) * c_prev + jnp.exp(g_max[h] - m_new) * upd[h]
        m_sc[h:h + 1, :] = m_new

        hn = h_cell * lax.rsqrt(jnp.mean(h_cell * h_cell, axis=1, keepdims=True) + EPS)
        hm_ref[:, hl[h]] = (hn * ng_ref[:, hl[h]] * _sigmoid(mo_ref[:, hl[h]].astype(F32))).astype(BF16)

    @pl.when(step == pl.num_programs(1) - 1)
    def _():
        co_ref[...] = c_sc[...]
        mout_ref[...] = m_sc[...]


def _mlstm(mq, mk, mv, mo, gt, c0, m0, ng, *, chunk):
    b, t, _ = mq.shape
    tok = lambda wd: pl.BlockSpec((None, chunk, wd), lambda i, j: (i, j, 0))
    c_spec = pl.BlockSpec((None, MLSTM_HEADS, MLSTM_EXT, LANES), lambda i, j: (i, 0, 0, 0))
    m_spec = pl.BlockSpec((None, SUBLANES, LANES), lambda i, j: (i, 0, 0))
    t_cols = max(chunk, LANES)
    return pl.pallas_call(
        functools.partial(_mlstm_kernel, chunk=chunk),
        grid=(b, t // chunk),
        in_specs=[tok(MLSTM_W), tok(MLSTM_W), tok(MLSTM_W), tok(MLSTM_W), tok(LANES), c_spec, m_spec,
                  pl.BlockSpec((1, MLSTM_W), lambda i, j: (0, 0))],
        out_specs=[tok(MLSTM_W), c_spec, m_spec],
        out_shape=[jax.ShapeDtypeStruct((b, t, MLSTM_W), BF16),
                   jax.ShapeDtypeStruct((b, MLSTM_HEADS, MLSTM_EXT, LANES), F32),
                   jax.ShapeDtypeStruct((b, SUBLANES, LANES), F32)],
        scratch_shapes=[pltpu.VMEM((MLSTM_HEADS, MLSTM_EXT, LANES), F32), pltpu.VMEM((SUBLANES, LANES), F32),
                        pltpu.VMEM((LANES, t_cols), F32), pltpu.VMEM((LANES, t_cols), F32)],
        compiler_params=_cp(("arbitrary", "arbitrary"), 32),
        name="mlstm",
    )(mq, mk, mv, mo, gt, c0, m0, ng)


def _even_out_kernel(x_ref, fox_ref, hm_ref, w_ref, o_ref, *, fox_transposed):
    dims = (((0,), (0,)), ((), ())) if fox_transposed else (((1,), (0,)), ((), ()))
    y = lax.dot_general(fox_ref[...], w_ref[0:FOX_W, :], dims, preferred_element_type=F32)
    y = y + jnp.dot(hm_ref[...], w_ref[FOX_W:, :], preferred_element_type=F32)
    o_ref[...] = x_ref[...] + y


def _even_out(x, fox, hm, w, *, tm, fox_transposed):
    b, t, _ = x.shape
    fox_spec = (pl.BlockSpec((None, FOX_W, tm), lambda i, j: (i, 0, j)) if fox_transposed
                else pl.BlockSpec((None, tm, FOX_W), lambda i, j: (i, j, 0)))
    return pl.pallas_call(
        functools.partial(_even_out_kernel, fox_transposed=fox_transposed),
        grid=(b, t // tm),
        in_specs=[pl.BlockSpec((None, tm, D_MODEL), lambda i, j: (i, j, 0)), fox_spec,
                  pl.BlockSpec((None, tm, MLSTM_W), lambda i, j: (i, j, 0)),
                  pl.BlockSpec((D_MODEL, D_MODEL), lambda i, j: (0, 0))],
        out_specs=pl.BlockSpec((None, tm, D_MODEL), lambda i, j: (i, j, 0)),
        out_shape=jax.ShapeDtypeStruct((b, t, D_MODEL), F32),
        compiler_params=_cp(("arbitrary", "arbitrary"), 48),
        name="even_out",
    )(x, fox, hm, w)


def _cross_body(h_ref, g_ref, wq_ref, wo_ref, mk_ref, mv_ref):
    x = h_ref[...]
    xn = _rms(x, g_ref[...]).astype(BF16)
    q = jnp.dot(xn, wq_ref[...], preferred_element_type=F32).astype(BF16)
    hls = [slice(MEM_DH * h, MEM_DH * (h + 1)) for h in range(MEM_HEADS)]
    scores = [lax.dot_general(q[:, hl], mk_ref[:, hl], (((1,), (1,)), ((), ())), preferred_element_type=F32) * MEM_SCALE
              for hl in hls]
    outs = []
    for s, hl in zip(scores, hls):
        e = jnp.exp(s - jnp.max(s, axis=1, keepdims=True))
        p = e * (1.0 / jnp.sum(e, axis=1, keepdims=True))
        outs.append(jnp.dot(p.astype(BF16), mv_ref[:, hl], preferred_element_type=F32).astype(BF16))
    o = jnp.concatenate(outs, axis=1)
    return x + jnp.dot(o, wo_ref[...], preferred_element_type=F32)


def _cross_kernel(h_ref, g_ref, wq_ref, wo_ref, mk_ref, mv_ref, o_ref):
    o_ref[...] = _cross_body(h_ref, g_ref, wq_ref, wo_ref, mk_ref, mv_ref)


def _cross_route_kernel(h_ref, g_ref, wq_ref, wo_ref, mk_ref, mv_ref, gr_ref, rw_ref, o_ref, rt_ref, cnt_ref, carry_ref):
    @pl.when((pl.program_id(0) == 0) & (pl.program_id(1) == 0))
    def _():
        carry_ref[...] = jnp.zeros_like(carry_ref)

    out = _cross_body(h_ref, g_ref, wq_ref, wo_ref, mk_ref, mv_ref)
    o_ref[...] = out
    rt_ref[...] = _route_records(_rms(out, gr_ref[...]), rw_ref, carry_ref)
    cnt_ref[...] = carry_ref[...]


def _cross_route(h, g, wq, wo, mem_k, mem_v, g_ffn, rw, *, tm):
    b, t, _ = h.shape
    tok = pl.BlockSpec((None, tm, D_MODEL), lambda i, j: (i, j, 0))
    mem = pl.BlockSpec((None, MEM_TOKENS, MEM_W), lambda i, j: (i, 0, 0))
    fixed = lambda r, c: pl.BlockSpec((r, c), lambda i, j: (0, 0))
    return pl.pallas_call(
        _cross_route_kernel,
        grid=(b, t // tm),
        in_specs=[tok, fixed(1, D_MODEL), fixed(D_MODEL, MEM_W), fixed(MEM_W, D_MODEL), mem, mem,
                  fixed(1, D_MODEL), fixed(D_MODEL, LANES)],
        out_specs=[tok, pl.BlockSpec((None, tm, LANES), lambda i, j: (i, j, 0)), fixed(1, LANES)],
        out_shape=[jax.ShapeDtypeStruct((b, t, D_MODEL), F32), jax.ShapeDtypeStruct((b, t, LANES), F32),
                   jax.ShapeDtypeStruct((1, LANES), F32)],
        scratch_shapes=[pltpu.VMEM((1, LANES), F32)],
        compiler_params=_cp(("arbitrary", "arbitrary"), 48),
        name="cross_attention_router",
    )(h, g, wq, wo, mem_k, mem_v, g_ffn, rw)


def _cross(h, g, wq, wo, mem_k, mem_v, *, tm):
    b, t, _ = h.shape
    tok = pl.BlockSpec((None, tm, D_MODEL), lambda i, j: (i, j, 0))
    mem = pl.BlockSpec((None, MEM_TOKENS, MEM_W), lambda i, j: (i, 0, 0))
    return pl.pallas_call(
        _cross_kernel,
        grid=(b, t // tm),
        in_specs=[tok, pl.BlockSpec((1, D_MODEL), lambda i, j: (0, 0)),
                  pl.BlockSpec((D_MODEL, MEM_W), lambda i, j: (0, 0)),
                  pl.BlockSpec((MEM_W, D_MODEL), lambda i, j: (0, 0)), mem, mem],
        out_specs=tok,
        out_shape=jax.ShapeDtypeStruct((b, t, D_MODEL), F32),
        compiler_params=_cp(("arbitrary", "arbitrary"), 48),
        name="cross_attention",
    )(h, g, wq, wo, mem_k, mem_v)


def _memkv_kernel(mem_ref, g_ref, wk_ref, wv_ref, k_ref, v_ref, kb_ref, vb_ref):
    mn = _rms(mem_ref[...], g_ref[...]).astype(BF16)
    k = jnp.dot(mn, wk_ref[...], preferred_element_type=F32)
    v = jnp.dot(mn, wv_ref[...], preferred_element_type=F32)
    k_ref[...] = k
    v_ref[...] = v
    kb_ref[...] = k.astype(BF16)
    vb_ref[...] = v.astype(BF16)


def _memkv(mem, g, wk, wv, *, tm):
    n = mem.shape[0]
    depth = g.shape[0]
    w_spec = pl.BlockSpec((None, D_MODEL, MEM_W), lambda l, i: (l, 0, 0))
    o_spec = pl.BlockSpec((None, tm, MEM_W), lambda l, i: (l, i, 0))
    return pl.pallas_call(
        _memkv_kernel,
        grid=(depth, n // tm),
        in_specs=[pl.BlockSpec((tm, D_MODEL), lambda l, i: (i, 0)),
                  pl.BlockSpec((None, 1, D_MODEL), lambda l, i: (l, 0, 0)), w_spec, w_spec],
        out_specs=[o_spec] * 4,
        out_shape=[jax.ShapeDtypeStruct((depth, n, MEM_W), dt) for dt in (F32, F32, BF16, BF16)],
        compiler_params=_cp(("arbitrary", "arbitrary"), 32),
        name="memory_kv",
    )(mem, g, wk, wv)


def _silu(x):
    return x * _sigmoid(x)


def _swiglu_chunk(x, wg_ref, wu_ref, wd_ref):
    gate_up = []
    for s in range(FF_TILE // FF_SUB):
        cols = slice(s * FF_SUB, (s + 1) * FF_SUB)
        gate_up.append((jnp.dot(x, wg_ref[:, cols], preferred_element_type=F32),
                        jnp.dot(x, wu_ref[:, cols], preferred_element_type=F32)))
    y = None
    for s, (g, u) in enumerate(gate_up):
        part = jnp.dot((_silu(g) * u).astype(BF16), wd_ref[s * FF_SUB:(s + 1) * FF_SUB, :], preferred_element_type=F32)
        y = part if y is None else y + part
    return y


def _ffn_kernel(h_ref, g_ref, wg_ref, wu_ref, wd_ref, o_ref, xn_sc):
    @pl.when(pl.program_id(1) == 0)
    def _():
        x = h_ref[...]
        xn_sc[...] = _rms(x, g_ref[...]).astype(BF16)
        o_ref[...] = x

    o_ref[...] += _swiglu_chunk(xn_sc[...], wg_ref, wu_ref, wd_ref)


def _ffn(h, g, wg, wu, wd, *, tm):
    n = h.shape[0]
    tok = pl.BlockSpec((tm, D_MODEL), lambda i, j: (i, 0))
    return pl.pallas_call(
        _ffn_kernel,
        grid=(n // tm, D_FF // FF_TILE),
        in_specs=[tok, pl.BlockSpec((1, D_MODEL), lambda i, j: (0, 0)),
                  pl.BlockSpec((D_MODEL, FF_TILE), lambda i, j: (0, j)),
                  pl.BlockSpec((D_MODEL, FF_TILE), lambda i, j: (0, j)),
                  pl.BlockSpec((FF_TILE, D_MODEL), lambda i, j: (j, 0))],
        out_specs=tok,
        out_shape=jax.ShapeDtypeStruct((n, D_MODEL), F32),
        scratch_shapes=[pltpu.VMEM((tm, D_MODEL), BF16)],
        compiler_params=_cp(("arbitrary", "arbitrary"), 48),
        name="dense_swiglu",
    )(h, g, wg, wu, wd)


def _odd_kernel(h_ref, g_ref, win_ref, cw_ref, wout_ref, past_ref, o_ref, st_ref, u_sc):
    j = pl.program_id(1)
    tm = h_ref.shape[0]

    @pl.when(j == 0)
    def _():
        u_sc[0:SUBLANES, :] = past_ref[...]

    @pl.when(j > 0)
    def _():
        u_sc[0:SUBLANES, :] = u_sc[tm:tm + SUBLANES, :]

    x = h_ref[...]
    xn = _rms(x, g_ref[...]).astype(BF16)
    z = jnp.dot(xn, win_ref[...], preferred_element_type=F32)
    gate_b = z[:, 0:D_MODEL]
    u = z[:, D_MODEL:2 * D_MODEL] * z[:, 2 * D_MODEL:]
    u_sc[SUBLANES:, :] = u
    conv = (cw_ref[0:1, :] * u_sc[SUBLANES - 2:SUBLANES - 2 + tm, :]
            + cw_ref[1:2, :] * u_sc[SUBLANES - 1:SUBLANES - 1 + tm, :]
            + cw_ref[2:3, :] * u)
    o_ref[...] = x + jnp.dot((gate_b * conv).astype(BF16), wout_ref[...], preferred_element_type=F32)

    @pl.when(j == pl.num_programs(1) - 1)
    def _():
        st_ref[...] = u_sc[tm:tm + SUBLANES, :]


def _odd(h, g, w_in, cw, w_out, past, *, tm):
    b, t, _ = h.shape
    tok = pl.BlockSpec((None, tm, D_MODEL), lambda i, j: (i, j, 0))
    st = pl.BlockSpec((None, SUBLANES, D_MODEL), lambda i, j: (i, 0, 0))
    return pl.pallas_call(
        _odd_kernel,
        grid=(b, t // tm),
        in_specs=[tok, pl.BlockSpec((1, D_MODEL), lambda i, j: (0, 0)),
                  pl.BlockSpec((D_MODEL, 3 * D_MODEL), lambda i, j: (0, 0)),
                  pl.BlockSpec((SUBLANES, D_MODEL), lambda i, j: (0, 0)),
                  pl.BlockSpec((D_MODEL, D_MODEL), lambda i, j: (0, 0)), st],
        out_specs=[tok, st],
        out_shape=[jax.ShapeDtypeStruct((b, t, D_MODEL), F32), jax.ShapeDtypeStruct((b, SUBLANES, D_MODEL), F32)],
        scratch_shapes=[pltpu.VMEM((tm + SUBLANES, D_MODEL), F32)],
        compiler_params=_cp(("arbitrary", "arbitrary"), 48),
        name="short_conv_mixer",
    )(h, g, w_in, cw, w_out, past)


def _top2(logits, lane_f):
    lg = jnp.where(lane_f < N_EXPERTS, logits, -jnp.inf)
    m1 = jnp.max(lg, axis=1, keepdims=True)
    i1 = jnp.min(jnp.where(lg == m1, lane_f, float(LANES)), axis=1, keepdims=True)
    lg2 = jnp.where(lane_f == i1, -jnp.inf, lg)
    m2 = jnp.max(lg2, axis=1, keepdims=True)
    i2 = jnp.min(jnp.where(lg2 == m2, lane_f, float(LANES)), axis=1, keepdims=True)
    e2 = jnp.exp(m2 - m1)
    inv = 1.0 / (1.0 + e2)
    return i1, i2, inv, e2 * inv


def _route(logits, lane_f):
    i1, i2, g1, g2 = _top2(logits, lane_f)
    return jnp.where(lane_f == i1, g1, 0.0) + jnp.where(lane_f == i2, g2, 0.0)


def _moe_kernel(h_ref, g_ref, rw_ref, wg_ref, wu_ref, wd_ref, fg_ref, o_ref, xn_sc, comb_sc):
    e = pl.program_id(1)
    j = pl.program_id(2)
    tm = h_ref.shape[0]
    lane = lax.broadcasted_iota(jnp.int32, (tm, LANES), 1)

    @pl.when((e == 0) & (j == 0))
    def _():
        x = h_ref[...]
        xn = _rms(x, g_ref[...]).astype(BF16)
        xn_sc[...] = xn
        logits = jnp.dot(xn, rw_ref[...], preferred_element_type=F32)
        comb_sc[...] = _route(logits, lane.astype(F32))
        o_ref[...] = x

    y = _swiglu_chunk(xn_sc[...], wg_ref, wu_ref, wd_ref)
    w_e = jnp.sum(jnp.where(lane == e, comb_sc[...], 0.0), axis=1, keepdims=True)
    o_ref[...] += y * w_e

    @pl.when((e == pl.num_programs(1) - 1) & (j == pl.num_programs(2) - 1))
    def _():
        o_ref[...] = _rms(o_ref[...], fg_ref[...])


def _moe(h, g, rw, wg, wu, wd, fg, *, tm):
    n = h.shape[0]
    tok = pl.BlockSpec((tm, D_MODEL), lambda i, e, j: (i, 0))
    vec = pl.BlockSpec((1, D_MODEL), lambda i, e, j: (0, 0))
    return pl.pallas_call(
        _moe_kernel,
        grid=(n // tm, N_EXPERTS, D_FF // FF_TILE),
        in_specs=[tok, vec, pl.BlockSpec((D_MODEL, LANES), lambda i, e, j: (0, 0)),
                  pl.BlockSpec((None, D_MODEL, FF_TILE), lambda i, e, j: (e, 0, j)),
                  pl.BlockSpec((None, D_MODEL, FF_TILE), lambda i, e, j: (e, 0, j)),
                  pl.BlockSpec((None, FF_TILE, D_MODEL), lambda i, e, j: (e, j, 0)), vec],
        out_specs=tok,
        out_shape=jax.ShapeDtypeStruct((n, D_MODEL), F32),
        scratch_shapes=[pltpu.VMEM((tm, D_MODEL), BF16), pltpu.VMEM((tm, LANES), F32)],
        compiler_params=_cp(("arbitrary", "arbitrary", "arbitrary"), 48),
        name="moe_swiglu",
    )(h, g, rw, wg, wu, wd, fg)


_R_E1, _R_E2, _R_G1, _R_G2, _R_P1, _R_P2 = range(6)


def _route_records(xn, rw_ref, carry_ref):
    tm = xn.shape[0]
    lane = lax.broadcasted_iota(jnp.int32, (tm, LANES), 1)
    lane_f = lane.astype(F32)
    logits = jnp.dot(xn.astype(BF16), rw_ref[...], preferred_element_type=F32)
    i1, i2, g1, g2 = _top2(logits, lane_f)
    sel = jnp.where((lane_f == i1) | (lane_f == i2), 1.0, 0.0)
    incl = _cumsum_rows(sel)
    rank = incl - sel + carry_ref[...]
    p1 = jnp.sum(jnp.where(lane_f == i1, rank, 0.0), axis=1, keepdims=True)
    p2 = jnp.sum(jnp.where(lane_f == i2, rank, 0.0), axis=1, keepdims=True)
    carry_ref[...] = carry_ref[...] + incl[tm - 1:tm, :]
    rec = jnp.zeros((tm, LANES), F32)
    for ln, val in ((_R_E1, i1), (_R_E2, i2), (_R_G1, g1), (_R_G2, g2), (_R_P1, p1), (_R_P2, p2)):
        rec = jnp.where(lane == ln, val, rec)
    return rec


def _row_copy(src, src_row, dst, dst_row, sem):
    return pltpu.make_async_copy(src.at[pl.ds(src_row, 1), :], dst.at[pl.ds(dst_row, 1), :], sem)


def _dispatch_kernel(ends_ref, dest_ref, h_ref, g_ref, xs_ref, zero_sc, x_ref, sem, *, tm_expert):
    tm = h_ref.shape[0]
    x_ref[...] = _rms(h_ref[...], g_ref[...])

    @pl.when(pl.program_id(0) == 0)
    def _():
        zero_sc[...] = jnp.zeros_like(zero_sc)

        def zero_tile(first_row, wanted):
            @pl.when(wanted)
            def _():
                cp = pltpu.make_async_copy(zero_sc, xs_ref.at[pl.ds(pl.multiple_of(first_row, tm_expert), tm_expert), :], sem)
                cp.start()
                cp.wait()

        for e in range(N_EXPERTS):
            begin = ends_ref[e - 1] if e else 0
            zero_tile(ends_ref[e] - tm_expert, ends_ref[e] > begin)
        for k in range(N_EXPERTS):
            first_row = ends_ref[N_EXPERTS - 1] + k * tm_expert
            zero_tile(first_row, first_row < xs_ref.shape[0])

    def start(t, c):
        _row_copy(x_ref, t, xs_ref, dest_ref[0, 2 * t], sem).start()
        _row_copy(x_ref, t, xs_ref, dest_ref[0, 2 * t + 1], sem).start()
        return c

    def wait(t, c):
        _row_copy(x_ref, t, xs_ref, dest_ref[0, 2 * t], sem).wait()
        _row_copy(x_ref, t, xs_ref, dest_ref[0, 2 * t + 1], sem).wait()
        return c

    lax.fori_loop(0, tm, start, 0, unroll=MOE_DMA_UNROLL)
    lax.fori_loop(0, tm, wait, 0, unroll=MOE_DMA_UNROLL)


def _dispatch(ends, dest, h, g, *, rows, tm, tm_expert):
    n = h.shape[0]
    return pl.pallas_call(
        functools.partial(_dispatch_kernel, tm_expert=tm_expert),
        grid_spec=pltpu.PrefetchScalarGridSpec(
            num_scalar_prefetch=1,
            grid=(n // tm,),
            in_specs=[pl.BlockSpec((None, 1, 2 * tm), lambda i, ends: (i, 0, 0), memory_space=pltpu.SMEM),
                      pl.BlockSpec((tm, D_MODEL), lambda i, ends: (i, 0)),
                      pl.BlockSpec((1, D_MODEL), lambda i, ends: (0, 0))],
            out_specs=pl.BlockSpec(memory_space=pl.ANY),
            scratch_shapes=[pltpu.VMEM((tm_expert, D_MODEL), F32), pltpu.VMEM((tm, D_MODEL), F32),
                            pltpu.SemaphoreType.DMA(())]),
        out_shape=jax.ShapeDtypeStruct((rows, D_MODEL), F32),
        compiler_params=_cp(("arbitrary",), 32),
        name="moe_dispatch",
    )(ends, dest.reshape(n // tm, 1, 2 * tm), h, g)


def _expert_kernel(te_ref, nv_ref, x_ref, wg_ref, wu_ref, wd_ref, o_ref, xb_sc):
    n_valid = nv_ref[pl.program_id(0)]

    @pl.when(pl.program_id(1) == 0)
    def _():
        xb_sc[...] = x_ref[...].astype(BF16)
        o_ref[...] = jnp.zeros_like(o_ref)

    @pl.when(n_valid > 0)
    def _():
        o_ref[...] += _swiglu_chunk(xb_sc[...], wg_ref, wu_ref, wd_ref)


def _experts(tile_expert, tile_valid, xs, wg, wu, wd, *, tm):
    rows = xs.shape[0]
    tok = pl.BlockSpec((tm, D_MODEL), lambda i, j, te, nv: (i, 0))
    return pl.pallas_call(
        _expert_kernel,
        grid_spec=pltpu.PrefetchScalarGridSpec(
            num_scalar_prefetch=2,
            grid=(rows // tm, D_FF // FF_TILE),
            in_specs=[tok,
                      pl.BlockSpec((None, D_MODEL, FF_TILE), lambda i, j, te, nv: (te[i], 0, j)),
                      pl.BlockSpec((None, D_MODEL, FF_TILE), lambda i, j, te, nv: (te[i], 0, j)),
                      pl.BlockSpec((None, FF_TILE, D_MODEL), lambda i, j, te, nv: (te[i], j, 0))],
            out_specs=tok,
            scratch_shapes=[pltpu.VMEM((tm, D_MODEL), BF16)]),
        out_shape=jax.ShapeDtypeStruct((rows, D_MODEL), F32),
        compiler_params=_cp(("arbitrary", "arbitrary"), 48),
        name="moe_experts",
    )(tile_expert, tile_valid, xs, wg, wu, wd)


def _combine_kernel(dest_ref, h_ref, rt_ref, fg_ref, ys_ref, o_ref, y1_sc, y2_sc, sem):
    tm = h_ref.shape[0]

    def start(t, c):
        _row_copy(ys_ref, dest_ref[0, 2 * t], y1_sc, t, sem).start()
        _row_copy(ys_ref, dest_ref[0, 2 * t + 1], y2_sc, t, sem).start()
        return c

    def wait(t, c):
        _row_copy(ys_ref, dest_ref[0, 2 * t], y1_sc, t, sem).wait()
        _row_copy(ys_ref, dest_ref[0, 2 * t + 1], y2_sc, t, sem).wait()
        return c

    lax.fori_loop(0, tm, start, 0, unroll=MOE_DMA_UNROLL)
    lax.fori_loop(0, tm, wait, 0, unroll=MOE_DMA_UNROLL)
    rt = rt_ref[...]
    moe = rt[:, _R_G1:_R_G1 + 1] * y1_sc[...] + rt[:, _R_G2:_R_G2 + 1] * y2_sc[...]
    o_ref[...] = _rms(h_ref[...] + moe, fg_ref[...])


def _combine(dest, h, rt, fg, ys, *, tm):
    n = h.shape[0]
    tok = pl.BlockSpec((tm, D_MODEL), lambda i: (i, 0))
    return pl.pallas_call(
        _combine_kernel,
        grid=(n // tm,),
        in_specs=[pl.BlockSpec((None, 1, 2 * tm), lambda i: (i, 0, 0), memory_space=pltpu.SMEM),
                  tok, pl.BlockSpec((tm, LANES), lambda i: (i, 0)), pl.BlockSpec((1, D_MODEL), lambda i: (0, 0)),
                  pl.BlockSpec(memory_space=pl.ANY)],
        out_specs=tok,
        out_shape=jax.ShapeDtypeStruct((n, D_MODEL), F32),
        scratch_shapes=[pltpu.VMEM((tm, D_MODEL), F32), pltpu.VMEM((tm, D_MODEL), F32), pltpu.SemaphoreType.DMA(())],
        compiler_params=_cp(("arbitrary",), 32),
        name="moe_combine",
    )(dest.reshape(n // tm, 1, 2 * tm), h, rt, fg, ys)


def _moe_routed(h, rt, counts, g, wg, wu, wd, fg, *, tm, tm_expert):
    n = h.shape[0]
    counts = counts[0, :N_EXPERTS].astype(jnp.int32)
    group = (counts + tm_expert - 1) // tm_expert * tm_expert
    ends = jnp.cumsum(group)
    starts = ends - group
    e = rt[:, _R_E1:_R_E2 + 1].astype(jnp.int32)
    dest = (starts[e] + rt[:, _R_P1:_R_P2 + 1].astype(jnp.int32)).reshape(2 * n)
    n_tiles = (TOP_K * n) // tm_expert + N_EXPERTS
    tile_start = jnp.arange(n_tiles, dtype=jnp.int32) * tm_expert
    active = tile_start < ends[-1]
    last_active = ends[-1] // tm_expert - 1
    probe = jnp.minimum(tile_start, last_active * tm_expert)
    tile_expert = jnp.sum((probe[:, None] >= ends[None, :]).astype(jnp.int32), axis=1)
    tile_valid = jnp.where(active, jnp.clip(starts[tile_expert] + counts[tile_expert] - tile_start, 0, tm_expert), 0)
    xs = _dispatch(ends.astype(jnp.int32), dest, h, g, rows=n_tiles * tm_expert, tm=tm, tm_expert=tm_expert)
    ys = _experts(tile_expert, tile_valid.astype(jnp.int32), xs, wg, wu, wd, tm=tm_expert)
    return _combine(dest, h, rt, fg, ys, tm=tm)


def _pack_params(p):
    fq, fk, fv, ff, mq, mk, mv, mo, mi, mf = jnp.split(p['even_w_in'][0], list(EVEN_SPLITS), axis=1)
    gate_w = jnp.concatenate([ff, mi, mf, jnp.zeros((D_MODEL, LANES - _G_END), F32)], axis=1)
    gate_b = jnp.concatenate([p['fox_b_f'][0], p['mlstm_b_i'][0], p['mlstm_b_f'][0], jnp.zeros((LANES - _G_END,), F32)])
    row = lambda a: a.reshape(1, -1).astype(F32)
    return dict(
        even_w=jnp.concatenate([fq, fk, fv, mq, mk, mv, mo, gate_w], axis=1).astype(BF16),
        even_b=gate_b.reshape(1, LANES),
        even_w_out=p['even_w_out'][0].astype(BF16),
        mlstm_norm_g=row(p['mlstm_norm_g'][0]),
        norm_mix_g=[row(p['norm_mix_g'][l]) for l in range(2)],
        norm_cross_g=[row(p['norm_cross_g'][l]) for l in range(2)],
        norm_ffn_g=[row(p['norm_ffn_g'][l]) for l in range(2)],
        final_norm_g=row(p['final_norm_g']),
        mem_wq=p['mem_wq'].astype(BF16), mem_wo=p['mem_wo'].astype(BF16),
        ffn_wg=p['ffn_w_gate'][0].astype(BF16), ffn_wu=p['ffn_w_up'][0].astype(BF16), ffn_wd=p['ffn_w_down'][0].astype(BF16),
        odd_w_in=p['odd_w_in'][0].astype(BF16), odd_w_out=p['odd_w_out'][0].astype(BF16),
        conv_w=jnp.concatenate([p['conv_w'][0], jnp.zeros((SUBLANES - CONV_W, D_MODEL), F32)], axis=0),
        router_w=jnp.concatenate([p['router_w'][0], jnp.zeros((D_MODEL, LANES - N_EXPERTS), F32)], axis=1).astype(BF16),
        moe_wg=p['moe_w_gate'][0].astype(BF16), moe_wu=p['moe_w_up'][0].astype(BF16), moe_wd=p['moe_w_down'][0].astype(BF16),
    )


def _trunk(x, mem_k, mem_v, caches, w, *, tm, tm_light, tm_wide, chunk):
    b, t, _ = x.shape
    n = b * t
    flat = lambda a: a.reshape(n, a.shape[-1])
    per_b = lambda a: a.reshape(b, t, a.shape[-1])

    if caches is None:
        f0 = jnp.zeros((b, 1, LANES), F32)
    else:
        cache_k, cache_v, cache_lf, c0, n0, m0, conv_st = caches
        past_len = cache_k.shape[1]
        lf_pad = jnp.pad(cache_lf.astype(F32), ((0, 0), (0, 0), (0, LANES - FOX_HEADS)))
        ka_cache, f0 = _cache_prep(cache_k.reshape(b, past_len, FOX_W).astype(F32), lf_pad, tm=FOX_TILE)
    qa, ka, fk, fv, vb, mq, mk, mv, mo, gt = _even_in(flat(x), w['norm_mix_g'][0], w['even_w'], w['even_b'], f0,
                                                      tm=tm, tiles_per_batch=t // tm)
    if caches is None:
        fox_t = _fox_attention(per_b(qa), per_b(ka), per_b(vb), q0=0)
        c_ext0 = jnp.zeros((b, MLSTM_HEADS, MLSTM_EXT, LANES), F32)
        m_ext0 = jnp.zeros((b, SUBLANES, LANES), F32)
        past = jnp.zeros((b, SUBLANES, D_MODEL), F32)
    else:
        pad_t = lambda a: jnp.pad(per_b(a), ((0, 0), (0, FOX_TILE - t), (0, 0)))
        k_all = jnp.concatenate([ka_cache, pad_t(ka)], axis=1)
        v_all = jnp.concatenate([cache_v.reshape(b, past_len, FOX_W).astype(F32), pad_t(fv.reshape(n, FOX_W))], axis=1)
        fox_full = _fox_attention(pad_t(qa), k_all, v_all, q0=past_len // FOX_TILE)
        fox_t = jnp.swapaxes(fox_full[:, :, :t], 1, 2)
        c_ext0 = jnp.concatenate([
            jnp.pad(c0.astype(F32), ((0, 0), (0, 0), (0, 0), (0, LANES - MLSTM_DK))),
            jnp.broadcast_to(jnp.pad(n0.astype(F32), ((0, 0), (0, 0), (0, LANES - MLSTM_DK)))[:, :, None, :],
                             (b, MLSTM_HEADS, MLSTM_EXT - MLSTM_DV, LANES))],
            axis=2)
        m_ext0 = jnp.broadcast_to(jnp.pad(m0.astype(F32), ((0, 0), (0, SUBLANES - MLSTM_HEADS)))[:, :, None],
                                  (b, SUBLANES, LANES))
        past = jnp.pad(conv_st.astype(F32), ((0, 0), (SUBLANES - (CONV_W - 1), 0), (0, 0)))
    hm, c_ext, m_ext = _mlstm(per_b(mq), per_b(mk), per_b(mv), per_b(mo), per_b(gt), c_ext0, m_ext0,
                              w['mlstm_norm_g'], chunk=chunk)
    h = _even_out(x, fox_t, hm, w['even_w_out'], tm=tm_light, fox_transposed=caches is None)
    h = _cross(h, w['norm_cross_g'][0], w['mem_wq'][0], w['mem_wo'][0], mem_k[0], mem_v[0], tm=tm_light)
    h = _ffn(flat(h), w['norm_ffn_g'][0], w['ffn_wg'], w['ffn_wu'], w['ffn_wd'], tm=tm_wide)

    h, conv_new = _odd(per_b(h), w['norm_mix_g'][1], w['odd_w_in'], w['conv_w'], w['odd_w_out'], past, tm=tm)
    cross_args = (h, w['norm_cross_g'][1], w['mem_wq'][1], w['mem_wo'][1], mem_k[1], mem_v[1])
    experts = (w['moe_wg'], w['moe_wu'], w['moe_wd'], w['final_norm_g'])
    if n * TOP_K >= N_EXPERTS * MOE_EXPERT_TILE:
        h, rt, counts = _cross_route(*cross_args, w['norm_ffn_g'][1], w['router_w'], tm=tm_light)
        y = _moe_routed(flat(h), flat(rt), counts, w['norm_ffn_g'][1], *experts, tm=MOE_TOKEN_TILE, tm_expert=MOE_EXPERT_TILE)
    else:
        h = _cross(*cross_args, tm=tm_light)
        y = _moe(flat(h), w['norm_ffn_g'][1], w['router_w'], *experts, tm=tm_wide)

    states = (
        fk.reshape(1, b, t, FOX_HEADS, FOX_DH), fv.reshape(1, b, t, FOX_HEADS, FOX_DH),
        per_b(gt)[None, :, :, _G_FOX:_G_IG],
        c_ext[None, :, :, 0:MLSTM_DV, 0:MLSTM_DK], c_ext[None, :, :, MLSTM_DV, 0:MLSTM_DK], m_ext[None, :, 0:MLSTM_HEADS, 0],
        conv_new[None, :, SUBLANES - (CONV_W - 1):, :],
    )
    return per_b(y), states


def kernel(x_prompt, x_sample, mem_prompt, cache_fox_k, cache_fox_v, cache_fox_logf, state_mlstm_c, state_mlstm_n, state_mlstm_m, state_conv, cache_mem_k, cache_mem_v, norm_mix_g, norm_mem_g, norm_cross_g, norm_ffn_g, final_norm_g, even_w_in, fox_b_f, mlstm_b_i, mlstm_b_f, mlstm_norm_g, even_w_out, odd_w_in, conv_w, odd_w_out, mem_wq, mem_wk, mem_wv, mem_wo, ffn_w_gate, ffn_w_up, ffn_w_down, router_w, moe_w_gate, moe_w_up, moe_w_down):
    w = _pack_params(dict(
        norm_mix_g=norm_mix_g, norm_cross_g=norm_cross_g, norm_ffn_g=norm_ffn_g, final_norm_g=final_norm_g,
        even_w_in=even_w_in, fox_b_f=fox_b_f, mlstm_b_i=mlstm_b_i, mlstm_b_f=mlstm_b_f, mlstm_norm_g=mlstm_norm_g,
        even_w_out=even_w_out, odd_w_in=odd_w_in, conv_w=conv_w, odd_w_out=odd_w_out, mem_wq=mem_wq, mem_wo=mem_wo,
        ffn_w_gate=ffn_w_gate, ffn_w_up=ffn_w_up, ffn_w_down=ffn_w_down, router_w=router_w,
        moe_w_gate=moe_w_gate, moe_w_up=moe_w_up, moe_w_down=moe_w_down))

    bp, tp, _ = x_prompt.shape
    bs, ts, _ = x_sample.shape
    depth = norm_mem_g.shape[0]

    mem_k_p, mem_v_p, mem_kb, mem_vb = _memkv(mem_prompt.reshape(bp * MEM_TOKENS, D_MODEL),
                                              norm_mem_g.reshape(depth, 1, D_MODEL).astype(F32),
                                              mem_wk.astype(BF16), mem_wv.astype(BF16), tm=512)
    per_layer = lambda a, nb: a.reshape(depth, nb, MEM_TOKENS, MEM_W)
    y_prompt, st_p = _trunk(x_prompt, per_layer(mem_kb, bp), per_layer(mem_vb, bp), None, w,
                            tm=512, tm_light=1024, tm_wide=1024, chunk=MLSTM_CHUNK)

    caches = (cache_fox_k[0], cache_fox_v[0], cache_fox_logf[0], state_mlstm_c[0], state_mlstm_n[0],
              state_mlstm_m[0], state_conv[0])
    y_sample, st_s = _trunk(x_sample, per_layer(cache_mem_k.astype(BF16), bs), per_layer(cache_mem_v.astype(BF16), bs),
                            caches, w, tm=ts, tm_light=ts, tm_wide=bs * ts, chunk=ts)

    mem_shape = (depth, bp, MEM_TOKENS, MEM_HEADS, MEM_DH)
    return (y_prompt, y_sample) + st_p + (mem_k_p.reshape(mem_shape), mem_v_p.reshape(mem_shape)) + st_s
```

```python
import functools

import jax
import jax.numpy as jnp
from jax import lax
from jax.experimental import pallas as pl
from jax.experimental.pallas import tpu as pltpu

F32 = jnp.float32
BF16 = jnp.bfloat16

D_MODEL = 1024
EPS = 1e-6
FOX_HEADS = 8
FOX_DH = 64
FOX_W = FOX_HEADS * FOX_DH
FOX_SCALE = FOX_DH ** -0.5
MLSTM_HEADS = 4
MLSTM_DV = 128
MLSTM_DK = 64
MLSTM_W = MLSTM_HEADS * MLSTM_DV
MLSTM_QK_W = MLSTM_HEADS * MLSTM_DK
MLSTM_SCALE = MLSTM_DK ** -0.5
MLSTM_CHUNK = 256
MLSTM_EXT = 256
MEM_TOKENS = 256
MEM_HEADS = 4
MEM_DH = 128
MEM_W = MEM_HEADS * MEM_DH
MEM_SCALE = MEM_DH ** -0.5
D_FF = 3584
N_EXPERTS = 8
TOP_K = 2
MOE_EXPERT_TILE = 1024
MOE_TOKEN_TILE = 512
MOE_DMA_UNROLL = 8
CONV_W = 3
EVEN_SIZES = (FOX_W, FOX_W, FOX_W, FOX_HEADS, MLSTM_QK_W, MLSTM_QK_W, MLSTM_W, MLSTM_W, MLSTM_HEADS, MLSTM_HEADS)
EVEN_SPLITS = tuple(sum(EVEN_SIZES[:i + 1]) for i in range(len(EVEN_SIZES) - 1))

LANES = 128
SUBLANES = 8
FOX_AUG_W = FOX_HEADS * LANES
FOX_TILE = 256
FOX_LOOKAHEAD = 3
FOX_PV_DELAY = 2
FF_TILE = 512
FF_SUB = 256
_C_Q, _C_K, _C_V, _C_MQ, _C_MK, _C_MV, _C_MO, _C_G, _C_END = 0, 512, 1024, 1536, 1792, 2048, 2560, 3072, 3200
_G_FOX, _G_IG, _G_LF, _G_END = 0, 8, 12, 16
NEG = -1e30
LOG2E = 1.4426950408889634


def _cp(sem, vmem_mb):
    return pltpu.CompilerParams(dimension_semantics=sem, vmem_limit_bytes=vmem_mb * 1024 * 1024)


def _rms(x, g):
    return x * lax.rsqrt(jnp.mean(x * x, axis=-1, keepdims=True) + EPS) * g


def _log_sigmoid(x):
    return jnp.minimum(x, 0.0) - jnp.log1p(jnp.exp(-jnp.abs(x)))


def _sigmoid(x):
    return 1.0 / (1.0 + jnp.exp(-x))


def _cumsum_rows(x):
    n = x.shape[0]
    row = lax.broadcasted_iota(jnp.int32, x.shape, 0)
    s = 1
    while s < n:
        x = x + jnp.where(row >= s, pltpu.roll(x, s, axis=0), 0.0)
        s *= 2
    return x


def _split3(f):
    hi = f.astype(BF16).astype(F32)
    r = f - hi
    mid = r.astype(BF16).astype(F32)
    return hi, mid, r - mid


def _head_block(src, h, lane):
    p, odd = divmod(h, 2)
    blk = src[:, LANES * p:LANES * (p + 1)]
    return pltpu.roll(blk, FOX_DH, axis=1) if odd else blk


_AUG_Q = (64, 72, 80)
_AUG_K = (88, 96, 104)


def _fox_bias_lanes(cum_f, lane):
    parts = _split3(jnp.where(lane < FOX_HEADS, cum_f * LOG2E, 0.0))
    q_side = sum(pltpu.roll(p, off, axis=1) for p, off in zip(parts, _AUG_Q))
    k_side = sum(pltpu.roll(-p, off, axis=1) for p, off in zip(parts, _AUG_K))
    return q_side, k_side


def _fox_aug(src, bias, h, is_query):
    lane = lax.broadcasted_iota(jnp.int32, (1, LANES), 1)
    blk = _head_block(src, h, lane)
    q_lanes = (lane == _AUG_Q[0] + h) | (lane == _AUG_Q[1] + h) | (lane == _AUG_Q[2] + h)
    k_lanes = (lane == _AUG_K[0] + h) | (lane == _AUG_K[1] + h) | (lane == _AUG_K[2] + h)
    own, ones = (q_lanes, k_lanes) if is_query else (k_lanes, q_lanes)
    aug = jnp.where(own, bias, jnp.where(ones, 1.0, 0.0))
    return jnp.where(lane < FOX_DH, blk, aug).astype(BF16)


def _even_in_kernel(x_ref, g_ref, w_ref, b_ref, f0_ref,
                    qa_ref, ka_ref, fk_ref, fv_ref, vb_ref, mq_ref, mk_ref, mv_ref, mo_ref, gt_ref,
                    carry_ref, *, tiles_per_batch):
    @pl.when(pl.program_id(0) % tiles_per_batch == 0)
    def _():
        carry_ref[...] = f0_ref[...]

    xn = _rms(x_ref[...], g_ref[...]).astype(BF16)
    acc = jnp.dot(xn, w_ref[...], preferred_element_type=F32)
    tm = acc.shape[0]
    lane = lax.broadcasted_iota(jnp.int32, (tm, LANES), 1)

    gates = acc[:, _C_G:_C_END] + b_ref[...]
    ls = _log_sigmoid(gates)
    is_ig = (lane >= _G_IG) & (lane < _G_LF)
    gt_ref[...] = jnp.where(is_ig, gates, jnp.where(lane < _G_END, ls, 0.0))
    cum_f = _cumsum_rows(jnp.where(lane < _G_IG, ls, 0.0)) + carry_ref[...]
    carry_ref[...] = cum_f[tm - 1:tm, :]

    q_all = acc[:, _C_Q:_C_K] * (FOX_SCALE * LOG2E)
    k_all = acc[:, _C_K:_C_V]
    v_all = acc[:, _C_V:_C_MQ]
    q_bias, k_bias = _fox_bias_lanes(cum_f, lane)
    for h in range(FOX_HEADS):
        qa_ref[:, LANES * h:LANES * (h + 1)] = _fox_aug(q_all, q_bias, h, True)
        ka_ref[:, LANES * h:LANES * (h + 1)] = _fox_aug(k_all, k_bias, h, False)
    for h in range(FOX_HEADS):
        fk_ref[:, h, :] = _head_block(k_all, h, lane)[:, 0:FOX_DH]
        fv_ref[:, h, :] = _head_block(v_all, h, lane)[:, 0:FOX_DH]
    vb_ref[...] = v_all.astype(BF16)

    mq_all = acc[:, _C_MQ:_C_MK] * MLSTM_SCALE
    mk_all = acc[:, _C_MK:_C_MV]
    for h in range(MLSTM_HEADS):
        mq_ref[:, LANES * h:LANES * (h + 1)] = jnp.where(lane < MLSTM_DK, _head_block(mq_all, h, lane), 0.0).astype(BF16)
        mk_ref[:, LANES * h:LANES * (h + 1)] = jnp.where(lane < MLSTM_DK, _head_block(mk_all, h, lane), 0.0).astype(BF16)
    mv_ref[...] = acc[:, _C_MV:_C_MO].astype(BF16)
    mo_ref[...] = acc[:, _C_MO:_C_G].astype(BF16)


def _even_in(x, g, w, bias, f0, *, tm, tiles_per_batch):
    n = x.shape[0]
    row = lambda i: (i, 0)
    fixed = lambda i: (0, 0)
    head_major = (FOX_HEADS, FOX_DH)
    widths = (FOX_AUG_W, FOX_AUG_W, head_major, head_major, FOX_W, MLSTM_W, MLSTM_W, MLSTM_W, MLSTM_W, LANES)
    dtypes = (BF16, BF16, F32, F32, BF16, BF16, BF16, BF16, BF16, F32)
    spec = lambda wd: (pl.BlockSpec((tm,) + wd, lambda i: (i, 0, 0)) if isinstance(wd, tuple) else pl.BlockSpec((tm, wd), row))
    shape = lambda wd: (n,) + wd if isinstance(wd, tuple) else (n, wd)
    return pl.pallas_call(
        functools.partial(_even_in_kernel, tiles_per_batch=tiles_per_batch),
        grid=(n // tm,),
        in_specs=[pl.BlockSpec((tm, D_MODEL), row), pl.BlockSpec((1, D_MODEL), fixed),
                  pl.BlockSpec((D_MODEL, _C_END), fixed), pl.BlockSpec((1, LANES), fixed),
                  pl.BlockSpec((None, 1, LANES), lambda i: (i // tiles_per_batch, 0, 0))],
        out_specs=[spec(wd) for wd in widths],
        out_shape=[jax.ShapeDtypeStruct(shape(wd), dt) for wd, dt in zip(widths, dtypes)],
        scratch_shapes=[pltpu.VMEM((1, LANES), F32)],
        compiler_params=_cp(("arbitrary",), 48),
        name="even_in",
    )(x, g, w, bias, f0)


def _cache_prep_kernel(k_ref, lf_ref, ka_ref, fend_ref, carry_ref):
    @pl.when(pl.program_id(1) == 0)
    def _():
        carry_ref[...] = jnp.zeros_like(carry_ref)

    k_all = k_ref[...]
    tm = k_all.shape[0]
    lane = lax.broadcasted_iota(jnp.int32, (tm, LANES), 1)
    cum_f = _cumsum_rows(lf_ref[...]) + carry_ref[...]
    carry_ref[...] = cum_f[tm - 1:tm, :]
    fend_ref[...] = cum_f[tm - 1:tm, :]
    _, k_bias = _fox_bias_lanes(cum_f, lane)
    for h in range(FOX_HEADS):
        ka_ref[:, LANES * h:LANES * (h + 1)] = _fox_aug(k_all, k_bias, h, False)


def _cache_prep(cache_k, cache_lf, *, tm):
    b, p, _ = cache_k.shape
    return pl.pallas_call(
        _cache_prep_kernel,
        grid=(b, p // tm),
        in_specs=[pl.BlockSpec((None, tm, FOX_W), lambda i, j: (i, j, 0)),
                  pl.BlockSpec((None, tm, LANES), lambda i, j: (i, j, 0))],
        out_specs=[pl.BlockSpec((None, tm, FOX_AUG_W), lambda i, j: (i, j, 0)),
                   pl.BlockSpec((None, 1, LANES), lambda i, j: (i, 0, 0))],
        out_shape=[jax.ShapeDtypeStruct((b, p, FOX_AUG_W), BF16), jax.ShapeDtypeStruct((b, 1, LANES), F32)],
        scratch_shapes=[pltpu.VMEM((1, LANES), F32)],
        compiler_params=_cp(("arbitrary", "arbitrary"), 32),
        name="fox_cache_prep",
    )(cache_k, cache_lf)


def _fox_kernel(q_ref, k_ref, v_ref, o_ref, m_sc, l_sc, acc_sc, s_sc, *, q0):
    t = FOX_TILE
    n_full = q0 + pl.program_id(1)
    visible = (lax.broadcasted_iota(jnp.int32, (t, t), 0) <= lax.broadcasted_iota(jnp.int32, (t, t), 1))
    m_sc[...] = jnp.full_like(m_sc, NEG)
    l_sc[...] = jnp.zeros_like(l_sc)
    acc_sc[...] = jnp.zeros_like(acc_sc)

    def scores(j, h):
        hl = slice(LANES * h, LANES * (h + 1))
        return lax.dot_general(k_ref[pl.ds(pl.multiple_of(j * t, t), t), hl], q_ref[:, hl], (((1,), (1,)), ((), ())),
                               preferred_element_type=F32)

    def step(j, diagonal):
        off = pl.multiple_of(j * t, t)

        def weighted_values(h, p, alpha):
            pair, hh = divmod(h, 2)
            v_t = v_ref[pl.ds(off, t), LANES * pair:LANES * (pair + 1)].astype(BF16).T
            v_ext = jnp.concatenate([v_t[FOX_DH * hh:FOX_DH * (hh + 1), :], jnp.ones((2 * SUBLANES, t), BF16)], axis=0)
            pv = jnp.dot(v_ext, p, preferred_element_type=F32)
            rows = slice(FOX_DH * h, FOX_DH * (h + 1))
            acc_sc[rows, :] = alpha * acc_sc[rows, :] + pv[0:FOX_DH, :]
            l_sc[h:h + 1, :] = alpha * l_sc[h:h + 1, :] + pv[FOX_DH:FOX_DH + 1, :]

        pending = {h: s_sc[h] for h in range(FOX_LOOKAHEAD)}
        ready = {}
        for h in range(FOX_HEADS):
            s = pending.pop(h)
            if diagonal:
                s = jnp.where(visible, s, NEG)
            m_old = m_sc[h:h + 1, :]
            m_new = jnp.maximum(m_old, jnp.max(s, axis=0, keepdims=True))
            alpha = jnp.exp2(m_old - m_new)
            p = jnp.exp2(s - m_new)
            m_sc[h:h + 1, :] = m_new
            ready[h] = (p.astype(BF16), alpha)
            ahead = h + FOX_LOOKAHEAD
            if ahead < FOX_HEADS:
                pending[ahead] = scores(j, ahead)
            elif not diagonal:
                s_sc[ahead - FOX_HEADS] = scores(j + 1, ahead - FOX_HEADS)
            if h - FOX_PV_DELAY in ready:
                weighted_values(h - FOX_PV_DELAY, *ready.pop(h - FOX_PV_DELAY))
        for h in sorted(ready):
            weighted_values(h, *ready[h])

    def body(j, c):
        step(j, False)
        return c

    for h in range(FOX_LOOKAHEAD):
        s_sc[h] = scores(0, h)
    lax.fori_loop(0, n_full, body, 0)
    step(n_full, True)
    for h in range(FOX_HEADS):
        rows = slice(FOX_DH * h, FOX_DH * (h + 1))
        o_ref[rows, :] = (acc_sc[rows, :] / l_sc[h:h + 1, :]).astype(BF16)


def _fox_attention(q_aug, k_aug, v, *, q0):
    b, tq_total, _ = q_aug.shape
    t_kv = k_aug.shape[1]
    nq = tq_total // FOX_TILE
    return pl.pallas_call(
        functools.partial(_fox_kernel, q0=q0),
        grid=(b, nq),
        in_specs=[pl.BlockSpec((None, FOX_TILE, FOX_AUG_W), lambda bi, i: (bi, i, 0)),
                  pl.BlockSpec((None, t_kv, FOX_AUG_W), lambda bi, i: (bi, 0, 0)),
                  pl.BlockSpec((None, t_kv, FOX_W), lambda bi, i: (bi, 0, 0))],
        out_specs=pl.BlockSpec((None, FOX_W, FOX_TILE), lambda bi, i: (bi, 0, i)),
        out_shape=jax.ShapeDtypeStruct((b, FOX_W, tq_total), BF16),
        scratch_shapes=[pltpu.VMEM((FOX_HEADS, FOX_TILE), F32), pltpu.VMEM((FOX_HEADS, FOX_TILE), F32),
                        pltpu.VMEM((FOX_W, FOX_TILE), F32), pltpu.VMEM((FOX_LOOKAHEAD, FOX_TILE, FOX_TILE), F32)],
        compiler_params=_cp(("arbitrary", "arbitrary"), 48),
        name="fox_attention",
    )(q_aug, k_aug, v)


def _mlstm_kernel(mq_ref, mk_ref, mv_ref, mo_ref, gt_ref, c0_ref, m0_ref, ng_ref,
                  hm_ref, co_ref, mout_ref, c_sc, m_sc, gt_t, cs_t, *, chunk):
    step = pl.program_id(1)

    @pl.when(step == 0)
    def _():
        c_sc[...] = c0_ref[...]
        m_sc[...] = m0_ref[...]

    ln = chunk
    nt = (((1,), (1,)), ((), ()))
    causal = lax.broadcasted_iota(jnp.int32, (ln, ln), 0) >= lax.broadcasted_iota(jnp.int32, (ln, ln), 1)
    heads = range(MLSTM_HEADS)
    hl = [slice(LANES * h, LANES * (h + 1)) for h in heads]

    g = gt_ref[...]
    cs = _cumsum_rows(g)
    if ln < LANES:
        pad = jnp.zeros((LANES - ln, LANES), F32)
        gt_t[...] = jnp.concatenate([g, pad], axis=0).T
        cs_t[...] = jnp.concatenate([cs, pad], axis=0).T
    else:
        gt_t[...] = g.T
        cs_t[...] = cs.T
    rep = lambda col: jnp.broadcast_to(col, (ln, LANES))
    wide = (lambda r: jnp.concatenate([r] * (ln // LANES), axis=1)) if ln >= LANES else (lambda r: r[:, 0:ln])
    ig = [rep(g[:, _G_IG + h:_G_IG + h + 1]) for h in heads]
    b = [rep(cs[:, _G_LF + h:_G_LF + h + 1]) for h in heads]
    b_last = [x[ln - 1:ln, :] for x in b]

    qk = [lax.dot_general(mq_ref[:, hl[h]], mk_ref[:, hl[h]], nt, preferred_element_type=F32) for h in heads]
    m_loc, a_sum, av, g_max, upd = [], [], [], [], []
    for h in heads:
        d = jnp.where(causal, wide(b[h]) - cs_t[_G_LF + h:_G_LF + h + 1, 0:ln] + gt_t[_G_IG + h:_G_IG + h + 1, 0:ln],
                      -jnp.inf)
        m_loc.append(rep(jnp.max(d, axis=1, keepdims=True)))
        a = qk[h] * jnp.exp(d - wide(m_loc[h]))
        a_sum.append(rep(jnp.sum(a, axis=1, keepdims=True)))
        av.append(jnp.dot(a.astype(BF16), mv_ref[:, hl[h]], preferred_element_type=F32))
    for h in heads:
        g_tok = b_last[h] - b[h] + ig[h]
        g_max.append(jnp.max(g_tok, axis=0, keepdims=True))
        w_loc = jnp.exp(g_tok - g_max[h])
        vw = jnp.concatenate([mv_ref[:, hl[h]].astype(F32) * w_loc, w_loc], axis=1)
        upd.append(lax.dot_general(vw.astype(BF16), mk_ref[:, hl[h]], (((0,), (0,)), ((), ())),
                                   preferred_element_type=F32))

    for h in heads:
        m_prev = m_sc[h:h + 1, :]
        c_prev = c_sc[h]
        cq = lax.dot_general(mq_ref[:, hl[h]], c_prev.astype(BF16), nt, preferred_element_type=F32)
        inter = b[h] + m_prev
        m_t = jnp.maximum(inter, m_loc[h])
        w_inter = jnp.exp(inter - m_t)
        w_intra = jnp.exp(m_loc[h] - m_t)
        num = w_inter * cq[:, 0:MLSTM_DV] + w_intra * av[h]
        den = w_inter * cq[:, MLSTM_DV:] + w_intra * a_sum[h]
        h_cell = num / jnp.maximum(jnp.abs(den), jnp.exp(-m_t))

        m_new = jnp.maximum(b_last[h] + m_prev, g_max[h])
        c_sc[h] = jnp.exp(b_last[h] + m_prev - m_new) * c_prev + jnp.exp(g_max[h] - m_new) * upd[h]
        m_sc[h:h + 1, :] = m_new

        hn = h_cell * lax.rsqrt(jnp.mean(h_cell * h_cell, axis=1, keepdims=True) + EPS)
        hm_ref[:, hl[h]] = (hn * ng_ref[:, hl[h]] * _sigmoid(mo_ref[:, hl[h]].astype(F32))).astype(BF16)

    @pl.when(step == pl.num_programs(1) - 1)
    def _():
        co_ref[...] = c_sc[...]
        mout_ref[...] = m_sc[...]


def _mlstm(mq, mk, mv, mo, gt, c0, m0, ng, *, chunk):
    b, t, _ = mq.shape
    tok = lambda wd: pl.BlockSpec((None, chunk, wd), lambda i, j: (i, j, 0))
    c_spec = pl.BlockSpec((None, MLSTM_HEADS, MLSTM_EXT, LANES), lambda i, j: (i, 0, 0, 0))
    m_spec = pl.BlockSpec((None, SUBLANES, LANES), lambda i, j: (i, 0, 0))
    t_cols = max(chunk, LANES)
    return pl.pallas_call(
        functools.partial(_mlstm_kernel, chunk=chunk),
        grid=(b, t // chunk),
        in_specs=[tok(MLSTM_W), tok(MLSTM_W), tok(MLSTM_W), tok(MLSTM_W), tok(LANES), c_spec, m_spec,
                  pl.BlockSpec((1, MLSTM_W), lambda i, j: (0, 0))],
        out_specs=[tok(MLSTM_W), c_spec, m_spec],
        out_shape=[jax.ShapeDtypeStruct((b, t, MLSTM_W), BF16),
                   jax.ShapeDtypeStruct((b, MLSTM_HEADS, MLSTM_EXT, LANES), F32),
                   jax.ShapeDtypeStruct((b, SUBLANES, LANES), F32)],
        scratch_shapes=[pltpu.VMEM((MLSTM_HEADS, MLSTM_EXT, LANES), F32), pltpu.VMEM((SUBLANES, LANES), F32),
                        pltpu.VMEM((LANES, t_cols), F32), pltpu.VMEM((LANES, t_cols), F32)],
        compiler_params=_cp(("arbitrary", "arbitrary"), 32),
        name="mlstm",
    )(mq, mk, mv, mo, gt, c0, m0, ng)


def _even_out_kernel(x_ref, fox_ref, hm_ref, w_ref, o_ref, *, fox_transposed):
    dims = (((0,), (0,)), ((), ())) if fox_transposed else (((1,), (0,)), ((), ()))
    y = lax.dot_general(fox_ref[...], w_ref[0:FOX_W, :], dims, preferred_element_type=F32)
    y = y + jnp.dot(hm_ref[...], w_ref[FOX_W:, :], preferred_element_type=F32)
    o_ref[...] = x_ref[...] + y


def _even_out(x, fox, hm, w, *, tm, fox_transposed):
    b, t, _ = x.shape
    fox_spec = (pl.BlockSpec((None, FOX_W, tm), lambda i, j: (i, 0, j)) if fox_transposed
                else pl.BlockSpec((None, tm, FOX_W), lambda i, j: (i, j, 0)))
    return pl.pallas_call(
        functools.partial(_even_out_kernel, fox_transposed=fox_transposed),
        grid=(b, t // tm),
        in_specs=[pl.BlockSpec((None, tm, D_MODEL), lambda i, j: (i, j, 0)), fox_spec,
                  pl.BlockSpec((None, tm, MLSTM_W), lambda i, j: (i, j, 0)),
                  pl.BlockSpec((D_MODEL, D_MODEL), lambda i, j: (0, 0))],
        out_specs=pl.BlockSpec((None, tm, D_MODEL), lambda i, j: (i, j, 0)),
        out_shape=jax.ShapeDtypeStruct((b, t, D_MODEL), F32),
        compiler_params=_cp(("arbitrary", "arbitrary"), 48),
        name="even_out",
    )(x, fox, hm, w)


def _cross_body(h_ref, g_ref, wq_ref, wo_ref, mk_ref, mv_ref):
    x = h_ref[...]
    xn = _rms(x, g_ref[...]).astype(BF16)
    q = jnp.dot(xn, wq_ref[...], preferred_element_type=F32).astype(BF16)
    hls = [slice(MEM_DH * h, MEM_DH * (h + 1)) for h in range(MEM_HEADS)]
    scores = [lax.dot_general(q[:, hl], mk_ref[:, hl], (((1,), (1,)), ((), ())), preferred_element_type=F32) * MEM_SCALE
              for hl in hls]
    outs = []
    for s, hl in zip(scores, hls):
        e = jnp.exp(s - jnp.max(s, axis=1, keepdims=True))
        p = e * (1.0 / jnp.sum(e, axis=1, keepdims=True))
        outs.append(jnp.dot(p.astype(BF16), mv_ref[:, hl], preferred_element_type=F32).astype(BF16))
    o = jnp.concatenate(outs, axis=1)
    return x + jnp.dot(o, wo_ref[...], preferred_element_type=F32)


def _cross_kernel(h_ref, g_ref, wq_ref, wo_ref, mk_ref, mv_ref, o_ref):
    o_ref[...] = _cross_body(h_ref, g_ref, wq_ref, wo_ref, mk_ref, mv_ref)


def _cross_route_kernel(h_ref, g_ref, wq_ref, wo_ref, mk_ref, mv_ref, gr_ref, rw_ref, o_ref, rt_ref, cnt_ref, carry_ref):
    @pl.when((pl.program_id(0) == 0) & (pl.program_id(1) == 0))
    def _():
        carry_ref[...] = jnp.zeros_like(carry_ref)

    out = _cross_body(h_ref, g_ref, wq_ref, wo_ref, mk_ref, mv_ref)
    o_ref[...] = out
    rt_ref[...] = _route_records(_rms(out, gr_ref[...]), rw_ref, carry_ref)
    cnt_ref[...] = carry_ref[...]


def _cross_route(h, g, wq, wo, mem_k, mem_v, g_ffn, rw, *, tm):
    b, t, _ = h.shape
    tok = pl.BlockSpec((None, tm, D_MODEL), lambda i, j: (i, j, 0))
    mem = pl.BlockSpec((None, MEM_TOKENS, MEM_W), lambda i, j: (i, 0, 0))
    fixed = lambda r, c: pl.BlockSpec((r, c), lambda i, j: (0, 0))
    return pl.pallas_call(
        _cross_route_kernel,
        grid=(b, t // tm),
        in_specs=[tok, fixed(1, D_MODEL), fixed(D_MODEL, MEM_W), fixed(MEM_W, D_MODEL), mem, mem,
                  fixed(1, D_MODEL), fixed(D_MODEL, LANES)],
        out_specs=[tok, pl.BlockSpec((None, tm, LANES), lambda i, j: (i, j, 0)), fixed(1, LANES)],
        out_shape=[jax.ShapeDtypeStruct((b, t, D_MODEL), F32), jax.ShapeDtypeStruct((b, t, LANES), F32),
                   jax.ShapeDtypeStruct((1, LANES), F32)],
        scratch_shapes=[pltpu.VMEM((1, LANES), F32)],
        compiler_params=_cp(("arbitrary", "arbitrary"), 48),
        name="cross_attention_router",
    )(h, g, wq, wo, mem_k, mem_v, g_ffn, rw)


def _cross(h, g, wq, wo, mem_k, mem_v, *, tm):
    b, t, _ = h.shape
    tok = pl.BlockSpec((None, tm, D_MODEL), lambda i, j: (i, j, 0))
    mem = pl.BlockSpec((None, MEM_TOKENS, MEM_W), lambda i, j: (i, 0, 0))
    return pl.pallas_call(
        _cross_kernel,
        grid=(b, t // tm),
        in_specs=[tok, pl.BlockSpec((1, D_MODEL), lambda i, j: (0, 0)),
                  pl.BlockSpec((D_MODEL, MEM_W), lambda i, j: (0, 0)),
                  pl.BlockSpec((MEM_W, D_MODEL), lambda i, j: (0, 0)), mem, mem],
        out_specs=tok,
        out_shape=jax.ShapeDtypeStruct((b, t, D_MODEL), F32),
        compiler_params=_cp(("arbitrary", "arbitrary"), 48),
        name="cross_attention",
    )(h, g, wq, wo, mem_k, mem_v)


def _memkv_kernel(mem_ref, g_ref, wk_ref, wv_ref, k_ref, v_ref, kb_ref, vb_ref):
    mn = _rms(mem_ref[...], g_ref[...]).astype(BF16)
    k = jnp.dot(mn, wk_ref[...], preferred_element_type=F32)
    v = jnp.dot(mn, wv_ref[...], preferred_element_type=F32)
    k_ref[...] = k
    v_ref[...] = v
    kb_ref[...] = k.astype(BF16)
    vb_ref[...] = v.astype(BF16)


def _memkv(mem, g, wk, wv, *, tm):
    n = mem.shape[0]
    depth = g.shape[0]
    w_spec = pl.BlockSpec((None, D_MODEL, MEM_W), lambda l, i: (l, 0, 0))
    o_spec = pl.BlockSpec((None, tm, MEM_W), lambda l, i: (l, i, 0))
    return pl.pallas_call(
        _memkv_kernel,
        grid=(depth, n // tm),
        in_specs=[pl.BlockSpec((tm, D_MODEL), lambda l, i: (i, 0)),
                  pl.BlockSpec((None, 1, D_MODEL), lambda l, i: (l, 0, 0)), w_spec, w_spec],
        out_specs=[o_spec] * 4,
        out_shape=[jax.ShapeDtypeStruct((depth, n, MEM_W), dt) for dt in (F32, F32, BF16, BF16)],
        compiler_params=_cp(("arbitrary", "arbitrary"), 32),
        name="memory_kv",
    )(mem, g, wk, wv)


def _silu(x):
    return x * _sigmoid(x)


def _swiglu_chunk(x, wg_ref, wu_ref, wd_ref):
    gate_up = []
    for s in range(FF_TILE // FF_SUB):
        cols = slice(s * FF_SUB, (s + 1) * FF_SUB)
        gate_up.append((jnp.dot(x, wg_ref[:, cols], preferred_element_type=F32),
                        jnp.dot(x, wu_ref[:, cols], preferred_element_type=F32)))
    y = None
    for s, (g, u) in enumerate(gate_up):
        part = jnp.dot((_silu(g) * u).astype(BF16), wd_ref[s * FF_SUB:(s + 1) * FF_SUB, :], preferred_element_type=F32)
        y = part if y is None else y + part
    return y


def _ffn_kernel(h_ref, g_ref, wg_ref, wu_ref, wd_ref, o_ref, xn_sc):
    @pl.when(pl.program_id(1) == 0)
    def _():
        x = h_ref[...]
        xn_sc[...] = _rms(x, g_ref[...]).astype(BF16)
        o_ref[...] = x

    o_ref[...] += _swiglu_chunk(xn_sc[...], wg_ref, wu_ref, wd_ref)


def _ffn(h, g, wg, wu, wd, *, tm):
    n = h.shape[0]
    tok = pl.BlockSpec((tm, D_MODEL), lambda i, j: (i, 0))
    return pl.pallas_call(
        _ffn_kernel,
        grid=(n // tm, D_FF // FF_TILE),
        in_specs=[tok, pl.BlockSpec((1, D_MODEL), lambda i, j: (0, 0)),
                  pl.BlockSpec((D_MODEL, FF_TILE), lambda i, j: (0, j)),
                  pl.BlockSpec((D_MODEL, FF_TILE), lambda i, j: (0, j)),
                  pl.BlockSpec((FF_TILE, D_MODEL), lambda i, j: (j, 0))],
        out_specs=tok,
        out_shape=jax.ShapeDtypeStruct((n, D_MODEL), F32),
        scratch_shapes=[pltpu.VMEM((tm, D_MODEL), BF16)],
        compiler_params=_cp(("arbitrary", "arbitrary"), 48),
        name="dense_swiglu",
    )(h, g, wg, wu, wd)


def _odd_kernel(h_ref, g_ref, win_ref, cw_ref, wout_ref, past_ref, o_ref, st_ref, u_sc):
    j = pl.program_id(1)
    tm = h_ref.shape[0]

    @pl.when(j == 0)
    def _():
        u_sc[0:SUBLANES, :] = past_ref[...]

    @pl.when(j > 0)
    def _():
        u_sc[0:SUBLANES, :] = u_sc[tm:tm + SUBLANES, :]

    x = h_ref[...]
    xn = _rms(x, g_ref[...]).astype(BF16)
    z = jnp.dot(xn, win_ref[...], preferred_element_type=F32)
    gate_b = z[:, 0:D_MODEL]
    u = z[:, D_MODEL:2 * D_MODEL] * z[:, 2 * D_MODEL:]
    u_sc[SUBLANES:, :] = u
    conv = (cw_ref[0:1, :] * u_sc[SUBLANES - 2:SUBLANES - 2 + tm, :]
            + cw_ref[1:2, :] * u_sc[SUBLANES - 1:SUBLANES - 1 + tm, :]
            + cw_ref[2:3, :] * u)
    o_ref[...] = x + jnp.dot((gate_b * conv).astype(BF16), wout_ref[...], preferred_element_type=F32)

    @pl.when(j == pl.num_programs(1) - 1)
    def _():
        st_ref[...] = u_sc[tm:tm + SUBLANES, :]


def _odd(h, g, w_in, cw, w_out, past, *, tm):
    b, t, _ = h.shape
    tok = pl.BlockSpec((None, tm, D_MODEL), lambda i, j: (i, j, 0))
    st = pl.BlockSpec((None, SUBLANES, D_MODEL), lambda i, j: (i, 0, 0))
    return pl.pallas_call(
        _odd_kernel,
        grid=(b, t // tm),
        in_specs=[tok, pl.BlockSpec((1, D_MODEL), lambda i, j: (0, 0)),
                  pl.BlockSpec((D_MODEL, 3 * D_MODEL), lambda i, j: (0, 0)),
                  pl.BlockSpec((SUBLANES, D_MODEL), lambda i, j: (0, 0)),
                  pl.BlockSpec((D_MODEL, D_MODEL), lambda i, j: (0, 0)), st],
        out_specs=[tok, st],
        out_shape=[jax.ShapeDtypeStruct((b, t, D_MODEL), F32), jax.ShapeDtypeStruct((b, SUBLANES, D_MODEL), F32)],
        scratch_shapes=[pltpu.VMEM((tm + SUBLANES, D_MODEL), F32)],
        compiler_params=_cp(("arbitrary", "arbitrary"), 48),
        name="short_conv_mixer",
    )(h, g, w_in, cw, w_out, past)


def _top2(logits, lane_f):
    lg = jnp.where(lane_f < N_EXPERTS, logits, -jnp.inf)
    m1 = jnp.max(lg, axis=1, keepdims=True)
    i1 = jnp.min(jnp.where(lg == m1, lane_f, float(LANES)), axis=1, keepdims=True)
    lg2 = jnp.where(lane_f == i1, -jnp.inf, lg)
    m2 = jnp.max(lg2, axis=1, keepdims=True)
    i2 = jnp.min(jnp.where(lg2 == m2, lane_f, float(LANES)), axis=1, keepdims=True)
    e2 = jnp.exp(m2 - m1)
    inv = 1.0 / (1.0 + e2)
    return i1, i2, inv, e2 * inv


def _route(logits, lane_f):
    i1, i2, g1, g2 = _top2(logits, lane_f)
    return jnp.where(lane_f == i1, g1, 0.0) + jnp.where(lane_f == i2, g2, 0.0)


def _moe_kernel(h_ref, g_ref, rw_ref, wg_ref, wu_ref, wd_ref, fg_ref, o_ref, xn_sc, comb_sc):
    e = pl.program_id(1)
    j = pl.program_id(2)
    tm = h_ref.shape[0]
    lane = lax.broadcasted_iota(jnp.int32, (tm, LANES), 1)

    @pl.when((e == 0) & (j == 0))
    def _():
        x = h_ref[...]
        xn = _rms(x, g_ref[...]).astype(BF16)
        xn_sc[...] = xn
        logits = jnp.dot(xn, rw_ref[...], preferred_element_type=F32)
        comb_sc[...] = _route(logits, lane.astype(F32))
        o_ref[...] = x

    y = _swiglu_chunk(xn_sc[...], wg_ref, wu_ref, wd_ref)
    w_e = jnp.sum(jnp.where(lane == e, comb_sc[...], 0.0), axis=1, keepdims=True)
    o_ref[...] += y * w_e

    @pl.when((e == pl.num_programs(1) - 1) & (j == pl.num_programs(2) - 1))
    def _():
        o_ref[...] = _rms(o_ref[...], fg_ref[...])


def _moe(h, g, rw, wg, wu, wd, fg, *, tm):
    n = h.shape[0]
    tok = pl.BlockSpec((tm, D_MODEL), lambda i, e, j: (i, 0))
    vec = pl.BlockSpec((1, D_MODEL), lambda i, e, j: (0, 0))
    return pl.pallas_call(
        _moe_kernel,
        grid=(n // tm, N_EXPERTS, D_FF // FF_TILE),
        in_specs=[tok, vec, pl.BlockSpec((D_MODEL, LANES), lambda i, e, j: (0, 0)),
                  pl.BlockSpec((None, D_MODEL, FF_TILE), lambda i, e, j: (e, 0, j)),
                  pl.BlockSpec((None, D_MODEL, FF_TILE), lambda i, e, j: (e, 0, j)),
                  pl.BlockSpec((None, FF_TILE, D_MODEL), lambda i, e, j: (e, j, 0)), vec],
        out_specs=tok,
        out_shape=jax.ShapeDtypeStruct((n, D_MODEL), F32),
        scratch_shapes=[pltpu.VMEM((tm, D_MODEL), BF16), pltpu.VMEM((tm, LANES), F32)],
        compiler_params=_cp(("arbitrary", "arbitrary", "arbitrary"), 48),
        name="moe_swiglu",
    )(h, g, rw, wg, wu, wd, fg)


_R_E1, _R_E2, _R_G1, _R_G2, _R_P1, _R_P2 = range(6)


def _route_records(xn, rw_ref, carry_ref):
    tm = xn.shape[0]
    lane = lax.broadcasted_iota(jnp.int32, (tm, LANES), 1)
    lane_f = lane.astype(F32)
    logits = jnp.dot(xn.astype(BF16), rw_ref[...], preferred_element_type=F32)
    i1, i2, g1, g2 = _top2(logits, lane_f)
    sel = jnp.where((lane_f == i1) | (lane_f == i2), 1.0, 0.0)
    incl = _cumsum_rows(sel)
    rank = incl - sel + carry_ref[...]
    p1 = jnp.sum(jnp.where(lane_f == i1, rank, 0.0), axis=1, keepdims=True)
    p2 = jnp.sum(jnp.where(lane_f == i2, rank, 0.0), axis=1, keepdims=True)
    carry_ref[...] = carry_ref[...] + incl[tm - 1:tm, :]
    rec = jnp.zeros((tm, LANES), F32)
    for ln, val in ((_R_E1, i1), (_R_E2, i2), (_R_G1, g1), (_R_G2, g2), (_R_P1, p1), (_R_P2, p2)):
        rec = jnp.where(lane == ln, val, rec)
    return rec


def _row_copy(src, src_row, dst, dst_row, sem):
    return pltpu.make_async_copy(src.at[pl.ds(src_row, 1), :], dst.at[pl.ds(dst_row, 1), :], sem)


def _dispatch_kernel(ends_ref, dest_ref, h_ref, g_ref, xs_ref, zero_sc, x_ref, sem, *, tm_expert):
    tm = h_ref.shape[0]
    x_ref[...] = _rms(h_ref[...], g_ref[...])

    @pl.when(pl.program_id(0) == 0)
    def _():
        zero_sc[...] = jnp.zeros_like(zero_sc)

        def zero_tile(first_row, wanted):
            @pl.when(wanted)
            def _():
                cp = pltpu.make_async_copy(zero_sc, xs_ref.at[pl.ds(pl.multiple_of(first_row, tm_expert), tm_expert), :], sem)
                cp.start()
                cp.wait()

        for e in range(N_EXPERTS):
            begin = ends_ref[e - 1] if e else 0
            zero_tile(ends_ref[e] - tm_expert, ends_ref[e] > begin)
        for k in range(N_EXPERTS):
            first_row = ends_ref[N_EXPERTS - 1] + k * tm_expert
            zero_tile(first_row, first_row < xs_ref.shape[0])

    def start(t, c):
        _row_copy(x_ref, t, xs_ref, dest_ref[0, 2 * t], sem).start(priority=0)
        _row_copy(x_ref, t, xs_ref, dest_ref[0, 2 * t + 1], sem).start(priority=1)
        return c

    def wait(t, c):
        _row_copy(x_ref, t, xs_ref, dest_ref[0, 2 * t], sem).wait()
        _row_copy(x_ref, t, xs_ref, dest_ref[0, 2 * t + 1], sem).wait()
        return c

    lax.fori_loop(0, tm, start, 0, unroll=MOE_DMA_UNROLL)
    lax.fori_loop(0, tm, wait, 0, unroll=MOE_DMA_UNROLL)


def _dispatch(ends, dest, h, g, *, rows, tm, tm_expert):
    n = h.shape[0]
    return pl.pallas_call(
        functools.partial(_dispatch_kernel, tm_expert=tm_expert),
        grid_spec=pltpu.PrefetchScalarGridSpec(
            num_scalar_prefetch=1,
            grid=(n // tm,),
            in_specs=[pl.BlockSpec((None, 1, 2 * tm), lambda i, ends: (i, 0, 0), memory_space=pltpu.SMEM),
                      pl.BlockSpec((tm, D_MODEL), lambda i, ends: (i, 0)),
                      pl.BlockSpec((1, D_MODEL), lambda i, ends: (0, 0))],
            out_specs=pl.BlockSpec(memory_space=pl.ANY),
            scratch_shapes=[pltpu.VMEM((tm_expert, D_MODEL), F32), pltpu.VMEM((tm, D_MODEL), F32),
                            pltpu.SemaphoreType.DMA(())]),
        out_shape=jax.ShapeDtypeStruct((rows, D_MODEL), F32),
        compiler_params=_cp(("arbitrary",), 32),
        name="moe_dispatch",
    )(ends, dest.reshape(n // tm, 1, 2 * tm), h, g)


def _expert_kernel(te_ref, nv_ref, x_ref, wg_ref, wu_ref, wd_ref, o_ref, xb_sc):
    n_valid = nv_ref[pl.program_id(0)]

    @pl.when(pl.program_id(1) == 0)
    def _():
        xb_sc[...] = x_ref[...].astype(BF16)
        o_ref[...] = jnp.zeros_like(o_ref)

    @pl.when(n_valid > 0)
    def _():
        o_ref[...] += _swiglu_chunk(xb_sc[...], wg_ref, wu_ref, wd_ref)


def _experts(tile_expert, tile_valid, xs, wg, wu, wd, *, tm):
    rows = xs.shape[0]
    tok = pl.BlockSpec((tm, D_MODEL), lambda i, j, te, nv: (i, 0))
    return pl.pallas_call(
        _expert_kernel,
        grid_spec=pltpu.PrefetchScalarGridSpec(
            num_scalar_prefetch=2,
            grid=(rows // tm, D_FF // FF_TILE),
            in_specs=[tok,
                      pl.BlockSpec((None, D_MODEL, FF_TILE), lambda i, j, te, nv: (te[i], 0, j)),
                      pl.BlockSpec((None, D_MODEL, FF_TILE), lambda i, j, te, nv: (te[i], 0, j)),
                      pl.BlockSpec((None, FF_TILE, D_MODEL), lambda i, j, te, nv: (te[i], j, 0))],
            out_specs=tok,
            scratch_shapes=[pltpu.VMEM((tm, D_MODEL), BF16)]),
        out_shape=jax.ShapeDtypeStruct((rows, D_MODEL), F32),
        compiler_params=_cp(("arbitrary", "arbitrary"), 48),
        name="moe_experts",
    )(tile_expert, tile_valid, xs, wg, wu, wd)


def _combine_kernel(dest_ref, h_ref, rt_ref, fg_ref, ys_ref, o_ref, y1_sc, y2_sc, sem):
    tm = h_ref.shape[0]

    def start(t, c):
        _row_copy(ys_ref, dest_ref[0, 2 * t], y1_sc, t, sem).start(priority=0)
        _row_copy(ys_ref, dest_ref[0, 2 * t + 1], y2_sc, t, sem).start(priority=1)
        return c

    def wait(t, c):
        _row_copy(ys_ref, dest_ref[0, 2 * t], y1_sc, t, sem).wait()
        _row_copy(ys_ref, dest_ref[0, 2 * t + 1], y2_sc, t, sem).wait()
        return c

    lax.fori_loop(0, tm, start, 0, unroll=MOE_DMA_UNROLL)
    lax.fori_loop(0, tm, wait, 0, unroll=MOE_DMA_UNROLL)
    rt = rt_ref[...]
    moe = rt[:, _R_G1:_R_G1 + 1] * y1_sc[...] + rt[:, _R_G2:_R_G2 + 1] * y2_sc[...]
    o_ref[...] = _rms(h_ref[...] + moe, fg_ref[...])


def _combine(dest, h, rt, fg, ys, *, tm):
    n = h.shape[0]
    tok = pl.BlockSpec((tm, D_MODEL), lambda i: (i, 0))
    return pl.pallas_call(
        _combine_kernel,
        grid=(n // tm,),
        in_specs=[pl.BlockSpec((None, 1, 2 * tm), lambda i: (i, 0, 0), memory_space=pltpu.SMEM),
                  tok, pl.BlockSpec((tm, LANES), lambda i: (i, 0)), pl.BlockSpec((1, D_MODEL), lambda i: (0, 0)),
                  pl.BlockSpec(memory_space=pl.ANY)],
        out_specs=tok,
        out_shape=jax.ShapeDtypeStruct((n, D_MODEL), F32),
        scratch_shapes=[pltpu.VMEM((tm, D_MODEL), F32), pltpu.VMEM((tm, D_MODEL), F32), pltpu.SemaphoreType.DMA(())],
        compiler_params=_cp(("arbitrary",), 32),
        name="moe_combine",
    )(dest.reshape(n // tm, 1, 2 * tm), h, rt, fg, ys)


def _moe_routed(h, rt, counts, g, wg, wu, wd, fg, *, tm, tm_expert):
    n = h.shape[0]
    counts = counts[0, :N_EXPERTS].astype(jnp.int32)
    group = (counts + tm_expert - 1) // tm_expert * tm_expert
    ends = jnp.cumsum(group)
    starts = ends - group
    e = rt[:, _R_E1:_R_E2 + 1].astype(jnp.int32)
    dest = (starts[e] + rt[:, _R_P1:_R_P2 + 1].astype(jnp.int32)).reshape(2 * n)
    n_tiles = (TOP_K * n) // tm_expert + N_EXPERTS
    tile_start = jnp.arange(n_tiles, dtype=jnp.int32) * tm_expert
    active = tile_start < ends[-1]
    last_active = ends[-1] // tm_expert - 1
    probe = jnp.minimum(tile_start, last_active * tm_expert)
    tile_expert = jnp.sum((probe[:, None] >= ends[None, :]).astype(jnp.int32), axis=1)
    tile_valid = jnp.where(active, jnp.clip(starts[tile_expert] + counts[tile_expert] - tile_start, 0, tm_expert), 0)
    xs = _dispatch(ends.astype(jnp.int32), dest, h, g, rows=n_tiles * tm_expert, tm=tm, tm_expert=tm_expert)
    ys = _experts(tile_expert, tile_valid.astype(jnp.int32), xs, wg, wu, wd, tm=tm_expert)
    return _combine(dest, h, rt, fg, ys, tm=tm)


def _pack_params(p):
    fq, fk, fv, ff, mq, mk, mv, mo, mi, mf = jnp.split(p['even_w_in'][0], list(EVEN_SPLITS), axis=1)
    gate_w = jnp.concatenate([ff, mi, mf, jnp.zeros((D_MODEL, LANES - _G_END), F32)], axis=1)
    gate_b = jnp.concatenate([p['fox_b_f'][0], p['mlstm_b_i'][0], p['mlstm_b_f'][0], jnp.zeros((LANES - _G_END,), F32)])
    row = lambda a: a.reshape(1, -1).astype(F32)
    return dict(
        even_w=jnp.concatenate([fq, fk, fv, mq, mk, mv, mo, gate_w], axis=1).astype(BF16),
        even_b=gate_b.reshape(1, LANES),
        even_w_out=p['even_w_out'][0].astype(BF16),
        mlstm_norm_g=row(p['mlstm_norm_g'][0]),
        norm_mix_g=[row(p['norm_mix_g'][l]) for l in range(2)],
        norm_cross_g=[row(p['norm_cross_g'][l]) for l in range(2)],
        norm_ffn_g=[row(p['norm_ffn_g'][l]) for l in range(2)],
        final_norm_g=row(p['final_norm_g']),
        mem_wq=p['mem_wq'].astype(BF16), mem_wo=p['mem_wo'].astype(BF16),
        ffn_wg=p['ffn_w_gate'][0].astype(BF16), ffn_wu=p['ffn_w_up'][0].astype(BF16), ffn_wd=p['ffn_w_down'][0].astype(BF16),
        odd_w_in=p['odd_w_in'][0].astype(BF16), odd_w_out=p['odd_w_out'][0].astype(BF16),
        conv_w=jnp.concatenate([p['conv_w'][0], jnp.zeros((SUBLANES - CONV_W, D_MODEL), F32)], axis=0),
        router_w=jnp.concatenate([p['router_w'][0], jnp.zeros((D_MODEL, LANES - N_EXPERTS), F32)], axis=1).astype(BF16),
        moe_wg=p['moe_w_gate'][0].astype(BF16), moe_wu=p['moe_w_up'][0].astype(BF16), moe_wd=p['moe_w_down'][0].astype(BF16),
    )


def _trunk(x, mem_k, mem_v, caches, w, *, tm, tm_light, tm_wide, chunk):
    b, t, _ = x.shape
    n = b * t
    flat = lambda a: a.reshape(n, a.shape[-1])
    per_b = lambda a: a.reshape(b, t, a.shape[-1])

    if caches is None:
        f0 = jnp.zeros((b, 1, LANES), F32)
    else:
        cache_k, cache_v, cache_lf, c0, n0, m0, conv_st = caches
        past_len = cache_k.shape[1]
        lf_pad = jnp.pad(cache_lf.astype(F32), ((0, 0), (0, 0), (0, LANES - FOX_HEADS)))
        ka_cache, f0 = _cache_prep(cache_k.reshape(b, past_len, FOX_W).astype(F32), lf_pad, tm=FOX_TILE)
    qa, ka, fk, fv, vb, mq, mk, mv, mo, gt = _even_in(flat(x), w['norm_mix_g'][0], w['even_w'], w['even_b'], f0,
                                                      tm=tm, tiles_per_batch=t // tm)
    if caches is None:
        fox_t = _fox_attention(per_b(qa), per_b(ka), per_b(vb), q0=0)
        c_ext0 = jnp.zeros((b, MLSTM_HEADS, MLSTM_EXT, LANES), F32)
        m_ext0 = jnp.zeros((b, SUBLANES, LANES), F32)
        past = jnp.zeros((b, SUBLANES, D_MODEL), F32)
    else:
        pad_t = lambda a: jnp.pad(per_b(a), ((0, 0), (0, FOX_TILE - t), (0, 0)))
        k_all = jnp.concatenate([ka_cache, pad_t(ka)], axis=1)
        v_all = jnp.concatenate([cache_v.reshape(b, past_len, FOX_W).astype(F32), pad_t(fv.reshape(n, FOX_W))], axis=1)
        fox_full = _fox_attention(pad_t(qa), k_all, v_all, q0=past_len // FOX_TILE)
        fox_t = jnp.swapaxes(fox_full[:, :, :t], 1, 2)
        c_ext0 = jnp.concatenate([
            jnp.pad(c0.astype(F32), ((0, 0), (0, 0), (0, 0), (0, LANES - MLSTM_DK))),
            jnp.broadcast_to(jnp.pad(n0.astype(F32), ((0, 0), (0, 0), (0, LANES - MLSTM_DK)))[:, :, None, :],
                             (b, MLSTM_HEADS, MLSTM_EXT - MLSTM_DV, LANES))],
            axis=2)
        m_ext0 = jnp.broadcast_to(jnp.pad(m0.astype(F32), ((0, 0), (0, SUBLANES - MLSTM_HEADS)))[:, :, None],
                                  (b, SUBLANES, LANES))
        past = jnp.pad(conv_st.astype(F32), ((0, 0), (SUBLANES - (CONV_W - 1), 0), (0, 0)))
    hm, c_ext, m_ext = _mlstm(per_b(mq), per_b(mk), per_b(mv), per_b(mo), per_b(gt), c_ext0, m_ext0,
                              w['mlstm_norm_g'], chunk=chunk)
    h = _even_out(x, fox_t, hm, w['even_w_out'], tm=tm_light, fox_transposed=caches is None)
    h = _cross(h, w['norm_cross_g'][0], w['mem_wq'][0], w['mem_wo'][0], mem_k[0], mem_v[0], tm=tm_light)
    h = _ffn(flat(h), w['norm_ffn_g'][0], w['ffn_wg'], w['ffn_wu'], w['ffn_wd'], tm=tm_wide)

    h, conv_new = _odd(per_b(h), w['norm_mix_g'][1], w['odd_w_in'], w['conv_w'], w['odd_w_out'], past, tm=tm)
    cross_args = (h, w['norm_cross_g'][1], w['mem_wq'][1], w['mem_wo'][1], mem_k[1], mem_v[1])
    experts = (w['moe_wg'], w['moe_wu'], w['moe_wd'], w['final_norm_g'])
    if n * TOP_K >= N_EXPERTS * MOE_EXPERT_TILE:
        h, rt, counts = _cross_route(*cross_args, w['norm_ffn_g'][1], w['router_w'], tm=tm_light)
        y = _moe_routed(flat(h), flat(rt), counts, w['norm_ffn_g'][1], *experts, tm=MOE_TOKEN_TILE, tm_expert=MOE_EXPERT_TILE)
    else:
        h = _cross(*cross_args, tm=tm_light)
        y = _moe(flat(h), w['norm_ffn_g'][1], w['router_w'], *experts, tm=tm_wide)

    states = (
        fk.reshape(1, b, t, FOX_HEADS, FOX_DH), fv.reshape(1, b, t, FOX_HEADS, FOX_DH),
        per_b(gt)[None, :, :, _G_FOX:_G_IG],
        c_ext[None, :, :, 0:MLSTM_DV, 0:MLSTM_DK], c_ext[None, :, :, MLSTM_DV, 0:MLSTM_DK], m_ext[None, :, 0:MLSTM_HEADS, 0],
        conv_new[None, :, SUBLANES - (CONV_W - 1):, :],
    )
    return per_b(y), states


def kernel(x_prompt, x_sample, mem_prompt, cache_fox_k, cache_fox_v, cache_fox_logf, state_mlstm_c, state_mlstm_n, state_mlstm_m, state_conv, cache_mem_k, cache_mem_v, norm_mix_g, norm_mem_g, norm_cross_g, norm_ffn_g, final_norm_g, even_w_in, fox_b_f, mlstm_b_i, mlstm_b_f, mlstm_norm_g, even_w_out, odd_w_in, conv_w, odd_w_out, mem_wq, mem_wk, mem_wv, mem_wo, ffn_w_gate, ffn_w_up, ffn_w_down, router_w, moe_w_gate, moe_w_up, moe_w_down):
    w = _pack_params(dict(
        norm_mix_g=norm_mix_g, norm_cross_g=norm_cross_g, norm_ffn_g=norm_ffn_g, final_norm_g=final_norm_g,
        even_w_in=even_w_in, fox_b_f=fox_b_f, mlstm_b_i=mlstm_b_i, mlstm_b_f=mlstm_b_f, mlstm_norm_g=mlstm_norm_g,
        even_w_out=even_w_out, odd_w_in=odd_w_in, conv_w=conv_w, odd_w_out=odd_w_out, mem_wq=mem_wq, mem_wo=mem_wo,
        ffn_w_gate=ffn_w_gate, ffn_w_up=ffn_w_up, ffn_w_down=ffn_w_down, router_w=router_w,
        moe_w_gate=moe_w_gate, moe_w_up=moe_w_up, moe_w_down=moe_w_down))

    bp, tp, _ = x_prompt.shape
    bs, ts, _ = x_sample.shape
    depth = norm_mem_g.shape[0]

    mem_k_p, mem_v_p, mem_kb, mem_vb = _memkv(mem_prompt.reshape(bp * MEM_TOKENS, D_MODEL),
                                              norm_mem_g.reshape(depth, 1, D_MODEL).astype(F32),
                                              mem_wk.astype(BF16), mem_wv.astype(BF16), tm=512)
    per_layer = lambda a, nb: a.reshape(depth, nb, MEM_TOKENS, MEM_W)
    y_prompt, st_p = _trunk(x_prompt, per_layer(mem_kb, bp), per_layer(mem_vb, bp), None, w,
                            tm=512, tm_light=1024, tm_wide=1024, chunk=MLSTM_CHUNK)

    caches = (cache_fox_k[0], cache_fox_v[0], cache_fox_logf[0], state_mlstm_c[0], state_mlstm_n[0],
              state_mlstm_m[0], state_conv[0])
    y_sample, st_s = _trunk(x_sample, per_layer(cache_mem_k.astype(BF16), bs), per_layer(cache_mem_v.astype(BF16), bs),
                            caches, w, tm=ts, tm_light=ts, tm_wide=bs * ts, chunk=ts)

    mem_shape = (depth, bp, MEM_TOKENS, MEM_HEADS, MEM_DH)
    return (y_prompt, y_sample) + st_p + (mem_k_p.reshape(mem_shape), mem_v_p.reshape(mem_shape)) + st_s
```
